```python
import math
import jax, jax.numpy as jnp
from jax import lax
import numpy as np

D_MODEL = 1024
BATCH = 8
SEQ = 4096
DEPTH = 4

GRID_W = 64
CTX_LEN = 256
SSM_WIDTH = D_MODEL // 2
SSM_GROUP = 16
SSM_GROUPS = SSM_WIDTH // SSM_GROUP
SSM_STATE = 64
ATTN_WIDTH = D_MODEL - SSM_WIDTH
HEAD_DIM = 64
N_HEADS = ATTN_WIDTH // HEAD_DIM
NA_ROWS_MAX = 8
NA_COLS = 16
IN_WIDTH = SSM_WIDTH + 3 * ATTN_WIDTH
D_FF = 2816
N_EXPERTS = 8
TOP_K = 2
D_FF_EXPERT = 3584
N_DENSE = (DEPTH + 1) // 2
N_MOE = DEPTH // 2
NORM_EPS = 1e-6
A_RE_MAX = -1e-4
DT_MIN = 1e-3
DT_MAX = 1e-1

kernel_name = "hymba_s5_natten_moe_dit"


def rms_norm(x, g):
    x32 = x.astype(jnp.float32)
    y = x32 * lax.rsqrt(jnp.mean(x32 * x32, axis=-1, keepdims=True) + NORM_EPS)
    return (y * g.astype(jnp.float32)).astype(x.dtype)


def modulate(h, shift, scale):
    return h * (1 + scale) + shift


def swiglu(t, w1, w3, w2):
    return (jax.nn.silu(t @ w1) * (t @ w3)) @ w2


def moe_swiglu(t, router, w1, w3, w2):
    logits = (t @ router).astype(jnp.float32)
    top_val, top_idx = lax.top_k(logits, TOP_K)
    top_w = jax.nn.softmax(top_val, axis=-1)
    gates = jnp.sum(jax.nn.one_hot(top_idx, N_EXPERTS, dtype=jnp.float32) * top_w[..., None], axis=1).astype(t.dtype)
    out = jnp.zeros_like(t)
    for e in range(N_EXPERTS):
        out = out + gates[:, e:e + 1] * swiglu(t, w1[e], w3[e], w2[e])
    return out


def ssm_discretize(a_re, a_im, log_step, b_re, b_im):
    f32 = jnp.float32
    a_re = jnp.minimum(a_re.astype(f32), A_RE_MAX)
    a_im = a_im.astype(f32)
    dt = jnp.exp(log_step.astype(f32))[:, None]
    mag = jnp.exp(a_re * dt)
    lam_re = mag * jnp.cos(a_im * dt)
    lam_im = mag * jnp.sin(a_im * dt)
    den = a_re * a_re + a_im * a_im
    z_re = ((lam_re - 1) * a_re + lam_im * a_im) / den
    z_im = (lam_im * a_re - (lam_re - 1) * a_im) / den
    b_re = b_re.astype(f32)
    b_im = b_im.astype(f32)
    bb_re = z_re[..., None] * b_re - z_im[..., None] * b_im
    bb_im = z_re[..., None] * b_im + z_im[..., None] * b_re
    return lam_re, lam_im, bb_re, bb_im


def complex_scan(lam_re, lam_im, b_re, b_im, reverse):
    n = b_re.shape[1]
    a_re = jnp.broadcast_to(lam_re, (1, n) + lam_re.shape)
    a_im = jnp.broadcast_to(lam_im, (1, n) + lam_im.shape)

    def combine(e1, e2):
        a1r, a1i, b1r, b1i = e1
        a2r, a2i, b2r, b2i = e2
        return (a2r * a1r - a2i * a1i, a2r * a1i + a2i * a1r,
                a2r * b1r - a2i * b1i + b2r, a2r * b1i + a2i * b1r + b2i)

    return lax.associative_scan(combine, (a_re, a_im, b_re, b_im), reverse=reverse, axis=1)


def ssm_readout(h_re, h_im, c_re, c_im):
    return jnp.einsum('blgp,ghp->blgh', h_re, c_re) - jnp.einsum('blgp,ghp->blgh', h_im, c_im)


def ssm_mixer(u, u_c, a_re, a_im, log_step, b_re, b_im, c_re, c_im, d_skip, w_glu, b_glu, with_ctx_out):
    f32 = jnp.float32
    bsz, seq, _ = u.shape
    n_ctx = u_c.shape[1]
    ug = u.astype(f32).reshape(bsz, seq, SSM_GROUPS, SSM_GROUP)
    ucg = u_c.astype(f32).reshape(bsz, n_ctx, SSM_GROUPS, SSM_GROUP)
    d = d_skip.astype(f32).reshape(SSM_GROUPS, SSM_GROUP)
    y = ug * d
    yc = ucg * d if with_ctx_out else None
    for direction in range(2):
        reverse = direction == 1
        lam_re, lam_im, bb_re, bb_im = ssm_discretize(a_re[direction], a_im[direction], log_step[direction],
                                                      b_re[direction], b_im[direction])
        cre = c_re[direction].astype(f32)
        cim = c_im[direction].astype(f32)
        dc_re = jnp.einsum('blgh,gph->blgp', ucg, bb_re)
        dc_im = jnp.einsum('blgh,gph->blgp', ucg, bb_im)
        _, _, hc_re, hc_im = complex_scan(lam_re, lam_im, dc_re, dc_im, reverse)
        end = 0 if reverse else n_ctx - 1
        h0_re = hc_re[:, end][:, None]
        h0_im = hc_im[:, end][:, None]
        dx_re = jnp.einsum('blgh,gph->blgp', ug, bb_re)
        dx_im = jnp.einsum('blgh,gph->blgp', ug, bb_im)
        p_re, p_im, s_re, s_im = complex_scan(lam_re, lam_im, dx_re, dx_im, reverse)
        h_re = p_re * h0_re - p_im * h0_im + s_re
        h_im = p_re * h0_im + p_im * h0_re + s_im
        y = y + ssm_readout(h_re, h_im, cre, cim)
        if with_ctx_out:
            yc = yc + ssm_readout(hc_re, hc_im, cre, cim)

    def glu(yy, n):
        g = jax.nn.gelu(yy.reshape(bsz, n, SSM_WIDTH)).astype(u.dtype)
        return g * jax.nn.sigmoid(g @ w_glu + b_glu)

    out = glu(y, seq)
    out_c = glu(yc, n_ctx) if with_ctx_out else None
    return out, out_c


def context_attention(qc, kc, vc):
    s = jnp.einsum('bqhd,bkhd->bhqk', qc, kc).astype(jnp.float32) * (HEAD_DIM ** -0.5)
    p = jax.nn.softmax(s, axis=-1).astype(vc.dtype)
    return jnp.einsum('bhqk,bkhd->bqhd', p, vc)


def neighbourhood_attention(q, k, v, kc, vc, rpb):
    bsz, seq = q.shape[0], q.shape[1]
    rows = seq // GRID_W
    win_r = min(NA_ROWS_MAX, rows)
    n_lat = win_r * NA_COLS
    grid = (bsz, rows, GRID_W, N_HEADS, HEAD_DIM)
    qg, kg, vg = q.reshape(grid), k.reshape(grid), v.reshape(grid)
    col = jnp.arange(GRID_W)
    col_start = jnp.clip(col - NA_COLS // 2, 0, GRID_W - NA_COLS)
    col_idx = col_start[:, None] + jnp.arange(NA_COLS)[None, :]
    col_rel = col_idx - col[:, None] + NA_COLS - 1
    scale = HEAD_DIM ** -0.5

    def row_block(r):
        r_start = jnp.clip(r - win_r // 2, 0, rows - win_r)
        q_r = lax.dynamic_index_in_dim(qg, r, axis=1, keepdims=False)
        k_band = lax.dynamic_slice_in_dim(kg, r_start, win_r, axis=1)
        v_band = lax.dynamic_slice_in_dim(vg, r_start, win_r, axis=1)
        k_win = k_band[:, :, col_idx]
        v_win = v_band[:, :, col_idx]
        row_rel = r_start + jnp.arange(win_r) - r + NA_ROWS_MAX - 1
        bias = jnp.transpose(rpb[:, row_rel][:, :, col_rel], (0, 2, 1, 3)).astype(jnp.float32)
        s_lat = jnp.einsum('bchd,brcjhd->bhcrj', q_r, k_win).astype(jnp.float32) * scale + bias
        s_ctx = jnp.einsum('bchd,bkhd->bhck', q_r, kc).astype(jnp.float32) * scale
        s = jnp.concatenate([s_lat.reshape(bsz, N_HEADS, GRID_W, n_lat), s_ctx], axis=-1)
        p = jax.nn.softmax(s, axis=-1).astype(v.dtype)
        p_lat = p[..., :n_lat].reshape(bsz, N_HEADS, GRID_W, win_r, NA_COLS)
        p_ctx = p[..., n_lat:]
        return (jnp.einsum('bhcrj,brcjhd->bchd', p_lat, v_win)
                + jnp.einsum('bhck,bkhd->bchd', p_ctx, vc))

    out = lax.map(row_block, jnp.arange(rows))
    return jnp.transpose(out, (1, 0, 2, 3, 4)).reshape(bsz, seq, ATTN_WIDTH)


def setup_inputs(seed: int = 0) -> dict:
    key = jax.random.key(seed)
    ks = iter(jax.random.split(key, 32))
    f32 = jnp.float32

    def nrm(shape, s):
        return jax.random.normal(next(ks), shape, f32) * s

    D = D_MODEL
    G, P, H = SSM_GROUPS, SSM_STATE, SSM_GROUP
    n_idx = jnp.arange(P, dtype=f32)
    return {
        "x": nrm((BATCH, SEQ, D), 1.0),
        "c": nrm((BATCH, D), 1.0),
        "ctx": nrm((BATCH, CTX_LEN, D), 1.0),
        "c_ctx": nrm((D,), 1.0),
        "w_mod": nrm((DEPTH, D, 6 * D), 0.5 * D ** -0.5),
        "b_mod": nrm((DEPTH, 6 * D), 0.02),
        "g_mix": 1.0 + nrm((DEPTH, D), 0.02),
        "g_ffn": 1.0 + nrm((DEPTH, D), 0.02),
        "w_in": nrm((DEPTH, D, IN_WIDTH), D ** -0.5),
        "w_out": nrm((DEPTH, SSM_WIDTH + ATTN_WIDTH, D), (SSM_WIDTH + ATTN_WIDTH) ** -0.5),
        "ssm_a_re": -0.5 + nrm((DEPTH, 2, G, P), 0.01),
        "ssm_a_im": jnp.pi * n_idx + nrm((DEPTH, 2, G, P), 0.01),
        "ssm_log_step": jax.random.uniform(next(ks), (DEPTH, 2, G), f32, math.log(DT_MIN), math.log(DT_MAX)),
        "ssm_b_re": nrm((DEPTH, 2, G, P, H), (2 * H) ** -0.5),
        "ssm_b_im": nrm((DEPTH, 2, G, P, H), (2 * H) ** -0.5),
        "ssm_c_re": nrm((DEPTH, 2, G, H, P), P ** -0.5),
        "ssm_c_im": nrm((DEPTH, 2, G, H, P), P ** -0.5),
        "ssm_d": nrm((DEPTH, SSM_WIDTH), 1.0),
        "glu_w": nrm((DEPTH, SSM_WIDTH, SSM_WIDTH), SSM_WIDTH ** -0.5),
        "glu_b": nrm((DEPTH, SSM_WIDTH), 0.02),
        "na_rpb": nrm((DEPTH, N_HEADS, 2 * NA_ROWS_MAX - 1, 2 * NA_COLS - 1), 0.1),
        "ffn_w1": nrm((N_DENSE, D, D_FF), D ** -0.5),
        "ffn_w3": nrm((N_DENSE, D, D_FF), D ** -0.5),
        "ffn_w2": nrm((N_DENSE, D_FF, D), D_FF ** -0.5),
        "moe_router": nrm((N_MOE, D, N_EXPERTS), D ** -0.5),
        "moe_w1": nrm((N_MOE, N_EXPERTS, D, D_FF_EXPERT), D ** -0.5),
        "moe_w3": nrm((N_MOE, N_EXPERTS, D, D_FF_EXPERT), D ** -0.5),
        "moe_w2": nrm((N_MOE, N_EXPERTS, D_FF_EXPERT, D), D_FF_EXPERT ** -0.5),
        "g_final": 1.0 + nrm((D,), 0.02),
    }


def reference(x, c, ctx, c_ctx, w_mod, b_mod, g_mix, g_ffn, w_in, w_out, ssm_a_re, ssm_a_im, ssm_log_step,
              ssm_b_re, ssm_b_im, ssm_c_re, ssm_c_im, ssm_d, glu_w, glu_b, na_rpb, ffn_w1, ffn_w3, ffn_w2,
              moe_router, moe_w1, moe_w3, moe_w2, g_final):
    bsz, seq, d = x.shape
    n_ctx = ctx.shape[1]
    silu_c = jax.nn.silu(c)
    silu_cc = jax.nn.silu(c_ctx)
    split_cols = [SSM_WIDTH, SSM_WIDTH + ATTN_WIDTH, SSM_WIDTH + 2 * ATTN_WIDTH]

    def heads(t):
        return t.reshape(t.shape[0], t.shape[1], N_HEADS, HEAD_DIM)

    xc = ctx
    for l in range(DEPTH):
        last = l == DEPTH - 1
        sh1, sc1, gt1, sh2, sc2, gt2 = jnp.split((silu_c @ w_mod[l] + b_mod[l])[:, None, :], 6, axis=-1)
        sh1c, sc1c, gt1c, sh2c, sc2c, gt2c = jnp.split(silu_cc @ w_mod[l] + b_mod[l], 6, axis=-1)

        h = modulate(rms_norm(x, g_mix[l]), sh1, sc1)
        hc = modulate(rms_norm(xc, g_mix[l]), sh1c, sc1c)
        u, q, k, v = jnp.split(h @ w_in[l], split_cols, axis=-1)
        if last:
            uc = hc @ w_in[l][:, :SSM_WIDTH]
            kc, vc = jnp.split(hc @ w_in[l][:, SSM_WIDTH + ATTN_WIDTH:], 2, axis=-1)
            qc = None
        else:
            uc, qc, kc, vc = jnp.split(hc @ w_in[l], split_cols, axis=-1)

        y_ssm, yc_ssm = ssm_mixer(u, uc, ssm_a_re[l], ssm_a_im[l], ssm_log_step[l], ssm_b_re[l], ssm_b_im[l],
                                  ssm_c_re[l], ssm_c_im[l], ssm_d[l], glu_w[l], glu_b[l], not last)
        kc_h, vc_h = heads(kc), heads(vc)
        y_att = neighbourhood_attention(heads(q), heads(k), heads(v), kc_h, vc_h, na_rpb[l])
        x = x + gt1 * (jnp.concatenate([y_ssm, y_att], axis=-1) @ w_out[l])
        if not last:
            yc_att = context_attention(heads(qc), kc_h, vc_h).reshape(bsz, n_ctx, ATTN_WIDTH)
            xc = xc + gt1c * (jnp.concatenate([yc_ssm, yc_att], axis=-1) @ w_out[l])

        tokens = modulate(rms_norm(x, g_ffn[l]), sh2, sc2).reshape(bsz * seq, d)
        if not last:
            tokens_c = modulate(rms_norm(xc, g_ffn[l]), sh2c, sc2c).reshape(bsz * n_ctx, d)
            tokens = jnp.concatenate([tokens, tokens_c], axis=0)
        if l % 2 == 0:
            f = swiglu(tokens, ffn_w1[l // 2], ffn_w3[l // 2], ffn_w2[l // 2])
        else:
            f = moe_swiglu(tokens, moe_router[l // 2], moe_w1[l // 2], moe_w3[l // 2], moe_w2[l // 2])
        x = x + gt2 * f[:bsz * seq].reshape(bsz, seq, d)
        if not last:
            xc = xc + gt2c * f[bsz * seq:].reshape(bsz, n_ctx, d)

    return rms_norm(x, g_final)
```

```python
import functools
import math

import jax
import jax.numpy as jnp
from jax import lax
from jax.experimental import pallas as pl
from jax.experimental.pallas import tpu as pltpu

F32 = jnp.float32
BF16 = jnp.bfloat16

GRID_W = 64
SSM_GROUP = 16
SSM_STATE = 64
HEAD_DIM = 64
NA_ROWS_MAX = 8
NA_COLS = 16
N_EXPERTS = 8
NORM_EPS = 1e-6
A_RE_MAX = -1e-4
MASK_VALUE = -1e30

LANES = 128
SUBLANES = 8
V7X_VMEM_LIMIT_BYTES = 56 * 1024 * 1024

MOD_ROWS = 16
SSM_BLOCK_IN = 128
SSM_BLOCK_STATE = 512


def _params(*semantics):
    return pltpu.CompilerParams(dimension_semantics=semantics, vmem_limit_bytes=V7X_VMEM_LIMIT_BYTES)


def _token_tile(rows_per_batch):
    for parts in range(1, rows_per_batch + 1):
        if rows_per_batch % parts == 0:
            tm = rows_per_batch // parts
            if tm <= 1152 and tm % 16 == 0:
                return tm
    raise ValueError("no token tile for %d rows" % rows_per_batch)


def _chunk(total, target):
    best = None
    for c in range(LANES, min(total, target) + 1, LANES):
        if total % c == 0:
            best = c
    if best is None:
        raise ValueError("no lane-aligned chunk for %d" % total)
    return best


def _mod_kernel(c_ref, w_ref, b_ref, o_ref):
    a = jax.nn.silu(c_ref[...])
    o_ref[...] = jnp.dot(a, w_ref[...], preferred_element_type=F32) + b_ref[...]


def _mod_table(cvec, w_mod, b_mod):
    depth, d, n = w_mod.shape
    tn = _chunk(n, 1536)
    return pl.pallas_call(
        _mod_kernel,
        grid=(depth, n // tn),
        in_specs=[
            pl.BlockSpec((MOD_ROWS, d), lambda l, j: (0, 0)),
            pl.BlockSpec((None, d, tn), lambda l, j: (l, 0, j)),
            pl.BlockSpec((None, 1, tn), lambda l, j: (l, 0, j)),
        ],
        out_specs=pl.BlockSpec((None, MOD_ROWS, tn), lambda l, j: (l, 0, j)),
        out_shape=jax.ShapeDtypeStruct((depth, MOD_ROWS, n), F32),
        compiler_params=_params("arbitrary", "arbitrary"),
        name="mod_table",
    )(cvec, w_mod, b_mod.reshape(depth, 1, n))


def _mod_rows(mod_ref, slab, tile_idx, tm, tpb, seq, ctx_row, d):
    b = tile_idx // tpb
    v_b = mod_ref[pl.ds(b, 1), slab * d:(slab + 1) * d]
    v_c = mod_ref[ctx_row:ctx_row + 1, slab * d:(slab + 1) * d]
    row = (tile_idx % tpb) * tm + lax.broadcasted_iota(jnp.int32, (tm, 1), 0)
    return jnp.where(row >= seq, v_c, v_b)


def _norm_mod(x, g, mod_ref, slab, tile_idx, tpb, seq, ctx_row):
    tm, d = x.shape
    shift = _mod_rows(mod_ref, slab, tile_idx, tm, tpb, seq, ctx_row, d)
    scale = _mod_rows(mod_ref, slab + 1, tile_idx, tm, tpb, seq, ctx_row, d)
    rs = lax.rsqrt(jnp.mean(x * x, axis=-1, keepdims=True) + NORM_EPS)
    return (x * rs) * g * (1.0 + scale) + shift


def _in_kernel(x_ref, g_ref, mod_ref, w_ref, u_ref, q_ref, k_ref, v_ref, *, tpb, seq, ctx_row, ssm_w, att_w):
    i = pl.program_id(0)
    h = _norm_mod(x_ref[...], g_ref[...], mod_ref, 0, i, tpb, seq, ctx_row).astype(BF16)
    c1 = ssm_w + att_w
    c2 = c1 + att_w
    u_ref[...] = jnp.dot(h, w_ref[:, 0:ssm_w], preferred_element_type=F32)
    q_ref[...] = (jnp.dot(h, w_ref[:, ssm_w:c1], preferred_element_type=F32) * (HEAD_DIM ** -0.5)).astype(BF16)
    k_ref[...] = jnp.dot(h, w_ref[:, c1:c2], preferred_element_type=F32).astype(BF16)
    v_ref[...] = jnp.dot(h, w_ref[:, c2:c2 + att_w], preferred_element_type=F32).astype(BF16)


def _in_proj(x_all, g, mod_l, w_in, *, tm, tpb, seq, ctx_row, ssm_w, att_w):
    t, d = x_all.shape
    n = w_in.shape[1]
    row_block = lambda width: pl.BlockSpec((tm, width), lambda i: (i, 0))
    return pl.pallas_call(
        functools.partial(_in_kernel, tpb=tpb, seq=seq, ctx_row=ctx_row, ssm_w=ssm_w, att_w=att_w),
        grid=(t // tm,),
        in_specs=[
            row_block(d),
            pl.BlockSpec((1, d), lambda i: (0, 0)),
            pl.BlockSpec(mod_l.shape, lambda i: (0, 0)),
            pl.BlockSpec((d, n), lambda i: (0, 0)),
        ],
        out_specs=[row_block(ssm_w), row_block(att_w), row_block(att_w), row_block(att_w)],
        out_shape=[
            jax.ShapeDtypeStruct((t, ssm_w), F32),
            jax.ShapeDtypeStruct((t, att_w), BF16),
            jax.ShapeDtypeStruct((t, att_w), BF16),
            jax.ShapeDtypeStruct((t, att_w), BF16),
        ],
        compiler_params=_params("arbitrary"),
        name="in_proj",
    )(x_all, g.reshape(1, d), mod_l, w_in)


def _att_kernel(q_ref, k_ref, v_ref, bias_ref, o_ref, *, seq, n_ctx, rows, win):
    lane = lax.broadcasted_iota(jnp.int32, (1, LANES), 1)
    head_lanes = [lane < HEAD_DIM, lane >= HEAD_DIM]
    head_mask = [m.astype(BF16) for m in head_lanes]
    trans_b = (((1,), (1,)), ((), ()))

    def attend(q, parts):
        outs = []
        for hh in range(2):
            qm = q * head_mask[hh]
            scores = []
            for keys, _, bias in parts:
                s = lax.dot_general(qm, keys, trans_b, preferred_element_type=F32)
                if bias is not None:
                    s = s + bias[hh]
                scores.append(s)
            m = functools.reduce(jnp.maximum, [jnp.max(s, axis=-1, keepdims=True) for s in scores])
            probs = [jnp.exp(s - m) for s in scores]
            den = functools.reduce(lambda a, b: a + b, [jnp.sum(p, axis=-1, keepdims=True) for p in probs])
            o = functools.reduce(
                lambda a, b: a + b,
                [jnp.dot(p.astype(BF16), vals, preferred_element_type=F32) for p, (_, vals, _) in zip(probs, parts)])
            outs.append(o / den)
        return jnp.where(head_lanes[0], outs[0], outs[1])

    def latent_row(r, carry):
        r0 = jnp.clip(r - win // 2, 0, rows - win)
        var = r - r0
        q0 = pl.multiple_of(r * GRID_W, GRID_W)
        k0 = pl.multiple_of(r0 * GRID_W, GRID_W)
        q = q_ref[pl.ds(q0, GRID_W), :]
        kw = k_ref[pl.ds(k0, win * GRID_W), :]
        vw = v_ref[pl.ds(k0, win * GRID_W), :]
        kc = k_ref[seq:seq + n_ctx, :]
        vc = v_ref[seq:seq + n_ctx, :]
        bias = [bias_ref[0, var], bias_ref[1, var]]
        o = attend(q, [(kw, vw, bias), (kc, vc, None)])
        o_ref[pl.ds(q0, GRID_W), :] = o.astype(o_ref.dtype)
        return carry

    lax.fori_loop(0, rows, latent_row, 0)

    kc = k_ref[seq:seq + n_ctx, :]
    vc = v_ref[seq:seq + n_ctx, :]
    o_ref[seq:seq + n_ctx, :] = attend(q_ref[seq:seq + n_ctx, :], [(kc, vc, None)]).astype(o_ref.dtype)


def _attention(q, k, v, bias, *, batch, rpb_rows, seq, n_ctx):
    t, att_w = q.shape
    rows = seq // GRID_W
    win = min(NA_ROWS_MAX, rows)
    n_pairs = att_w // LANES
    blk = pl.BlockSpec((rpb_rows, LANES), lambda b, p: (b, p))
    return pl.pallas_call(
        functools.partial(_att_kernel, seq=seq, n_ctx=n_ctx, rows=rows, win=win),
        grid=(batch, n_pairs),
        in_specs=[blk, blk, blk,
                  pl.BlockSpec((2, win, GRID_W, win * GRID_W), lambda b, p: (p, 0, 0, 0))],
        out_specs=blk,
        out_shape=jax.ShapeDtypeStruct((t, att_w), BF16),
        compiler_params=_params("arbitrary", "arbitrary"),
        name="attention",
    )(q, k, v, bias)


def _attention_bias(rpb, rows):
    win = min(NA_ROWS_MAX, rows)
    col = jnp.arange(GRID_W)
    col_start = jnp.clip(col - NA_COLS // 2, 0, GRID_W - NA_COLS)
    kcol = jnp.arange(GRID_W)
    valid = (kcol[None, :] >= col_start[:, None]) & (kcol[None, :] < col_start[:, None] + NA_COLS)
    col_rel = jnp.clip(kcol[None, :] - col[:, None] + NA_COLS - 1, 0, 2 * NA_COLS - 2)
    var = jnp.arange(win)
    rr = jnp.arange(win)
    row_rel = jnp.clip(rr[None, :] - var[:, None] + NA_ROWS_MAX - 1, 0, 2 * NA_ROWS_MAX - 2)
    tab = rpb.astype(F32)[:, row_rel[:, None, :, None], col_rel[None, :, None, :]]
    tab = jnp.where(valid[None, None, :, None, :], tab, MASK_VALUE)
    return tab.reshape(rpb.shape[0], win, GRID_W, win * GRID_W)


def _ssm_kernel(*refs, lc, reverse, first):
    if first:
        u_ref, lre_ref, lim_ref, bc_ref, cc_ref, dsk_ref, y_ref, hre_s, him_s, xre_s, xim_s = refs
        yprev_ref = None
    else:
        u_ref, yprev_ref, lre_ref, lim_ref, bc_ref, cc_ref, y_ref, hre_s, him_s, xre_s, xim_s = refs
    nb = u_ref.shape[1]
    width = u_ref.shape[2]
    n_state = xre_s.shape[2]
    n_blocks = width // SSM_BLOCK_IN

    @pl.when(pl.program_id(0) == 0)
    def _():
        hre_s[...] = jnp.zeros_like(hre_s)
        him_s[...] = jnp.zeros_like(him_s)

    u = u_ref[...].reshape(lc * nb, width)
    ub = u.astype(BF16)
    for jb in range(n_blocks):
        xb = jnp.dot(ub[:, jb * SSM_BLOCK_IN:(jb + 1) * SSM_BLOCK_IN], bc_ref[jb], preferred_element_type=F32)
        s0 = jb * SSM_BLOCK_STATE
        xre_s[:, :, s0:s0 + SSM_BLOCK_STATE] = xb[:, :SSM_BLOCK_STATE].reshape(lc, nb, SSM_BLOCK_STATE)
        xim_s[:, :, s0:s0 + SSM_BLOCK_STATE] = xb[:, SSM_BLOCK_STATE:].reshape(lc, nb, SSM_BLOCK_STATE)

    cw = 512
    for cb in range(n_state // cw):
        c0 = cb * cw
        lr = jnp.broadcast_to(lre_ref[:, c0:c0 + cw], (nb, cw))
        li = jnp.broadcast_to(lim_ref[:, c0:c0 + cw], (nb, cw))

        def step(t, carry, c0=c0, lr=lr, li=li):
            hr, hi = carry
            tt = (lc - 1 - t) if reverse else t
            nr = lr * hr - li * hi + xre_s[tt, :, c0:c0 + cw]
            ni = lr * hi + li * hr + xim_s[tt, :, c0:c0 + cw]
            xre_s[tt, :, c0:c0 + cw] = nr
            xim_s[tt, :, c0:c0 + cw] = ni
            return nr, ni

        hr, hi = lax.fori_loop(0, lc, step, (hre_s[:, c0:c0 + cw], him_s[:, c0:c0 + cw]), unroll=2)
        hre_s[:, c0:c0 + cw] = hr
        him_s[:, c0:c0 + cw] = hi

    if first:
        base = u * dsk_ref[...]
    else:
        base = yprev_ref[...].reshape(lc * nb, width)
    for jb in range(n_blocks):
        s0 = jb * SSM_BLOCK_STATE
        h_re = xre_s[:, :, s0:s0 + SSM_BLOCK_STATE].reshape(lc * nb, SSM_BLOCK_STATE).astype(BF16)
        h_im = xim_s[:, :, s0:s0 + SSM_BLOCK_STATE].reshape(lc * nb, SSM_BLOCK_STATE).astype(BF16)
        yj = (jnp.dot(h_re, cc_ref[jb, 0:SSM_BLOCK_STATE, :], preferred_element_type=F32)
              + jnp.dot(h_im, cc_ref[jb, SSM_BLOCK_STATE:, :], preferred_element_type=F32))
        o0 = jb * SSM_BLOCK_IN
        y_ref[:, :, o0:o0 + SSM_BLOCK_IN] = (base[:, o0:o0 + SSM_BLOCK_IN] + yj).reshape(lc, nb, SSM_BLOCK_IN)


def _ssm_direction(u_tm, y_prev, lam_re, lam_im, bc, cc, d_skip, *, lc, seq, n_ctx, reverse):
    total, nb, width = u_tm.shape
    n_state = lam_re.shape[1]
    n_ch = total // lc
    n_lat_ch = seq // lc
    n_ctx_ch = n_ctx // lc
    first = y_prev is None
    if reverse:
        chunk = lambda i: (n_ch - 1 - i, 0, 0)
    else:
        chunk = lambda i: (jnp.where(i < n_ctx_ch, n_lat_ch + i, i - n_ctx_ch), 0, 0)
    seq_block = pl.BlockSpec((lc, nb, width), chunk)
    whole = lambda a: pl.BlockSpec(a.shape, lambda i: (0,) * a.ndim)
    args = [u_tm] + ([] if first else [y_prev]) + [lam_re, lam_im, bc, cc] + ([d_skip] if first else [])
    in_specs = [seq_block] + ([] if first else [seq_block]) + [whole(lam_re), whole(lam_im), whole(bc), whole(cc)]
    if first:
        in_specs.append(whole(d_skip))
    return pl.pallas_call(
        functools.partial(_ssm_kernel, lc=lc, reverse=reverse, first=first),
        grid=(n_ch,),
        in_specs=in_specs,
        out_specs=seq_block,
        out_shape=jax.ShapeDtypeStruct((total, nb, width), F32),
        scratch_shapes=[
            pltpu.VMEM((nb, n_state), F32),
            pltpu.VMEM((nb, n_state), F32),
            pltpu.VMEM((lc, nb, n_state), F32),
            pltpu.VMEM((lc, nb, n_state), F32),
        ],
        compiler_params=_params("arbitrary"),
        name="ssm_bwd" if reverse else "ssm_fwd",
    )(*args)


def _ssm_params(a_re, a_im, log_step, b_re, b_im, c_re, c_im):
    g, p, h = b_re.shape
    a_re = jnp.minimum(a_re.astype(F32), A_RE_MAX)
    a_im = a_im.astype(F32)
    dt = jnp.exp(log_step.astype(F32))[:, None]
    mag = jnp.exp(a_re * dt)
    lam_re = mag * jnp.cos(a_im * dt)
    lam_im = mag * jnp.sin(a_im * dt)
    den = a_re * a_re + a_im * a_im
    z_re = ((lam_re - 1) * a_re + lam_im * a_im) / den
    z_im = (lam_im * a_re - (lam_re - 1) * a_im) / den
    b_re = b_re.astype(F32)
    b_im = b_im.astype(F32)
    bb_re = z_re[..., None] * b_re - z_im[..., None] * b_im
    bb_im = z_re[..., None] * b_im + z_im[..., None] * b_re
    gpb = SSM_BLOCK_IN // h
    nblk = g // gpb
    eye = jnp.eye(gpb, dtype=F32)

    def pack_in(bb):
        bb = bb.reshape(nblk, gpb, p, h)
        return jnp.einsum("ngph,gk->nghkp", bb, eye).reshape(nblk, gpb * h, gpb * p)

    def pack_out(c):
        c = c.reshape(nblk, gpb, h, p)
        return jnp.einsum("nghp,gk->ngpkh", c, eye).reshape(nblk, gpb * p, gpb * h)

    bc = jnp.concatenate([pack_in(bb_re), pack_in(bb_im)], axis=2).astype(BF16)
    cc = jnp.concatenate([pack_out(c_re.astype(F32)), -pack_out(c_im.astype(F32))], axis=1).astype(BF16)
    return lam_re.reshape(1, g * p), lam_im.reshape(1, g * p), bc, cc


def _mix_kernel(y_ref, att_ref, x_ref, mod_ref, gw_ref, gb_ref, wo_ref, o_ref, *, tpb, seq, ctx_row):
    i = pl.program_id(0)
    tm, d = x_ref.shape
    ssm_w = y_ref.shape[1]
    g = jax.nn.gelu(y_ref[...])
    z = jnp.dot(g.astype(BF16), gw_ref[...], preferred_element_type=F32) + gb_ref[...]
    s = g * jax.nn.sigmoid(z)
    o = (jnp.dot(s.astype(BF16), wo_ref[0:ssm_w, :], preferred_element_type=F32)
         + jnp.dot(att_ref[...], wo_ref[ssm_w:, :], preferred_element_type=F32))
    gate = _mod_rows(mod_ref, 2, i, tm, tpb, seq, ctx_row, d)
    o_ref[...] = x_ref[...] + gate * o


def _mix_out(y_ssm, y_att, x_all, mod_l, glu_w, glu_b, w_out, *, tm, tpb, seq, ctx_row):
    t, d = x_all.shape
    ssm_w = y_ssm.shape[1]
    att_w = y_att.shape[1]
    row_block = lambda width: pl.BlockSpec((tm, width), lambda i: (i, 0))
    whole = lambda a: pl.BlockSpec(a.shape, lambda i: (0, 0))
    gb = glu_b.reshape(1, ssm_w)
    return pl.pallas_call(
        functools.partial(_mix_kernel, tpb=tpb, seq=seq, ctx_row=ctx_row),
        grid=(t // tm,),
        in_specs=[row_block(ssm_w), row_block(att_w), row_block(d), whole(mod_l), whole(glu_w), whole(gb),
                  whole(w_out)],
        out_specs=row_block(d),
        out_shape=jax.ShapeDtypeStruct((t, d), F32),
        compiler_params=_params("arbitrary"),
        name="mix_out",
    )(y_ssm, y_att, x_all, mod_l, glu_w, gb, w_out)


def _ffn_kernel(x_ref, g_ref, mod_ref, w1_ref, w3_ref, w2_ref, o_ref, t_s, acc_s, *, tpb, seq, ctx_row, nf):
    i = pl.program_id(0)
    j = pl.program_id(1)
    tm, d = x_ref.shape

    @pl.when(j == 0)
    def _():
        t_s[...] = _norm_mod(x_ref[...], g_ref[...], mod_ref, 3, i, tpb, seq, ctx_row).astype(BF16)
        acc_s[...] = jnp.zeros_like(acc_s)

    t = t_s[...]
    h1 = jnp.dot(t, w1_ref[...], preferred_element_type=F32)
    h3 = jnp.dot(t, w3_ref[...], preferred_element_type=F32)
    a = (jax.nn.silu(h1) * h3).astype(BF16)
    acc_s[...] += jnp.dot(a, w2_ref[...], preferred_element_type=F32)

    @pl.when(j == nf - 1)
    def _():
        gate = _mod_rows(mod_ref, 5, i, tm, tpb, seq, ctx_row, d)
        o_ref[...] = x_ref[...] + gate * acc_s[...]


def _dense_ffn(x_all, g, mod_l, w1, w3, w2, *, tm, tpb, seq, ctx_row):
    t, d = x_all.shape
    dff = w1.shape[1]
    tf = _chunk(dff, 256)
    nf = dff // tf
    return pl.pallas_call(
        functools.partial(_ffn_kernel, tpb=tpb, seq=seq, ctx_row=ctx_row, nf=nf),
        grid=(t // tm, nf),
        in_specs=[
            pl.BlockSpec((tm, d), lambda i, j: (i, 0)),
            pl.BlockSpec((1, d), lambda i, j: (0, 0)),
            pl.BlockSpec(mod_l.shape, lambda i, j: (0, 0)),
            pl.BlockSpec((d, tf), lambda i, j: (0, j)),
            pl.BlockSpec((d, tf), lambda i, j: (0, j)),
            pl.BlockSpec((tf, d), lambda i, j: (j, 0)),
        ],
        out_specs=pl.BlockSpec((tm, d), lambda i, j: (i, 0)),
        out_shape=jax.ShapeDtypeStruct((t, d), F32),
        scratch_shapes=[pltpu.VMEM((tm, d), BF16), pltpu.VMEM((tm, d), F32)],
        compiler_params=_params("arbitrary", "arbitrary"),
        name="dense_ffn",
    )(x_all, g.reshape(1, d), mod_l, w1, w3, w2)


def _route_kernel(x_ref, g_ref, mod_ref, r_ref, t_ref, route_ref, *, tpb, seq, ctx_row):
    i = pl.program_id(0)
    t = _norm_mod(x_ref[...], g_ref[...], mod_ref, 3, i, tpb, seq, ctx_row)
    t_ref[...] = t
    th = t.astype(BF16)
    tl = (t - th.astype(F32)).astype(BF16)
    r = r_ref[...]
    rh = r.astype(BF16)
    rl = (r - rh.astype(F32)).astype(BF16)
    dot = lambda a, b: jnp.dot(a, b, preferred_element_type=F32)
    logits = dot(th, rh) + (dot(th, rl) + dot(tl, rh)) + dot(tl, rl)
    tm = logits.shape[0]
    lane = lax.broadcasted_iota(jnp.int32, (tm, LANES), 1)
    lane_f = lane.astype(F32)
    neg_inf = jnp.float32(-jnp.inf)
    lg = jnp.where(lane < N_EXPERTS, logits, neg_inf)
    m1 = jnp.max(lg, axis=-1, keepdims=True)
    i1 = jnp.min(jnp.where(lg == m1, lane_f, float(LANES)), axis=-1, keepdims=True)
    lg2 = jnp.where(lane_f == i1, neg_inf, lg)
    m2 = jnp.max(lg2, axis=-1, keepdims=True)
    i2 = jnp.min(jnp.where(lg2 == m2, lane_f, float(LANES)), axis=-1, keepdims=True)
    e = jnp.exp(m2 - m1)
    g1 = 1.0 / (1.0 + e)
    g2 = e / (1.0 + e)
    route_ref[...] = jnp.where(lane == 0, i1, jnp.where(lane == 1, i2, jnp.where(lane == 2, g1, jnp.where(
        lane == 3, g2, 0.0))))


def _route(x_all, g, mod_l, router, *, tm, tpb, seq, ctx_row):
    t, d = x_all.shape
    r_pad = jnp.zeros((d, LANES), F32).at[:, :router.shape[1]].set(router.astype(F32))
    return pl.pallas_call(
        functools.partial(_route_kernel, tpb=tpb, seq=seq, ctx_row=ctx_row),
        grid=(t // tm,),
        in_specs=[
            pl.BlockSpec((tm, d), lambda i: (i, 0)),
            pl.BlockSpec((1, d), lambda i: (0, 0)),
            pl.BlockSpec(mod_l.shape, lambda i: (0, 0)),
            pl.BlockSpec((d, LANES), lambda i: (0, 0)),
        ],
        out_specs=[pl.BlockSpec((tm, d), lambda i: (i, 0)), pl.BlockSpec((tm, LANES), lambda i: (i, 0))],
        out_shape=[jax.ShapeDtypeStruct((t, d), F32), jax.ShapeDtypeStruct((t, LANES), F32)],
        compiler_params=_params("arbitrary"),
        name="moe_route",
    )(x_all, g.reshape(1, d), mod_l, r_pad)


def _row_copy(src_hbm, dst_vmem, sem, src_row, dst_row):
    return pltpu.make_async_copy(src_hbm.at[pl.ds(src_row, 1), :], dst_vmem.at[pl.ds(dst_row, 1), :], sem)


def _gather_kernel(idx_ref, src_hbm, o_ref, buf, sem):
    rows = buf.shape[0]

    def issue(r, carry):
        _row_copy(src_hbm, buf, sem, idx_ref[0, 0, r], r).start()
        return carry

    lax.fori_loop(0, rows, issue, 0)
    pltpu.make_async_copy(src_hbm.at[pl.ds(0, rows), :], buf, sem).wait()
    o_ref[...] = buf[...].astype(o_ref.dtype)


def _gather_rows(src, idx, *, rows_per_step):
    n = idx.shape[0]
    d = src.shape[1]
    steps = n // rows_per_step
    return pl.pallas_call(
        _gather_kernel,
        grid=(steps,),
        in_specs=[
            pl.BlockSpec((1, 1, rows_per_step), lambda i: (i, 0, 0), memory_space=pltpu.SMEM),
            pl.BlockSpec(memory_space=pl.ANY),
        ],
        out_specs=pl.BlockSpec((rows_per_step, d), lambda i: (i, 0)),
        out_shape=jax.ShapeDtypeStruct((n, d), BF16),
        scratch_shapes=[pltpu.VMEM((rows_per_step, d), src.dtype), pltpu.SemaphoreType.DMA(())],
        compiler_params=_params("arbitrary"),
        name="moe_gather",
    )(idx.reshape(steps, 1, rows_per_step), src)


def _gmm_kernel(te_ref, tv_ref, xs_ref, w1_ref, w3_ref, w2_ref, o_ref, acc_s, *, nf):
    j = pl.program_id(0)
    f = pl.program_id(1)
    valid = tv_ref[j] > 0

    @pl.when(valid)
    def _():
        @pl.when(f == 0)
        def _():
            acc_s[...] = jnp.zeros_like(acc_s)

        x = xs_ref[...]
        h1 = jnp.dot(x, w1_ref[...], preferred_element_type=F32)
        h3 = jnp.dot(x, w3_ref[...], preferred_element_type=F32)
        a = (jax.nn.silu(h1) * h3).astype(BF16)
        acc_s[...] += jnp.dot(a, w2_ref[...], preferred_element_type=F32)

        @pl.when(f == nf - 1)
        def _():
            o_ref[...] = acc_s[...]

    @pl.when(jnp.logical_and(jnp.logical_not(valid), f == nf - 1))
    def _():
        o_ref[...] = jnp.zeros_like(o_ref)


def _grouped_swiglu(xs, tile_expert, tile_valid, w1, w3, w2, *, tme):
    n, d = xs.shape
    dff = w1.shape[2]
    tf = _chunk(dff, 512)
    nf = dff // tf
    grid_spec = pltpu.PrefetchScalarGridSpec(
        num_scalar_prefetch=2,
        grid=(n // tme, nf),
        in_specs=[
            pl.BlockSpec((tme, d), lambda j, f, te, tv: (j, 0)),
            pl.BlockSpec((None, d, tf), lambda j, f, te, tv: (te[j], 0, f)),
            pl.BlockSpec((None, d, tf), lambda j, f, te, tv: (te[j], 0, f)),
            pl.BlockSpec((None, tf, d), lambda j, f, te, tv: (te[j], f, 0)),
        ],
        out_specs=pl.BlockSpec((tme, d), lambda j, f, te, tv: (j, 0)),
        scratch_shapes=[pltpu.VMEM((tme, d), F32)],
    )
    return pl.pallas_call(
        functools.partial(_gmm_kernel, nf=nf),
        grid_spec=grid_spec,
        out_shape=jax.ShapeDtypeStruct((n, d), F32),
        compiler_params=_params("arbitrary", "arbitrary"),
        name="moe_experts",
    )(tile_expert, tile_valid, xs, w1, w3, w2)


def _combine_kernel(pos_ref, ys_hbm, x_ref, route_ref, mod_ref, o_ref, buf0, buf1, sems, *, tpb, seq, ctx_row):
    i = pl.program_id(0)
    tm, d = x_ref.shape

    def issue(r, carry):
        _row_copy(ys_hbm, buf0, sems.at[0], pos_ref[0, 0, 2 * r], r).start()
        _row_copy(ys_hbm, buf1, sems.at[1], pos_ref[0, 0, 2 * r + 1], r).start()
        return carry

    lax.fori_loop(0, tm, issue, 0)
    pltpu.make_async_copy(ys_hbm.at[pl.ds(0, tm), :], buf0, sems.at[0]).wait()
    pltpu.make_async_copy(ys_hbm.at[pl.ds(0, tm), :], buf1, sems.at[1]).wait()
    g1 = route_ref[:, 2:3]
    g2 = route_ref[:, 3:4]
    gate = _mod_rows(mod_ref, 5, i, tm, tpb, seq, ctx_row, d)
    o_ref[...] = x_ref[...] + gate * (g1 * buf0[...] + g2 * buf1[...])


def _combine(ys, pos, x_all, route, mod_l, *, tm, tpb, seq, ctx_row):
    t, d = x_all.shape
    steps = t // tm
    return pl.pallas_call(
        functools.partial(_combine_kernel, tpb=tpb, seq=seq, ctx_row=ctx_row),
        grid=(steps,),
        in_specs=[
            pl.BlockSpec((1, 1, 2 * tm), lambda i: (i, 0, 0), memory_space=pltpu.SMEM),
            pl.BlockSpec(memory_space=pl.ANY),
            pl.BlockSpec((tm, d), lambda i: (i, 0)),
            pl.BlockSpec((tm, LANES), lambda i: (i, 0)),
            pl.BlockSpec(mod_l.shape, lambda i: (0, 0)),
        ],
        out_specs=pl.BlockSpec((tm, d), lambda i: (i, 0)),
        out_shape=jax.ShapeDtypeStruct((t, d), F32),
        scratch_shapes=[pltpu.VMEM((tm, d), F32), pltpu.VMEM((tm, d), F32), pltpu.SemaphoreType.DMA((2,))],
        compiler_params=_params("arbitrary"),
        name="moe_combine",
    )(pos.reshape(steps, 1, 2 * tm), ys, x_all, route, mod_l)


def _moe_ffn(x_all, g, mod_l, router, w1, w3, w2, *, tm, tpb, seq, ctx_row, tme, gather_rows):
    t = x_all.shape[0]
    tokens, route = _route(x_all, g, mod_l, router, tm=tm, tpb=tpb, seq=seq, ctx_row=ctx_row)
    e_flat = route[:, 0:2].astype(jnp.int32).reshape(-1)
    onehot = (e_flat[:, None] == jnp.arange(N_EXPERTS, dtype=jnp.int32)[None, :]).astype(jnp.int32)
    csum = jnp.cumsum(onehot, axis=0)
    rank = jnp.take_along_axis(csum, e_flat[:, None], axis=1)[:, 0] - 1
    counts = csum[-1]
    padded = ((counts + tme - 1) // tme) * tme
    seg_end = jnp.cumsum(padded)
    seg_start = seg_end - padded
    pos = seg_start[e_flat] + rank
    n_tiles = -(-(2 * t + N_EXPERTS * (tme - 1)) // tme)
    n_rows = n_tiles * tme
    src = jnp.zeros((n_rows,), jnp.int32).at[pos].set(jnp.arange(2 * t, dtype=jnp.int32) // 2)
    tile_start = jnp.arange(n_tiles, dtype=jnp.int32) * tme
    tile_expert = jnp.minimum(jnp.sum((tile_start[:, None] >= seg_end[None, :]).astype(jnp.int32), axis=1),
                              N_EXPERTS - 1)
    tile_valid = (tile_start < seg_end[-1]).astype(jnp.int32)

    xs = _gather_rows(tokens, src, rows_per_step=gather_rows)
    ys = _grouped_swiglu(xs, tile_expert, tile_valid, w1, w3, w2, tme=tme)
    return _combine(ys, pos, x_all, route, mod_l, tm=tm, tpb=tpb, seq=seq, ctx_row=ctx_row)


def _final_kernel(x_ref, g_ref, o_ref):
    x = x_ref[...]
    o_ref[...] = x * lax.rsqrt(jnp.mean(x * x, axis=-1, keepdims=True) + NORM_EPS) * g_ref[...]


def _final_norm(x3, g, *, seq):
    batch, _, d = x3.shape
    tr = _chunk(seq, 1024)
    return pl.pallas_call(
        _final_kernel,
        grid=(batch, seq // tr),
        in_specs=[pl.BlockSpec((None, tr, d), lambda b, j: (b, j, 0)), pl.BlockSpec((1, d), lambda b, j: (0, 0))],
        out_specs=pl.BlockSpec((None, tr, d), lambda b, j: (b, j, 0)),
        out_shape=jax.ShapeDtypeStruct((batch, seq, d), F32),
        compiler_params=_params("arbitrary", "arbitrary"),
        name="final_norm",
    )(x3, g.reshape(1, d))


def kernel(x, c, ctx, c_ctx, w_mod, b_mod, g_mix, g_ffn, w_in, w_out, ssm_a_re, ssm_a_im, ssm_log_step, ssm_b_re, ssm_b_im, ssm_c_re, ssm_c_im, ssm_d, glu_w, glu_b, na_rpb, ffn_w1, ffn_w3, ffn_w2, moe_router, moe_w1, moe_w3, moe_w2, g_final):
    batch, seq, d = x.shape
    n_ctx = ctx.shape[1]
    depth = w_mod.shape[0]
    ssm_w = ssm_d.shape[1]
    att_w = (w_in.shape[2] - ssm_w) // 3
    rpb_rows = seq + n_ctx
    assert batch == SUBLANES and batch < MOD_ROWS
    assert seq % GRID_W == 0 and att_w % LANES == 0 and ssm_w % SSM_BLOCK_IN == 0
    tm = _token_tile(rpb_rows)
    tpb = rpb_rows // tm
    lc = math.gcd(math.gcd(seq, n_ctx), 64)
    tme = 1024
    common = dict(tm=tm, tpb=tpb, seq=seq, ctx_row=batch)

    cvec = jnp.zeros((MOD_ROWS, d), F32).at[:batch].set(c.astype(F32)).at[batch].set(c_ctx.astype(F32))
    mod = _mod_table(cvec, w_mod.astype(F32), b_mod.astype(F32))
    x_all = jnp.concatenate([x, ctx], axis=1).astype(F32).reshape(batch * rpb_rows, d)

    for l in range(depth):
        mod_l = mod[l]
        u, q, k, v = _in_proj(x_all, g_mix[l].astype(F32), mod_l, w_in[l].astype(BF16), ssm_w=ssm_w, att_w=att_w,
                              **common)

        u_tm = u.reshape(batch, rpb_rows, ssm_w).transpose(1, 0, 2)
        y_tm = None
        for direction in range(2):
            lam_re, lam_im, bc, cc = _ssm_params(ssm_a_re[l, direction], ssm_a_im[l, direction],
                                                 ssm_log_step[l, direction], ssm_b_re[l, direction],
                                                 ssm_b_im[l, direction], ssm_c_re[l, direction],
                                                 ssm_c_im[l, direction])
            y_tm = _ssm_direction(u_tm, y_tm, lam_re, lam_im, bc, cc, ssm_d[l].astype(F32).reshape(1, ssm_w),
                                  lc=lc, seq=seq, n_ctx=n_ctx, reverse=direction == 1)
        y_ssm = y_tm.transpose(1, 0, 2).reshape(batch * rpb_rows, ssm_w)

        bias = _attention_bias(na_rpb[l], seq // GRID_W)
        y_att = _attention(q, k, v, bias, batch=batch, rpb_rows=rpb_rows, seq=seq, n_ctx=n_ctx)

        x_all = _mix_out(y_ssm, y_att, x_all, mod_l, glu_w[l].astype(BF16), glu_b[l].astype(F32),
                         w_out[l].astype(BF16), **common)

        if l % 2 == 0:
            x_all = _dense_ffn(x_all, g_ffn[l].astype(F32), mod_l, ffn_w1[l // 2].astype(BF16),
                               ffn_w3[l // 2].astype(BF16), ffn_w2[l // 2].astype(BF16), **common)
        else:
            x_all = _moe_ffn(x_all, g_ffn[l].astype(F32), mod_l, moe_router[l // 2],
                             moe_w1[l // 2].astype(BF16), moe_w3[l // 2].astype(BF16),
                             moe_w2[l // 2].astype(BF16), tme=tme, gather_rows=512, **common)

    return _final_norm(x_all.reshape(batch, rpb_rows, d), g_final.astype(F32), seq=seq).astype(x.dtype)
```

```python
import functools
import math

import jax
import jax.numpy as jnp
from jax import lax
from jax.experimental import pallas as pl
from jax.experimental.pallas import tpu as pltpu

F32 = jnp.float32
BF16 = jnp.bfloat16

GRID_W = 64
SSM_GROUP = 16
SSM_STATE = 64
HEAD_DIM = 64
NA_ROWS_MAX = 8
NA_COLS = 16
N_EXPERTS = 8
NORM_EPS = 1e-6
A_RE_MAX = -1e-4
MASK_VALUE = -1e30

LANES = 128
SUBLANES = 8
V7X_VMEM_LIMIT_BYTES = 56 * 1024 * 1024

MOD_ROWS = 16
SSM_BLOCK_IN = 128
SSM_BLOCK_STATE = 512


def _params(*semantics):
    return pltpu.CompilerParams(dimension_semantics=semantics, vmem_limit_bytes=V7X_VMEM_LIMIT_BYTES)


def _token_tile(rows_per_batch):
    for parts in range(1, rows_per_batch + 1):
        if rows_per_batch % parts == 0:
            tm = rows_per_batch // parts
            if tm <= 1152 and tm % 16 == 0:
                return tm
    raise ValueError("no token tile for %d rows" % rows_per_batch)


def _chunk(total, target):
    best = None
    for c in range(LANES, min(total, target) + 1, LANES):
        if total % c == 0:
            best = c
    if best is None:
        raise ValueError("no lane-aligned chunk for %d" % total)
    return best


def _mod_kernel(c_ref, w_ref, b_ref, o_ref):
    a = jax.nn.silu(c_ref[...])
    o_ref[...] = jnp.dot(a, w_ref[...], preferred_element_type=F32) + b_ref[...]


def _mod_table(cvec, w_mod, b_mod):
    depth, d, n = w_mod.shape
    tn = _chunk(n, 1536)
    return pl.pallas_call(
        _mod_kernel,
        grid=(depth, n // tn),
        in_specs=[
            pl.BlockSpec((MOD_ROWS, d), lambda l, j: (0, 0)),
            pl.BlockSpec((None, d, tn), lambda l, j: (l, 0, j)),
            pl.BlockSpec((None, 1, tn), lambda l, j: (l, 0, j)),
        ],
        out_specs=pl.BlockSpec((None, MOD_ROWS, tn), lambda l, j: (l, 0, j)),
        out_shape=jax.ShapeDtypeStruct((depth, MOD_ROWS, n), F32),
        compiler_params=_params("arbitrary", "arbitrary"),
        name="mod_table",
    )(cvec, w_mod, b_mod.reshape(depth, 1, n))


def _mod_rows(mod_ref, slab, tile_idx, tm, tpb, seq, ctx_row, d):
    b = tile_idx // tpb
    v_b = mod_ref[pl.ds(b, 1), slab * d:(slab + 1) * d]
    v_c = mod_ref[ctx_row:ctx_row + 1, slab * d:(slab + 1) * d]
    row = (tile_idx % tpb) * tm + lax.broadcasted_iota(jnp.int32, (tm, 1), 0)
    return jnp.where(row >= seq, v_c, v_b)


def _norm_mod(x, g, mod_ref, slab, tile_idx, tpb, seq, ctx_row):
    tm, d = x.shape
    shift = _mod_rows(mod_ref, slab, tile_idx, tm, tpb, seq, ctx_row, d)
    scale = _mod_rows(mod_ref, slab + 1, tile_idx, tm, tpb, seq, ctx_row, d)
    rs = lax.rsqrt(jnp.mean(x * x, axis=-1, keepdims=True) + NORM_EPS)
    return (x * rs) * g * (1.0 + scale) + shift


def _in_kernel(x_ref, g_ref, mod_ref, w_ref, u_ref, q_ref, k_ref, v_ref, *, tpb, seq, ctx_row, ssm_w, att_w):
    i = pl.program_id(0)
    h = _norm_mod(x_ref[...], g_ref[...], mod_ref, 0, i, tpb, seq, ctx_row).astype(BF16)
    c1 = ssm_w + att_w
    c2 = c1 + att_w
    u_ref[...] = jnp.dot(h, w_ref[:, 0:ssm_w], preferred_element_type=F32)
    q_ref[...] = (jnp.dot(h, w_ref[:, ssm_w:c1], preferred_element_type=F32) * (HEAD_DIM ** -0.5)).astype(BF16)
    k_ref[...] = jnp.dot(h, w_ref[:, c1:c2], preferred_element_type=F32).astype(BF16)
    v_ref[...] = jnp.dot(h, w_ref[:, c2:c2 + att_w], preferred_element_type=F32).astype(BF16)


def _in_proj(x_all, g, mod_l, w_in, *, tm, tpb, seq, ctx_row, ssm_w, att_w):
    t, d = x_all.shape
    n = w_in.shape[1]
    row_block = lambda width: pl.BlockSpec((tm, width), lambda i: (i, 0))
    return pl.pallas_call(
        functools.partial(_in_kernel, tpb=tpb, seq=seq, ctx_row=ctx_row, ssm_w=ssm_w, att_w=att_w),
        grid=(t // tm,),
        in_specs=[
            row_block(d),
            pl.BlockSpec((1, d), lambda i: (0, 0)),
            pl.BlockSpec(mod_l.shape, lambda i: (0, 0)),
            pl.BlockSpec((d, n), lambda i: (0, 0)),
        ],
        out_specs=[row_block(ssm_w), row_block(att_w), row_block(att_w), row_block(att_w)],
        out_shape=[
            jax.ShapeDtypeStruct((t, ssm_w), F32),
            jax.ShapeDtypeStruct((t, att_w), BF16),
            jax.ShapeDtypeStruct((t, att_w), BF16),
            jax.ShapeDtypeStruct((t, att_w), BF16),
        ],
        compiler_params=_params("arbitrary"),
        name="in_proj",
    )(x_all, g.reshape(1, d), mod_l, w_in)


def _att_kernel(q_ref, k_ref, v_ref, bias_ref, o_ref, *, seq, n_ctx, rows, win, unroll):
    lane = lax.broadcasted_iota(jnp.int32, (1, LANES), 1)
    first_head = lane < HEAD_DIM
    mask0 = first_head.astype(BF16)
    mask1 = 1 - mask0
    trans_b = (((1,), (1,)), ((), ()))

    def stack(q):
        return jnp.concatenate([q * mask0, q * mask1], axis=0)

    def attend(q, parts):
        m_rows = q.shape[0]
        q2 = stack(q)
        scores = []
        for keys, _, bias in parts:
            s = lax.dot_general(q2, keys, trans_b, preferred_element_type=F32)
            scores.append(s if bias is None else s + bias)
        lane_tiles = lambda arrs: [a[:, c:c + LANES] for a in arrs for c in range(0, a.shape[1], LANES)]
        m = jnp.max(functools.reduce(jnp.maximum, lane_tiles(scores)), axis=-1, keepdims=True)
        probs = [jnp.exp(s - m) for s in scores]
        den = jnp.sum(functools.reduce(lambda a, b: a + b, lane_tiles(probs)), axis=-1, keepdims=True)
        o = functools.reduce(
            lambda a, b: a + b,
            [jnp.dot(p.astype(BF16), vals, preferred_element_type=F32) for p, (_, vals, _) in zip(probs, parts)])
        o = o / den
        return jnp.where(first_head, o[:m_rows], o[m_rows:])

    def latent_row(r, carry):
        r0 = jnp.clip(r - win // 2, 0, rows - win)
        q0 = pl.multiple_of(r * GRID_W, GRID_W)
        k0 = pl.multiple_of(r0 * GRID_W, GRID_W)
        q = q_ref[pl.ds(q0, GRID_W), :]
        kw = k_ref[pl.ds(k0, win * GRID_W), :]
        vw = v_ref[pl.ds(k0, win * GRID_W), :]
        kc = k_ref[seq:seq + n_ctx, :]
        vc = v_ref[seq:seq + n_ctx, :]
        o = attend(q, [(kw, vw, bias_ref[r - r0]), (kc, vc, None)])
        o_ref[pl.ds(q0, GRID_W), :] = o.astype(o_ref.dtype)
        return carry

    lax.fori_loop(0, rows, latent_row, 0, unroll=unroll)

    kc = k_ref[seq:seq + n_ctx, :]
    vc = v_ref[seq:seq + n_ctx, :]
    o_ref[seq:seq + n_ctx, :] = attend(q_ref[seq:seq + n_ctx, :], [(kc, vc, None)]).astype(o_ref.dtype)


def _attention(q, k, v, bias, *, batch, rpb_rows, seq, n_ctx):
    t, att_w = q.shape
    rows = seq // GRID_W
    win = min(NA_ROWS_MAX, rows)
    n_pairs = att_w // LANES
    blk = pl.BlockSpec((rpb_rows, LANES), lambda b, p: (b, p))
    return pl.pallas_call(
        functools.partial(_att_kernel, seq=seq, n_ctx=n_ctx, rows=rows, win=win, unroll=4),
        grid=(batch, n_pairs),
        in_specs=[blk, blk, blk,
                  pl.BlockSpec((None, win, 2 * GRID_W, win * GRID_W), lambda b, p: (p, 0, 0, 0))],
        out_specs=blk,
        out_shape=jax.ShapeDtypeStruct((t, att_w), BF16),
        compiler_params=_params("arbitrary", "arbitrary"),
        name="attention",
    )(q, k, v, bias)


def _attention_bias(rpb, rows):
    n_heads = rpb.shape[0]
    win = min(NA_ROWS_MAX, rows)
    col = jnp.arange(GRID_W)
    col_start = jnp.clip(col - NA_COLS // 2, 0, GRID_W - NA_COLS)
    kcol = jnp.arange(GRID_W)
    valid = (kcol[None, :] >= col_start[:, None]) & (kcol[None, :] < col_start[:, None] + NA_COLS)
    col_rel = kcol[None, :] - col[:, None] + NA_COLS - 1
    var = jnp.arange(win)
    rr = jnp.arange(win)
    row_rel = rr[None, :] - var[:, None] + NA_ROWS_MAX - 1
    by_row = rpb.astype(F32)[:, row_rel]
    pick = (col_rel[:, :, None] == jnp.arange(2 * NA_COLS - 1)[None, None, :]).astype(F32)
    tab = jnp.einsum("hvrj,ckj->hvcrk", by_row, pick, precision=lax.Precision.HIGHEST)
    tab = jnp.where(valid[None, None, :, None, :], tab, MASK_VALUE)
    tab = tab.reshape(n_heads // 2, 2, win, GRID_W, win * GRID_W).transpose(0, 2, 1, 3, 4)
    return tab.reshape(n_heads // 2, win, 2 * GRID_W, win * GRID_W)


def _ssm_kernel(*refs, lc, reverse, first):
    if first:
        u_ref, lre_ref, lim_ref, bc_ref, cc_ref, dsk_ref, y_ref, hre_s, him_s, xre_s, xim_s = refs
        yprev_ref = None
    else:
        u_ref, yprev_ref, lre_ref, lim_ref, bc_ref, cc_ref, y_ref, hre_s, him_s, xre_s, xim_s = refs
    nb = u_ref.shape[1]
    width = u_ref.shape[2]
    n_state = xre_s.shape[2]
    n_blocks = width // SSM_BLOCK_IN

    @pl.when(pl.program_id(0) == 0)
    def _():
        hre_s[...] = jnp.zeros_like(hre_s)
        him_s[...] = jnp.zeros_like(him_s)

    u = u_ref[...].reshape(lc * nb, width)
    ub = u.astype(BF16)
    for jb in range(n_blocks):
        xb = jnp.dot(ub[:, jb * SSM_BLOCK_IN:(jb + 1) * SSM_BLOCK_IN], bc_ref[jb], preferred_element_type=F32)
        s0 = jb * SSM_BLOCK_STATE
        xre_s[:, :, s0:s0 + SSM_BLOCK_STATE] = xb[:, :SSM_BLOCK_STATE].reshape(lc, nb, SSM_BLOCK_STATE)
        xim_s[:, :, s0:s0 + SSM_BLOCK_STATE] = xb[:, SSM_BLOCK_STATE:].reshape(lc, nb, SSM_BLOCK_STATE)

    cw = 512
    for cb in range(n_state // cw):
        c0 = cb * cw
        lr = jnp.broadcast_to(lre_ref[:, c0:c0 + cw], (nb, cw))
        li = jnp.broadcast_to(lim_ref[:, c0:c0 + cw], (nb, cw))

        def step(t, carry, c0=c0, lr=lr, li=li):
            hr, hi = carry
            tt = (lc - 1 - t) if reverse else t
            nr = lr * hr - li * hi + xre_s[tt, :, c0:c0 + cw]
            ni = lr * hi + li * hr + xim_s[tt, :, c0:c0 + cw]
            xre_s[tt, :, c0:c0 + cw] = nr
            xim_s[tt, :, c0:c0 + cw] = ni
            return nr, ni

        hr, hi = lax.fori_loop(0, lc, step, (hre_s[:, c0:c0 + cw], him_s[:, c0:c0 + cw]), unroll=2)
        hre_s[:, c0:c0 + cw] = hr
        him_s[:, c0:c0 + cw] = hi

    if first:
        base = u * dsk_ref[...]
    else:
        base = yprev_ref[...].reshape(lc * nb, width)
    for jb in range(n_blocks):
        s0 = jb * SSM_BLOCK_STATE
        h_re = xre_s[:, :, s0:s0 + SSM_BLOCK_STATE].reshape(lc * nb, SSM_BLOCK_STATE).astype(BF16)
        h_im = xim_s[:, :, s0:s0 + SSM_BLOCK_STATE].reshape(lc * nb, SSM_BLOCK_STATE).astype(BF16)
        yj = (jnp.dot(h_re, cc_ref[jb, 0:SSM_BLOCK_STATE, :], preferred_element_type=F32)
              + jnp.dot(h_im, cc_ref[jb, SSM_BLOCK_STATE:, :], preferred_element_type=F32))
        o0 = jb * SSM_BLOCK_IN
        y_ref[:, :, o0:o0 + SSM_BLOCK_IN] = (base[:, o0:o0 + SSM_BLOCK_IN] + yj).reshape(lc, nb, SSM_BLOCK_IN)


def _ssm_direction(u_tm, y_prev, lam_re, lam_im, bc, cc, d_skip, *, lc, seq, n_ctx, reverse):
    total, nb, width = u_tm.shape
    n_state = lam_re.shape[1]
    n_ch = total // lc
    n_lat_ch = seq // lc
    n_ctx_ch = n_ctx // lc
    first = y_prev is None
    if reverse:
        chunk = lambda i: (n_ch - 1 - i, 0, 0)
    else:
        chunk = lambda i: (jnp.where(i < n_ctx_ch, n_lat_ch + i, i - n_ctx_ch), 0, 0)
    seq_block = pl.BlockSpec((lc, nb, width), chunk)
    whole = lambda a: pl.BlockSpec(a.shape, lambda i: (0,) * a.ndim)
    args = [u_tm] + ([] if first else [y_prev]) + [lam_re, lam_im, bc, cc] + ([d_skip] if first else [])
    in_specs = [seq_block] + ([] if first else [seq_block]) + [whole(lam_re), whole(lam_im), whole(bc), whole(cc)]
    if first:
        in_specs.append(whole(d_skip))
    return pl.pallas_call(
        functools.partial(_ssm_kernel, lc=lc, reverse=reverse, first=first),
        grid=(n_ch,),
        in_specs=in_specs,
        out_specs=seq_block,
        out_shape=jax.ShapeDtypeStruct((total, nb, width), F32),
        scratch_shapes=[
            pltpu.VMEM((nb, n_state), F32),
            pltpu.VMEM((nb, n_state), F32),
            pltpu.VMEM((lc, nb, n_state), F32),
            pltpu.VMEM((lc, nb, n_state), F32),
        ],
        compiler_params=_params("arbitrary"),
        name="ssm_bwd" if reverse else "ssm_fwd",
    )(*args)


def _ssm_params(a_re, a_im, log_step, b_re, b_im, c_re, c_im):
    g, p, h = b_re.shape
    a_re = jnp.minimum(a_re.astype(F32), A_RE_MAX)
    a_im = a_im.astype(F32)
    dt = jnp.exp(log_step.astype(F32))[:, None]
    mag = jnp.exp(a_re * dt)
    lam_re = mag * jnp.cos(a_im * dt)
    lam_im = mag * jnp.sin(a_im * dt)
    den = a_re * a_re + a_im * a_im
    z_re = ((lam_re - 1) * a_re + lam_im * a_im) / den
    z_im = (lam_im * a_re - (lam_re - 1) * a_im) / den
    b_re = b_re.astype(F32)
    b_im = b_im.astype(F32)
    bb_re = z_re[..., None] * b_re - z_im[..., None] * b_im
    bb_im = z_re[..., None] * b_im + z_im[..., None] * b_re
    gpb = SSM_BLOCK_IN // h
    nblk = g // gpb
    eye = jnp.eye(gpb, dtype=F32)

    def pack_in(bb):
        bb = bb.reshape(nblk, gpb, p, h)
        return jnp.einsum("ngph,gk->nghkp", bb, eye).reshape(nblk, gpb * h, gpb * p)

    def pack_out(c):
        c = c.reshape(nblk, gpb, h, p)
        return jnp.einsum("nghp,gk->ngpkh", c, eye).reshape(nblk, gpb * p, gpb * h)

    bc = jnp.concatenate([pack_in(bb_re), pack_in(bb_im)], axis=2).astype(BF16)
    cc = jnp.concatenate([pack_out(c_re.astype(F32)), -pack_out(c_im.astype(F32))], axis=1).astype(BF16)
    return lam_re.reshape(1, g * p), lam_im.reshape(1, g * p), bc, cc


def _mix_kernel(y_ref, att_ref, x_ref, mod_ref, gw_ref, gb_ref, wo_ref, o_ref, *, tpb, seq, ctx_row):
    i = pl.program_id(0)
    tm, d = x_ref.shape
    ssm_w = y_ref.shape[1]
    g = jax.nn.gelu(y_ref[...])
    z = jnp.dot(g.astype(BF16), gw_ref[...], preferred_element_type=F32) + gb_ref[...]
    s = g * jax.nn.sigmoid(z)
    o = (jnp.dot(s.astype(BF16), wo_ref[0:ssm_w, :], preferred_element_type=F32)
         + jnp.dot(att_ref[...], wo_ref[ssm_w:, :], preferred_element_type=F32))
    gate = _mod_rows(mod_ref, 2, i, tm, tpb, seq, ctx_row, d)
    o_ref[...] = x_ref[...] + gate * o


def _mix_out(y_ssm, y_att, x_all, mod_l, glu_w, glu_b, w_out, *, tm, tpb, seq, ctx_row):
    t, d = x_all.shape
    ssm_w = y_ssm.shape[1]
    att_w = y_att.shape[1]
    row_block = lambda width: pl.BlockSpec((tm, width), lambda i: (i, 0))
    whole = lambda a: pl.BlockSpec(a.shape, lambda i: (0, 0))
    gb = glu_b.reshape(1, ssm_w)
    return pl.pallas_call(
        functools.partial(_mix_kernel, tpb=tpb, seq=seq, ctx_row=ctx_row),
        grid=(t // tm,),
        in_specs=[row_block(ssm_w), row_block(att_w), row_block(d), whole(mod_l), whole(glu_w), whole(gb),
                  whole(w_out)],
        out_specs=row_block(d),
        out_shape=jax.ShapeDtypeStruct((t, d), F32),
        compiler_params=_params("arbitrary"),
        name="mix_out",
    )(y_ssm, y_att, x_all, mod_l, glu_w, gb, w_out)


def _ffn_kernel(x_ref, g_ref, mod_ref, w1_ref, w3_ref, w2_ref, o_ref, t_s, acc_s, *, tpb, seq, ctx_row, nf):
    i = pl.program_id(0)
    j = pl.program_id(1)
    tm, d = x_ref.shape

    @pl.when(j == 0)
    def _():
        t_s[...] = _norm_mod(x_ref[...], g_ref[...], mod_ref, 3, i, tpb, seq, ctx_row).astype(BF16)
        acc_s[...] = jnp.zeros_like(acc_s)

    t = t_s[...]
    h1 = jnp.dot(t, w1_ref[...], preferred_element_type=F32)
    h3 = jnp.dot(t, w3_ref[...], preferred_element_type=F32)
    a = (jax.nn.silu(h1) * h3).astype(BF16)
    acc_s[...] += jnp.dot(a, w2_ref[...], preferred_element_type=F32)

    @pl.when(j == nf - 1)
    def _():
        gate = _mod_rows(mod_ref, 5, i, tm, tpb, seq, ctx_row, d)
        o_ref[...] = x_ref[...] + gate * acc_s[...]


def _dense_ffn(x_all, g, mod_l, w1, w3, w2, *, tm, tpb, seq, ctx_row):
    t, d = x_all.shape
    dff = w1.shape[1]
    tf = _chunk(dff, 256)
    nf = dff // tf
    return pl.pallas_call(
        functools.partial(_ffn_kernel, tpb=tpb, seq=seq, ctx_row=ctx_row, nf=nf),
        grid=(t // tm, nf),
        in_specs=[
            pl.BlockSpec((tm, d), lambda i, j: (i, 0)),
            pl.BlockSpec((1, d), lambda i, j: (0, 0)),
            pl.BlockSpec(mod_l.shape, lambda i, j: (0, 0)),
            pl.BlockSpec((d, tf), lambda i, j: (0, j)),
            pl.BlockSpec((d, tf), lambda i, j: (0, j)),
            pl.BlockSpec((tf, d), lambda i, j: (j, 0)),
        ],
        out_specs=pl.BlockSpec((tm, d), lambda i, j: (i, 0)),
        out_shape=jax.ShapeDtypeStruct((t, d), F32),
        scratch_shapes=[pltpu.VMEM((tm, d), BF16), pltpu.VMEM((tm, d), F32)],
        compiler_params=_params("arbitrary", "arbitrary"),
        name="dense_ffn",
    )(x_all, g.reshape(1, d), mod_l, w1, w3, w2)


def _route_kernel(x_ref, g_ref, mod_ref, r_ref, t_ref, route_ref, *, tpb, seq, ctx_row):
    i = pl.program_id(0)
    t = _norm_mod(x_ref[...], g_ref[...], mod_ref, 3, i, tpb, seq, ctx_row)
    t_ref[...] = t
    th = t.astype(BF16)
    tl = (t - th.astype(F32)).astype(BF16)
    r = r_ref[...]
    rh = r.astype(BF16)
    rl = (r - rh.astype(F32)).astype(BF16)
    dot = lambda a, b: jnp.dot(a, b, preferred_element_type=F32)
    logits = dot(th, rh) + (dot(th, rl) + dot(tl, rh)) + dot(tl, rl)
    tm = logits.shape[0]
    lane = lax.broadcasted_iota(jnp.int32, (tm, LANES), 1)
    lane_f = lane.astype(F32)
    neg_inf = jnp.float32(-jnp.inf)
    lg = jnp.where(lane < N_EXPERTS, logits, neg_inf)
    m1 = jnp.max(lg, axis=-1, keepdims=True)
    i1 = jnp.min(jnp.where(lg == m1, lane_f, float(LANES)), axis=-1, keepdims=True)
    lg2 = jnp.where(lane_f == i1, neg_inf, lg)
    m2 = jnp.max(lg2, axis=-1, keepdims=True)
    i2 = jnp.min(jnp.where(lg2 == m2, lane_f, float(LANES)), axis=-1, keepdims=True)
    e = jnp.exp(m2 - m1)
    g1 = 1.0 / (1.0 + e)
    g2 = e / (1.0 + e)
    route_ref[...] = jnp.where(lane == 0, i1, jnp.where(lane == 1, i2, jnp.where(lane == 2, g1, jnp.where(
        lane == 3, g2, 0.0))))


def _route(x_all, g, mod_l, router, *, tm, tpb, seq, ctx_row):
    t, d = x_all.shape
    r_pad = jnp.zeros((d, LANES), F32).at[:, :router.shape[1]].set(router.astype(F32))
    return pl.pallas_call(
        functools.partial(_route_kernel, tpb=tpb, seq=seq, ctx_row=ctx_row),
        grid=(t // tm,),
        in_specs=[
            pl.BlockSpec((tm, d), lambda i: (i, 0)),
            pl.BlockSpec((1, d), lambda i: (0, 0)),
            pl.BlockSpec(mod_l.shape, lambda i: (0, 0)),
            pl.BlockSpec((d, LANES), lambda i: (0, 0)),
        ],
        out_specs=[pl.BlockSpec((tm, d), lambda i: (i, 0)), pl.BlockSpec((tm, LANES), lambda i: (i, 0))],
        out_shape=[jax.ShapeDtypeStruct((t, d), F32), jax.ShapeDtypeStruct((t, LANES), F32)],
        compiler_params=_params("arbitrary"),
        name="moe_route",
    )(x_all, g.reshape(1, d), mod_l, r_pad)


def _row_copy(src_hbm, dst_vmem, sem, src_row, dst_row):
    return pltpu.make_async_copy(src_hbm.at[pl.ds(src_row, 1), :], dst_vmem.at[pl.ds(dst_row, 1), :], sem)


ISSUE_UNROLL = 8


def _gather_kernel(idx_ref, idx_next_ref, src_hbm, o_ref, buf, sems):
    i = pl.program_id(0)
    rows = buf.shape[1]

    def issue_all(idx, slot):
        def issue(g, carry):
            for k in range(ISSUE_UNROLL):
                r = g * ISSUE_UNROLL + k
                _row_copy(src_hbm, buf.at[slot], sems.at[slot], idx[0, 0, r], r).start()
            return carry

        lax.fori_loop(0, rows // ISSUE_UNROLL, issue, 0)

    @pl.when(i == 0)
    def _():
        issue_all(idx_ref, 0)

    @pl.when(i + 1 < pl.num_programs(0))
    def _():
        issue_all(idx_next_ref, (i + 1) % 2)

    slot = i % 2
    pltpu.make_async_copy(src_hbm.at[pl.ds(0, rows), :], buf.at[slot], sems.at[slot]).wait()
    o_ref[...] = buf[slot].astype(o_ref.dtype)


def _gather_rows(src, idx, *, rows_per_step):
    n = idx.shape[0]
    d = src.shape[1]
    steps = n // rows_per_step
    assert rows_per_step % ISSUE_UNROLL == 0
    idx3 = idx.reshape(steps, 1, rows_per_step)
    return pl.pallas_call(
        _gather_kernel,
        grid=(steps,),
        in_specs=[
            pl.BlockSpec((1, 1, rows_per_step), lambda i: (i, 0, 0), memory_space=pltpu.SMEM),
            pl.BlockSpec((1, 1, rows_per_step), lambda i: (jnp.minimum(i + 1, steps - 1), 0, 0),
                         memory_space=pltpu.SMEM),
            pl.BlockSpec(memory_space=pl.ANY),
        ],
        out_specs=pl.BlockSpec((rows_per_step, d), lambda i: (i, 0)),
        out_shape=jax.ShapeDtypeStruct((n, d), BF16),
        scratch_shapes=[pltpu.VMEM((2, rows_per_step, d), src.dtype), pltpu.SemaphoreType.DMA((2,))],
        compiler_params=_params("arbitrary"),
        name="moe_gather",
    )(idx3, idx3, src)


def _gmm_kernel(te_ref, tv_ref, xs_ref, w1_ref, w3_ref, w2_ref, o_ref, acc_s, *, nf):
    j = pl.program_id(0)
    f = pl.program_id(1)
    valid = tv_ref[j] > 0

    @pl.when(valid)
    def _():
        @pl.when(f == 0)
        def _():
            acc_s[...] = jnp.zeros_like(acc_s)

        x = xs_ref[...]
        h1 = jnp.dot(x, w1_ref[...], preferred_element_type=F32)
        h3 = jnp.dot(x, w3_ref[...], preferred_element_type=F32)
        a = (jax.nn.silu(h1) * h3).astype(BF16)
        acc_s[...] += jnp.dot(a, w2_ref[...], preferred_element_type=F32)

        @pl.when(f == nf - 1)
        def _():
            o_ref[...] = acc_s[...]

    @pl.when(jnp.logical_and(jnp.logical_not(valid), f == nf - 1))
    def _():
        o_ref[...] = jnp.zeros_like(o_ref)


def _grouped_swiglu(xs, tile_expert, tile_valid, w1, w3, w2, *, tme):
    n, d = xs.shape
    dff = w1.shape[2]
    tf = _chunk(dff, 512)
    nf = dff // tf
    grid_spec = pltpu.PrefetchScalarGridSpec(
        num_scalar_prefetch=2,
        grid=(n // tme, nf),
        in_specs=[
            pl.BlockSpec((tme, d), lambda j, f, te, tv: (j, 0)),
            pl.BlockSpec((None, d, tf), lambda j, f, te, tv: (te[j], 0, f)),
            pl.BlockSpec((None, d, tf), lambda j, f, te, tv: (te[j], 0, f)),
            pl.BlockSpec((None, tf, d), lambda j, f, te, tv: (te[j], f, 0)),
        ],
        out_specs=pl.BlockSpec((tme, d), lambda j, f, te, tv: (j, 0)),
        scratch_shapes=[pltpu.VMEM((tme, d), F32)],
    )
    return pl.pallas_call(
        functools.partial(_gmm_kernel, nf=nf),
        grid_spec=grid_spec,
        out_shape=jax.ShapeDtypeStruct((n, d), F32),
        compiler_params=_params("arbitrary", "arbitrary"),
        name="moe_experts",
    )(tile_expert, tile_valid, xs, w1, w3, w2)


def _combine_kernel(pos_ref, pos_next_ref, ys_hbm, x_ref, route_ref, mod_ref, o_ref, buf, sems, *, tpb, seq,
                    ctx_row):
    i = pl.program_id(0)
    tm, d = x_ref.shape

    def issue_all(pos, slot):
        def issue(g, carry):
            for k in range(ISSUE_UNROLL // 2):
                r = g * (ISSUE_UNROLL // 2) + k
                _row_copy(ys_hbm, buf.at[slot, 0], sems.at[slot, 0], pos[0, 0, 2 * r], r).start()
                _row_copy(ys_hbm, buf.at[slot, 1], sems.at[slot, 1], pos[0, 0, 2 * r + 1], r).start()
            return carry

        lax.fori_loop(0, tm // (ISSUE_UNROLL // 2), issue, 0)

    @pl.when(i == 0)
    def _():
        issue_all(pos_ref, 0)

    @pl.when(i + 1 < pl.num_programs(0))
    def _():
        issue_all(pos_next_ref, (i + 1) % 2)

    slot = i % 2
    pltpu.make_async_copy(ys_hbm.at[pl.ds(0, tm), :], buf.at[slot, 0], sems.at[slot, 0]).wait()
    pltpu.make_async_copy(ys_hbm.at[pl.ds(0, tm), :], buf.at[slot, 1], sems.at[slot, 1]).wait()
    g1 = route_ref[:, 2:3]
    g2 = route_ref[:, 3:4]
    gate = _mod_rows(mod_ref, 5, i, tm, tpb, seq, ctx_row, d)
    o_ref[...] = x_ref[...] + gate * (g1 * buf[slot, 0] + g2 * buf[slot, 1])


def _combine(ys, pos, x_all, route, mod_l, *, tm, tpb, seq, ctx_row):
    t, d = x_all.shape
    steps = t // tm
    assert tm % (ISSUE_UNROLL // 2) == 0
    pos3 = pos.reshape(steps, 1, 2 * tm)
    return pl.pallas_call(
        functools.partial(_combine_kernel, tpb=tpb, seq=seq, ctx_row=ctx_row),
        grid=(steps,),
        in_specs=[
            pl.BlockSpec((1, 1, 2 * tm), lambda i: (i, 0, 0), memory_space=pltpu.SMEM),
            pl.BlockSpec((1, 1, 2 * tm), lambda i: (jnp.minimum(i + 1, steps - 1), 0, 0), memory_space=pltpu.SMEM),
            pl.BlockSpec(memory_space=pl.ANY),
            pl.BlockSpec((tm, d), lambda i: (i, 0)),
            pl.BlockSpec((tm, LANES), lambda i: (i, 0)),
            pl.BlockSpec(mod_l.shape, lambda i: (0, 0)),
        ],
        out_specs=pl.BlockSpec((tm, d), lambda i: (i, 0)),
        out_shape=jax.ShapeDtypeStruct((t, d), F32),
        scratch_shapes=[pltpu.VMEM((2, 2, tm, d), F32), pltpu.SemaphoreType.DMA((2, 2))],
        compiler_params=_params("arbitrary"),
        name="moe_combine",
    )(pos3, pos3, ys, x_all, route, mod_l)


def _moe_ffn(x_all, g, mod_l, router, w1, w3, w2, *, tm, tpb, seq, ctx_row, tme, gather_rows):
    t = x_all.shape[0]
    tokens, route = _route(x_all, g, mod_l, router, tm=tm, tpb=tpb, seq=seq, ctx_row=ctx_row)
    e_flat = route[:, 0:2].astype(jnp.int32).reshape(-1)
    onehot = (e_flat[:, None] == jnp.arange(N_EXPERTS, dtype=jnp.int32)[None, :]).astype(jnp.int32)
    csum = jnp.cumsum(onehot, axis=0)
    rank = jnp.take_along_axis(csum, e_flat[:, None], axis=1)[:, 0] - 1
    counts = csum[-1]
    padded = ((counts + tme - 1) // tme) * tme
    seg_end = jnp.cumsum(padded)
    seg_start = seg_end - padded
    pos = seg_start[e_flat] + rank
    n_tiles = -(-(2 * t + N_EXPERTS * (tme - 1)) // tme)
    n_rows = n_tiles * tme
    src = jnp.zeros((n_rows,), jnp.int32).at[pos].set(jnp.arange(2 * t, dtype=jnp.int32) // 2)
    tile_start = jnp.arange(n_tiles, dtype=jnp.int32) * tme
    tile_expert = jnp.minimum(jnp.sum((tile_start[:, None] >= seg_end[None, :]).astype(jnp.int32), axis=1),
                              N_EXPERTS - 1)
    tile_valid = (tile_start < seg_end[-1]).astype(jnp.int32)

    xs = _gather_rows(tokens, src, rows_per_step=gather_rows)
    ys = _grouped_swiglu(xs, tile_expert, tile_valid, w1, w3, w2, tme=tme)
    return _combine(ys, pos, x_all, route, mod_l, tm=tm, tpb=tpb, seq=seq, ctx_row=ctx_row)


def _final_kernel(x_ref, g_ref, o_ref):
    x = x_ref[...]
    o_ref[...] = x * lax.rsqrt(jnp.mean(x * x, axis=-1, keepdims=True) + NORM_EPS) * g_ref[...]


def _final_norm(x3, g, *, seq):
    batch, _, d = x3.shape
    tr = _chunk(seq, 1024)
    return pl.pallas_call(
        _final_kernel,
        grid=(batch, seq // tr),
        in_specs=[pl.BlockSpec((None, tr, d), lambda b, j: (b, j, 0)), pl.BlockSpec((1, d), lambda b, j: (0, 0))],
        out_specs=pl.BlockSpec((None, tr, d), lambda b, j: (b, j, 0)),
        out_shape=jax.ShapeDtypeStruct((batch, seq, d), F32),
        compiler_params=_params("arbitrary", "arbitrary"),
        name="final_norm",
    )(x3, g.reshape(1, d))


def kernel(x, c, ctx, c_ctx, w_mod, b_mod, g_mix, g_ffn, w_in, w_out, ssm_a_re, ssm_a_im, ssm_log_step, ssm_b_re, ssm_b_im, ssm_c_re, ssm_c_im, ssm_d, glu_w, glu_b, na_rpb, ffn_w1, ffn_w3, ffn_w2, moe_router, moe_w1, moe_w3, moe_w2, g_final):
    batch, seq, d = x.shape
    n_ctx = ctx.shape[1]
    depth = w_mod.shape[0]
    ssm_w = ssm_d.shape[1]
    att_w = (w_in.shape[2] - ssm_w) // 3
    rpb_rows = seq + n_ctx
    assert batch == SUBLANES and batch < MOD_ROWS
    assert seq % GRID_W == 0 and att_w % LANES == 0 and ssm_w % SSM_BLOCK_IN == 0
    tm = _token_tile(rpb_rows)
    tpb = rpb_rows // tm
    lc = math.gcd(math.gcd(seq, n_ctx), 64)
    tme = 1024
    common = dict(tm=tm, tpb=tpb, seq=seq, ctx_row=batch)

    cvec = jnp.zeros((MOD_ROWS, d), F32).at[:batch].set(c.astype(F32)).at[batch].set(c_ctx.astype(F32))
    mod = _mod_table(cvec, w_mod.astype(F32), b_mod.astype(F32))
    x_all = jnp.concatenate([x, ctx], axis=1).astype(F32).reshape(batch * rpb_rows, d)

    for l in range(depth):
        mod_l = mod[l]
        u, q, k, v = _in_proj(x_all, g_mix[l].astype(F32), mod_l, w_in[l].astype(BF16), ssm_w=ssm_w, att_w=att_w,
                              **common)

        u_tm = u.reshape(batch, rpb_rows, ssm_w).transpose(1, 0, 2)
        y_tm = None
        for direction in range(2):
            lam_re, lam_im, bc, cc = _ssm_params(ssm_a_re[l, direction], ssm_a_im[l, direction],
                                                 ssm_log_step[l, direction], ssm_b_re[l, direction],
                                                 ssm_b_im[l, direction], ssm_c_re[l, direction],
                                                 ssm_c_im[l, direction])
            y_tm = _ssm_direction(u_tm, y_tm, lam_re, lam_im, bc, cc, ssm_d[l].astype(F32).reshape(1, ssm_w),
                                  lc=lc, seq=seq, n_ctx=n_ctx, reverse=direction == 1)
        y_ssm = y_tm.transpose(1, 0, 2).reshape(batch * rpb_rows, ssm_w)

        bias = _attention_bias(na_rpb[l], seq // GRID_W)
        y_att = _attention(q, k, v, bias, batch=batch, rpb_rows=rpb_rows, seq=seq, n_ctx=n_ctx)

        x_all = _mix_out(y_ssm, y_att, x_all, mod_l, glu_w[l].astype(BF16), glu_b[l].astype(F32),
                         w_out[l].astype(BF16), **common)

        if l % 2 == 0:
            x_all = _dense_ffn(x_all, g_ffn[l].astype(F32), mod_l, ffn_w1[l // 2].astype(BF16),
                               ffn_w3[l // 2].astype(BF16), ffn_w2[l // 2].astype(BF16), **common)
        else:
            x_all = _moe_ffn(x_all, g_ffn[l].astype(F32), mod_l, moe_router[l // 2],
                             moe_w1[l // 2].astype(BF16), moe_w3[l // 2].astype(BF16),
                             moe_w2[l // 2].astype(BF16), tme=tme, gather_rows=512, **common)

    return _final_norm(x_all.reshape(batch, rpb_rows, d), g_final.astype(F32), seq=seq).astype(x.dtype)
```

```python
import functools
import math

import jax
import jax.numpy as jnp
from jax import lax
from jax.experimental import pallas as pl
from jax.experimental.pallas import tpu as pltpu

F32 = jnp.float32
BF16 = jnp.bfloat16

GRID_W = 64
SSM_GROUP = 16
SSM_STATE = 64
HEAD_DIM = 64
NA_ROWS_MAX = 8
NA_COLS = 16
N_EXPERTS = 8
NORM_EPS = 1e-6
A_RE_MAX = -1e-4
MASK_VALUE = -1e30

LANES = 128
SUBLANES = 8
V7X_VMEM_LIMIT_BYTES = 56 * 1024 * 1024

MOD_ROWS = 16
SSM_BLOCK_IN = 128
SSM_BLOCK_STATE = 512


def _params(*semantics):
    return pltpu.CompilerParams(dimension_semantics=semantics, vmem_limit_bytes=V7X_VMEM_LIMIT_BYTES)


def _token_tile(rows_per_batch):
    for parts in range(1, rows_per_batch + 1):
        if rows_per_batch % parts == 0:
            tm = rows_per_batch // parts
            if tm <= 1152 and tm % 16 == 0:
                return tm
    raise ValueError("no token tile for %d rows" % rows_per_batch)


def _chunk(total, target):
    best = None
    for c in range(LANES, min(total, target) + 1, LANES):
        if total % c == 0:
            best = c
    if best is None:
        raise ValueError("no lane-aligned chunk for %d" % total)
    return best


def _mod_kernel(c_ref, w_ref, b_ref, o_ref):
    a = jax.nn.silu(c_ref[...])
    o_ref[...] = jnp.dot(a, w_ref[...], preferred_element_type=F32) + b_ref[...]


def _mod_table(cvec, w_mod, b_mod):
    depth, d, n = w_mod.shape
    tn = _chunk(n, 1536)
    return pl.pallas_call(
        _mod_kernel,
        grid=(depth, n // tn),
        in_specs=[
            pl.BlockSpec((MOD_ROWS, d), lambda l, j: (0, 0)),
            pl.BlockSpec((None, d, tn), lambda l, j: (l, 0, j)),
            pl.BlockSpec((None, 1, tn), lambda l, j: (l, 0, j)),
        ],
        out_specs=pl.BlockSpec((None, MOD_ROWS, tn), lambda l, j: (l, 0, j)),
        out_shape=jax.ShapeDtypeStruct((depth, MOD_ROWS, n), F32),
        compiler_params=_params("arbitrary", "arbitrary"),
        name="mod_table",
    )(cvec, w_mod, b_mod.reshape(depth, 1, n))


def _mod_rows(mod_ref, slab, tile_idx, tm, tpb, seq, ctx_row, d):
    b = tile_idx // tpb
    v_b = mod_ref[pl.ds(b, 1), slab * d:(slab + 1) * d]
    v_c = mod_ref[ctx_row:ctx_row + 1, slab * d:(slab + 1) * d]
    row = (tile_idx % tpb) * tm + lax.broadcasted_iota(jnp.int32, (tm, 1), 0)
    return jnp.where(row >= seq, v_c, v_b)


def _norm_mod(x, g, mod_ref, slab, tile_idx, tpb, seq, ctx_row):
    tm, d = x.shape
    shift = _mod_rows(mod_ref, slab, tile_idx, tm, tpb, seq, ctx_row, d)
    scale = _mod_rows(mod_ref, slab + 1, tile_idx, tm, tpb, seq, ctx_row, d)
    rs = lax.rsqrt(jnp.mean(x * x, axis=-1, keepdims=True) + NORM_EPS)
    return (x * rs) * g * (1.0 + scale) + shift


def _in_kernel(x_ref, g_ref, mod_ref, w_ref, u_ref, q_ref, k_ref, v_ref, *, tpb, seq, ctx_row, ssm_w, att_w):
    i = pl.program_id(0)
    h = _norm_mod(x_ref[...], g_ref[...], mod_ref, 0, i, tpb, seq, ctx_row).astype(BF16)
    c1 = ssm_w + att_w
    c2 = c1 + att_w
    u_ref[...] = jnp.dot(h, w_ref[:, 0:ssm_w], preferred_element_type=F32)
    q_ref[...] = (jnp.dot(h, w_ref[:, ssm_w:c1], preferred_element_type=F32) * (HEAD_DIM ** -0.5)).astype(BF16)
    k_ref[...] = jnp.dot(h, w_ref[:, c1:c2], preferred_element_type=F32).astype(BF16)
    v_ref[...] = jnp.dot(h, w_ref[:, c2:c2 + att_w], preferred_element_type=F32).astype(BF16)


def _in_proj(x_all, g, mod_l, w_in, *, tm, tpb, seq, ctx_row, ssm_w, att_w):
    t, d = x_all.shape
    n = w_in.shape[1]
    row_block = lambda width: pl.BlockSpec((tm, width), lambda i: (i, 0))
    return pl.pallas_call(
        functools.partial(_in_kernel, tpb=tpb, seq=seq, ctx_row=ctx_row, ssm_w=ssm_w, att_w=att_w),
        grid=(t // tm,),
        in_specs=[
            row_block(d),
            pl.BlockSpec((1, d), lambda i: (0, 0)),
            pl.BlockSpec(mod_l.shape, lambda i: (0, 0)),
            pl.BlockSpec((d, n), lambda i: (0, 0)),
        ],
        out_specs=[row_block(ssm_w), row_block(att_w), row_block(att_w), row_block(att_w)],
        out_shape=[
            jax.ShapeDtypeStruct((t, ssm_w), F32),
            jax.ShapeDtypeStruct((t, att_w), BF16),
            jax.ShapeDtypeStruct((t, att_w), BF16),
            jax.ShapeDtypeStruct((t, att_w), BF16),
        ],
        compiler_params=_params("arbitrary"),
        name="in_proj",
    )(x_all, g.reshape(1, d), mod_l, w_in)


def _att_kernel(q_ref, k_ref, v_ref, bias_ref, o_ref, plat_s, pctx_s, den_s, *, seq, n_ctx, rows, win, group):
    lane = lax.broadcasted_iota(jnp.int32, (1, LANES), 1)
    first_head = lane < HEAD_DIM
    mask0 = first_head.astype(BF16)
    mask1 = 1 - mask0
    trans_b = (((1,), (1,)), ((), ()))
    lane_tiles = lambda arrs: [a[:, c:c + LANES] for a in arrs for c in range(0, a.shape[1], LANES)]

    def stack(q):
        return jnp.concatenate([q * mask0, q * mask1], axis=0)

    def unstack(o):
        m_rows = o.shape[0] // 2
        return jnp.where(first_head, o[:m_rows], o[m_rows:])

    def probabilities(q, parts):
        q2 = stack(q)
        scores = []
        for keys, bias in parts:
            s = lax.dot_general(q2, keys, trans_b, preferred_element_type=F32)
            scores.append(s if bias is None else s + bias)
        m = jnp.max(functools.reduce(jnp.maximum, lane_tiles(scores)), axis=-1, keepdims=True)
        probs = [jnp.exp(s - m) for s in scores]
        den = jnp.sum(functools.reduce(lambda a, b: a + b, lane_tiles(probs)), axis=-1, keepdims=True)
        return [p.astype(BF16) for p in probs], den

    def window_start(r):
        return pl.multiple_of(jnp.clip(r - win // 2, 0, rows - win) * GRID_W, GRID_W)

    def score_stage(r, slot):
        q0 = pl.multiple_of(r * GRID_W, GRID_W)
        r0 = jnp.clip(r - win // 2, 0, rows - win)
        kw = k_ref[pl.ds(window_start(r), win * GRID_W), :]
        kc = k_ref[seq:seq + n_ctx, :]
        (p_lat, p_ctx), den = probabilities(q_ref[pl.ds(q0, GRID_W), :], [(kw, bias_ref[r - r0]), (kc, None)])
        plat_s[slot] = p_lat
        pctx_s[slot] = p_ctx
        den_s[slot] = jnp.broadcast_to(den, den_s.shape[1:])

    def value_stage(r, slot):
        q0 = pl.multiple_of(r * GRID_W, GRID_W)
        vw = v_ref[pl.ds(window_start(r), win * GRID_W), :]
        vc = v_ref[seq:seq + n_ctx, :]
        o = (jnp.dot(plat_s[slot], vw, preferred_element_type=F32)
             + jnp.dot(pctx_s[slot], vc, preferred_element_type=F32)) / den_s[slot]
        o_ref[pl.ds(q0, GRID_W), :] = unstack(o).astype(o_ref.dtype)

    n_groups = rows // group
    for g in range(group):
        score_stage(g, g)

    def pipelined(it, carry):
        cur = (it % 2) * group
        nxt = group - cur
        for g in range(group):
            value_stage(it * group + g, cur + g)
        for g in range(group):
            score_stage((it + 1) * group + g, nxt + g)
        return carry

    lax.fori_loop(0, n_groups - 1, pipelined, 0)
    last = ((n_groups - 1) % 2) * group
    for g in range(group):
        value_stage((n_groups - 1) * group + g, last + g)

    kc = k_ref[seq:seq + n_ctx, :]
    vc = v_ref[seq:seq + n_ctx, :]
    (p_ctx,), den = probabilities(q_ref[seq:seq + n_ctx, :], [(kc, None)])
    o = jnp.dot(p_ctx, vc, preferred_element_type=F32) / den
    o_ref[seq:seq + n_ctx, :] = unstack(o).astype(o_ref.dtype)


def _attention(q, k, v, bias, *, batch, rpb_rows, seq, n_ctx):
    t, att_w = q.shape
    rows = seq // GRID_W
    win = min(NA_ROWS_MAX, rows)
    n_pairs = att_w // LANES
    group = 2
    assert rows % group == 0
    blk = pl.BlockSpec((rpb_rows, LANES), lambda b, p: (b, p))
    return pl.pallas_call(
        functools.partial(_att_kernel, seq=seq, n_ctx=n_ctx, rows=rows, win=win, group=group),
        grid=(batch, n_pairs),
        in_specs=[blk, blk, blk,
                  pl.BlockSpec((None, win, 2 * GRID_W, win * GRID_W), lambda b, p: (p, 0, 0, 0))],
        out_specs=blk,
        out_shape=jax.ShapeDtypeStruct((t, att_w), BF16),
        scratch_shapes=[
            pltpu.VMEM((2 * group, 2 * GRID_W, win * GRID_W), BF16),
            pltpu.VMEM((2 * group, 2 * GRID_W, n_ctx), BF16),
            pltpu.VMEM((2 * group, 2 * GRID_W, LANES), F32),
        ],
        compiler_params=_params("arbitrary", "arbitrary"),
        name="attention",
    )(q, k, v, bias)


def _attention_bias(rpb, rows):
    n_heads = rpb.shape[0]
    win = min(NA_ROWS_MAX, rows)
    col = jnp.arange(GRID_W)
    col_start = jnp.clip(col - NA_COLS // 2, 0, GRID_W - NA_COLS)
    kcol = jnp.arange(GRID_W)
    valid = (kcol[None, :] >= col_start[:, None]) & (kcol[None, :] < col_start[:, None] + NA_COLS)
    col_rel = kcol[None, :] - col[:, None] + NA_COLS - 1
    var = jnp.arange(win)
    rr = jnp.arange(win)
    row_rel = rr[None, :] - var[:, None] + NA_ROWS_MAX - 1
    by_row = rpb.astype(F32)[:, row_rel]
    pick = (col_rel[:, :, None] == jnp.arange(2 * NA_COLS - 1)[None, None, :]).astype(F32)
    tab = jnp.einsum("hvrj,ckj->hvcrk", by_row, pick, precision=lax.Precision.HIGHEST)
    tab = jnp.where(valid[None, None, :, None, :], tab, MASK_VALUE)
    tab = tab.reshape(n_heads // 2, 2, win, GRID_W, win * GRID_W).transpose(0, 2, 1, 3, 4)
    return tab.reshape(n_heads // 2, win, 2 * GRID_W, win * GRID_W)


def _ssm_kernel(*refs, lc, reverse, first):
    if first:
        u_ref, lre_ref, lim_ref, bc_ref, cc_ref, dsk_ref, y_ref, hre_s, him_s, xre_s, xim_s = refs
        yprev_ref = None
    else:
        u_ref, yprev_ref, lre_ref, lim_ref, bc_ref, cc_ref, y_ref, hre_s, him_s, xre_s, xim_s = refs
    nb = u_ref.shape[1]
    width = u_ref.shape[2]
    n_state = xre_s.shape[2]
    n_blocks = width // SSM_BLOCK_IN

    @pl.when(pl.program_id(0) == 0)
    def _():
        hre_s[...] = jnp.zeros_like(hre_s)
        him_s[...] = jnp.zeros_like(him_s)

    u = u_ref[...].reshape(lc * nb, width)
    ub = u.astype(BF16)
    for jb in range(n_blocks):
        xb = jnp.dot(ub[:, jb * SSM_BLOCK_IN:(jb + 1) * SSM_BLOCK_IN], bc_ref[jb], preferred_element_type=F32)
        s0 = jb * SSM_BLOCK_STATE
        xre_s[:, :, s0:s0 + SSM_BLOCK_STATE] = xb[:, :SSM_BLOCK_STATE].reshape(lc, nb, SSM_BLOCK_STATE)
        xim_s[:, :, s0:s0 + SSM_BLOCK_STATE] = xb[:, SSM_BLOCK_STATE:].reshape(lc, nb, SSM_BLOCK_STATE)

    cw = 512
    for cb in range(n_state // cw):
        c0 = cb * cw
        lr = jnp.broadcast_to(lre_ref[:, c0:c0 + cw], (nb, cw))
        li = jnp.broadcast_to(lim_ref[:, c0:c0 + cw], (nb, cw))

        def step(t, carry, c0=c0, lr=lr, li=li):
            hr, hi = carry
            tt = (lc - 1 - t) if reverse else t
            nr = lr * hr - li * hi + xre_s[tt, :, c0:c0 + cw]
            ni = lr * hi + li * hr + xim_s[tt, :, c0:c0 + cw]
            xre_s[tt, :, c0:c0 + cw] = nr
            xim_s[tt, :, c0:c0 + cw] = ni
            return nr, ni

        hr, hi = lax.fori_loop(0, lc, step, (hre_s[:, c0:c0 + cw], him_s[:, c0:c0 + cw]), unroll=2)
        hre_s[:, c0:c0 + cw] = hr
        him_s[:, c0:c0 + cw] = hi

    if first:
        base = u * dsk_ref[...]
    else:
        base = yprev_ref[...].reshape(lc * nb, width)
    pair_state = 2 * SSM_BLOCK_STATE
    pair_out = 2 * SSM_BLOCK_IN
    for jp in range(n_blocks // 2):
        s0 = jp * pair_state
        h_re = xre_s[:, :, s0:s0 + pair_state].reshape(lc * nb, pair_state).astype(BF16)
        h_im = xim_s[:, :, s0:s0 + pair_state].reshape(lc * nb, pair_state).astype(BF16)
        yj = jnp.dot(jnp.concatenate([h_re, h_im], axis=1), cc_ref[jp], preferred_element_type=F32)
        o0 = jp * pair_out
        y_ref[:, :, o0:o0 + pair_out] = (base[:, o0:o0 + pair_out] + yj).reshape(lc, nb, pair_out)


def _ssm_direction(u_tm, y_prev, lam_re, lam_im, bc, cc, d_skip, *, lc, seq, n_ctx, reverse):
    total, nb, width = u_tm.shape
    n_state = lam_re.shape[1]
    n_ch = total // lc
    n_lat_ch = seq // lc
    n_ctx_ch = n_ctx // lc
    first = y_prev is None
    if reverse:
        chunk = lambda i: (n_ch - 1 - i, 0, 0)
    else:
        chunk = lambda i: (jnp.where(i < n_ctx_ch, n_lat_ch + i, i - n_ctx_ch), 0, 0)
    seq_block = pl.BlockSpec((lc, nb, width), chunk)
    whole = lambda a: pl.BlockSpec(a.shape, lambda i: (0,) * a.ndim)
    args = [u_tm] + ([] if first else [y_prev]) + [lam_re, lam_im, bc, cc] + ([d_skip] if first else [])
    in_specs = [seq_block] + ([] if first else [seq_block]) + [whole(lam_re), whole(lam_im), whole(bc), whole(cc)]
    if first:
        in_specs.append(whole(d_skip))
    return pl.pallas_call(
        functools.partial(_ssm_kernel, lc=lc, reverse=reverse, first=first),
        grid=(n_ch,),
        in_specs=in_specs,
        out_specs=seq_block,
        out_shape=jax.ShapeDtypeStruct((total, nb, width), F32),
        scratch_shapes=[
            pltpu.VMEM((nb, n_state), F32),
            pltpu.VMEM((nb, n_state), F32),
            pltpu.VMEM((lc, nb, n_state), F32),
            pltpu.VMEM((lc, nb, n_state), F32),
        ],
        compiler_params=_params("arbitrary"),
        name="ssm_bwd" if reverse else "ssm_fwd",
    )(*args)


def _ssm_params(a_re, a_im, log_step, b_re, b_im, c_re, c_im):
    g, p, h = b_re.shape
    a_re = jnp.minimum(a_re.astype(F32), A_RE_MAX)
    a_im = a_im.astype(F32)
    dt = jnp.exp(log_step.astype(F32))[:, None]
    mag = jnp.exp(a_re * dt)
    lam_re = mag * jnp.cos(a_im * dt)
    lam_im = mag * jnp.sin(a_im * dt)
    den = a_re * a_re + a_im * a_im
    z_re = ((lam_re - 1) * a_re + lam_im * a_im) / den
    z_im = (lam_im * a_re - (lam_re - 1) * a_im) / den
    b_re = b_re.astype(F32)
    b_im = b_im.astype(F32)
    bb_re = z_re[..., None] * b_re - z_im[..., None] * b_im
    bb_im = z_re[..., None] * b_im + z_im[..., None] * b_re
    gpb = SSM_BLOCK_IN // h
    nblk = g // gpb
    eye = jnp.eye(gpb, dtype=F32)

    def pack_in(bb):
        bb = bb.reshape(nblk, gpb, p, h)
        return jnp.einsum("ngph,gk->nghkp", bb, eye).reshape(nblk, gpb * h, gpb * p)

    def pack_out(c):
        c = c.reshape(nblk, gpb, h, p)
        return jnp.einsum("nghp,gk->ngpkh", c, eye).reshape(nblk, gpb * p, gpb * h)

    def pair_out(c):
        c = c.reshape(nblk // 2, 2, gpb * p, gpb * h)
        return jnp.einsum("nasb,ak->naskb", c, jnp.eye(2, dtype=F32)).reshape(nblk // 2, 2 * gpb * p, 2 * gpb * h)

    bc = jnp.concatenate([pack_in(bb_re), pack_in(bb_im)], axis=2).astype(BF16)
    cc = jnp.concatenate([pair_out(pack_out(c_re.astype(F32))), pair_out(-pack_out(c_im.astype(F32)))],
                         axis=1).astype(BF16)
    return lam_re.reshape(1, g * p), lam_im.reshape(1, g * p), bc, cc


def _mix_kernel(y_ref, att_ref, x_ref, mod_ref, gw_ref, gb_ref, wo_ref, o_ref, *, tpb, seq, ctx_row):
    i = pl.program_id(0)
    tm, d = x_ref.shape
    ssm_w = y_ref.shape[1]
    g = jax.nn.gelu(y_ref[...])
    z = jnp.dot(g.astype(BF16), gw_ref[...], preferred_element_type=F32) + gb_ref[...]
    s = g * jax.nn.sigmoid(z)
    o = (jnp.dot(s.astype(BF16), wo_ref[0:ssm_w, :], preferred_element_type=F32)
         + jnp.dot(att_ref[...], wo_ref[ssm_w:, :], preferred_element_type=F32))
    gate = _mod_rows(mod_ref, 2, i, tm, tpb, seq, ctx_row, d)
    o_ref[...] = x_ref[...] + gate * o


def _mix_out(y_ssm, y_att, x_all, mod_l, glu_w, glu_b, w_out, *, tm, tpb, seq, ctx_row):
    t, d = x_all.shape
    ssm_w = y_ssm.shape[1]
    att_w = y_att.shape[1]
    row_block = lambda width: pl.BlockSpec((tm, width), lambda i: (i, 0))
    whole = lambda a: pl.BlockSpec(a.shape, lambda i: (0, 0))
    gb = glu_b.reshape(1, ssm_w)
    return pl.pallas_call(
        functools.partial(_mix_kernel, tpb=tpb, seq=seq, ctx_row=ctx_row),
        grid=(t // tm,),
        in_specs=[row_block(ssm_w), row_block(att_w), row_block(d), whole(mod_l), whole(glu_w), whole(gb),
                  whole(w_out)],
        out_specs=row_block(d),
        out_shape=jax.ShapeDtypeStruct((t, d), F32),
        compiler_params=_params("arbitrary"),
        name="mix_out",
    )(y_ssm, y_att, x_all, mod_l, glu_w, gb, w_out)


def _ffn_kernel(x_ref, g_ref, mod_ref, w1_ref, w3_ref, w2_ref, o_ref, t_s, acc_s, *, tpb, seq, ctx_row):
    i = pl.program_id(0)
    tm, d = x_ref.shape
    nf = w1_ref.shape[0]
    t_s[...] = _norm_mod(x_ref[...], g_ref[...], mod_ref, 3, i, tpb, seq, ctx_row).astype(BF16)
    acc_s[...] = jnp.zeros_like(acc_s)

    def chunk(c, carry):
        t = t_s[...]
        h1 = jnp.dot(t, w1_ref[c], preferred_element_type=F32)
        h3 = jnp.dot(t, w3_ref[c], preferred_element_type=F32)
        a = (jax.nn.silu(h1) * h3).astype(BF16)
        acc_s[...] += jnp.dot(a, w2_ref[c], preferred_element_type=F32)
        return carry

    lax.fori_loop(0, nf, chunk, 0)
    gate = _mod_rows(mod_ref, 5, i, tm, tpb, seq, ctx_row, d)
    o_ref[...] = x_ref[...] + gate * acc_s[...]


def _dense_ffn(x_all, g, mod_l, w1, w3, w2, *, tm, tpb, seq, ctx_row):
    t, d = x_all.shape
    dff = w1.shape[1]
    tf = _chunk(dff, 256)
    nf = dff // tf
    w1c = w1.reshape(d, nf, tf).transpose(1, 0, 2)
    w3c = w3.reshape(d, nf, tf).transpose(1, 0, 2)
    w2c = w2.reshape(nf, tf, d)
    resident = lambda a: pl.BlockSpec(a.shape, lambda i: (0,) * a.ndim, pipeline_mode=pl.Buffered(1))
    return pl.pallas_call(
        functools.partial(_ffn_kernel, tpb=tpb, seq=seq, ctx_row=ctx_row),
        grid=(t // tm,),
        in_specs=[
            pl.BlockSpec((tm, d), lambda i: (i, 0)),
            pl.BlockSpec((1, d), lambda i: (0, 0)),
            pl.BlockSpec(mod_l.shape, lambda i: (0, 0)),
            resident(w1c), resident(w3c), resident(w2c),
        ],
        out_specs=pl.BlockSpec((tm, d), lambda i: (i, 0)),
        out_shape=jax.ShapeDtypeStruct((t, d), F32),
        scratch_shapes=[pltpu.VMEM((tm, d), BF16), pltpu.VMEM((tm, d), F32)],
        compiler_params=_params("arbitrary"),
        name="dense_ffn",
    )(x_all, g.reshape(1, d), mod_l, w1c, w3c, w2c)


def _route_kernel(x_ref, g_ref, mod_ref, r_ref, t_ref, route_ref, *, tpb, seq, ctx_row):
    i = pl.program_id(0)
    t = _norm_mod(x_ref[...], g_ref[...], mod_ref, 3, i, tpb, seq, ctx_row)
    t_ref[...] = t
    th = t.astype(BF16)
    tl = (t - th.astype(F32)).astype(BF16)
    r = r_ref[...]
    rh = r.astype(BF16)
    rl = (r - rh.astype(F32)).astype(BF16)
    dot = lambda a, b: jnp.dot(a, b, preferred_element_type=F32)
    logits = dot(th, rh) + (dot(th, rl) + dot(tl, rh)) + dot(tl, rl)
    tm = logits.shape[0]
    lane = lax.broadcasted_iota(jnp.int32, (tm, LANES), 1)
    lane_f = lane.astype(F32)
    neg_inf = jnp.float32(-jnp.inf)
    lg = jnp.where(lane < N_EXPERTS, logits, neg_inf)
    m1 = jnp.max(lg, axis=-1, keepdims=True)
    i1 = jnp.min(jnp.where(lg == m1, lane_f, float(LANES)), axis=-1, keepdims=True)
    lg2 = jnp.where(lane_f == i1, neg_inf, lg)
    m2 = jnp.max(lg2, axis=-1, keepdims=True)
    i2 = jnp.min(jnp.where(lg2 == m2, lane_f, float(LANES)), axis=-1, keepdims=True)
    e = jnp.exp(m2 - m1)
    g1 = 1.0 / (1.0 + e)
    g2 = e / (1.0 + e)
    route_ref[...] = jnp.where(lane == 0, i1, jnp.where(lane == 1, i2, jnp.where(lane == 2, g1, jnp.where(
        lane == 3, g2, 0.0))))


def _route(x_all, g, mod_l, router, *, tm, tpb, seq, ctx_row):
    t, d = x_all.shape
    r_pad = jnp.zeros((d, LANES), F32).at[:, :router.shape[1]].set(router.astype(F32))
    return pl.pallas_call(
        functools.partial(_route_kernel, tpb=tpb, seq=seq, ctx_row=ctx_row),
        grid=(t // tm,),
        in_specs=[
            pl.BlockSpec((tm, d), lambda i: (i, 0)),
            pl.BlockSpec((1, d), lambda i: (0, 0)),
            pl.BlockSpec(mod_l.shape, lambda i: (0, 0)),
            pl.BlockSpec((d, LANES), lambda i: (0, 0)),
        ],
        out_specs=[pl.BlockSpec((tm, d), lambda i: (i, 0)), pl.BlockSpec((tm, LANES), lambda i: (i, 0))],
        out_shape=[jax.ShapeDtypeStruct((t, d), F32), jax.ShapeDtypeStruct((t, LANES), F32)],
        compiler_params=_params("arbitrary"),
        name="moe_route",
    )(x_all, g.reshape(1, d), mod_l, r_pad)


def _row_copy(src_hbm, dst_vmem, sem, src_row, dst_row):
    return pltpu.make_async_copy(src_hbm.at[pl.ds(src_row, 1), :], dst_vmem.at[pl.ds(dst_row, 1), :], sem)


ISSUE_UNROLL = 8


ZERO_BLOCK_ROWS = 256


def _dispatch_kernel(fill_lo_ref, fill_hi_ref, pos_ref, t_ref, xs_hbm, zero_s, sems):
    i = pl.program_id(0)
    tm = t_ref.shape[0]
    per_trip = ISSUE_UNROLL // 2

    def issue(g, carry):
        for k in range(per_trip):
            r = g * per_trip + k
            for choice in range(2):
                pltpu.make_async_copy(t_ref.at[pl.ds(r, 1), :], xs_hbm.at[pl.ds(pos_ref[0, 0, 2 * r + choice], 1), :],
                                      sems.at[choice]).start()
        return carry

    lax.fori_loop(0, tm // per_trip, issue, 0)

    @pl.when(i == 0)
    def _():
        zero_s[...] = jnp.zeros_like(zero_s)
        zero_row = lambda row: pltpu.make_async_copy(zero_s.at[pl.ds(0, 1), :], xs_hbm.at[pl.ds(row, 1), :],
                                                     sems.at[2])
        zero_block = lambda blk: pltpu.make_async_copy(
            zero_s, xs_hbm.at[pl.ds(pl.multiple_of(blk * ZERO_BLOCK_ROWS, ZERO_BLOCK_ROWS), ZERO_BLOCK_ROWS), :],
            sems.at[2])

        def start_then_wait(copy, lo, hi):
            def start(k, carry):
                copy(k).start()
                return carry

            def wait(k, carry):
                copy(k).wait()
                return carry

            lax.fori_loop(lo, hi, start, 0)
            lax.fori_loop(lo, hi, wait, 0)

        for e in range(N_EXPERTS):
            start_then_wait(zero_row, fill_lo_ref[e], fill_hi_ref[e])
        start_then_wait(zero_block, fill_lo_ref[N_EXPERTS] // ZERO_BLOCK_ROWS,
                        fill_hi_ref[N_EXPERTS] // ZERO_BLOCK_ROWS)

    for choice in range(2):
        pltpu.make_async_copy(t_ref, xs_hbm.at[pl.ds(0, tm), :], sems.at[choice]).wait()


def _dispatch_rows(tokens, pos, fill_lo, fill_hi, *, tm, n_rows):
    t, d = tokens.shape
    steps = t // tm
    assert tm % (ISSUE_UNROLL // 2) == 0 and n_rows % ZERO_BLOCK_ROWS == 0
    grid_spec = pltpu.PrefetchScalarGridSpec(
        num_scalar_prefetch=2,
        grid=(steps,),
        in_specs=[
            pl.BlockSpec((1, 1, 2 * tm), lambda i, lo, hi: (i, 0, 0), memory_space=pltpu.SMEM),
            pl.BlockSpec((tm, d), lambda i, lo, hi: (i, 0)),
        ],
        out_specs=pl.BlockSpec(memory_space=pl.ANY),
        scratch_shapes=[pltpu.VMEM((ZERO_BLOCK_ROWS, d), tokens.dtype), pltpu.SemaphoreType.DMA((3,))],
    )
    return pl.pallas_call(
        _dispatch_kernel,
        grid_spec=grid_spec,
        out_shape=jax.ShapeDtypeStruct((n_rows, d), tokens.dtype),
        compiler_params=_params("arbitrary"),
        name="moe_dispatch",
    )(fill_lo, fill_hi, pos.reshape(steps, 1, 2 * tm), tokens)


def _gmm_kernel(te_ref, tv_ref, ts_ref, xs_ref, w1_ref, w3_ref, w2_ref, o_ref, xb_s, acc_s, *, nf):
    j = pl.program_id(0)
    f = pl.program_id(1)
    valid = tv_ref[j] > 0

    @pl.when(valid)
    def _():
        @pl.when(f == 0)
        def _():
            xb_s[...] = xs_ref[...].astype(BF16)
            acc_s[...] = jnp.zeros_like(acc_s)

        x = xb_s[...]
        h1 = jnp.dot(x, w1_ref[...], preferred_element_type=F32)
        h3 = jnp.dot(x, w3_ref[...], preferred_element_type=F32)
        a = (jax.nn.silu(h1) * h3).astype(BF16)
        acc_s[...] += jnp.dot(a, w2_ref[...], preferred_element_type=F32)

        @pl.when(f == nf - 1)
        def _():
            o_ref[...] = acc_s[...]

    @pl.when(jnp.logical_and(jnp.logical_not(valid), f == nf - 1))
    def _():
        o_ref[...] = jnp.zeros_like(o_ref)


def _grouped_swiglu(xs, tile_expert, tile_valid, tile_src, w1, w3, w2, *, tme):
    n, d = xs.shape
    dff = w1.shape[2]
    tf = _chunk(dff, 512)
    nf = dff // tf
    grid_spec = pltpu.PrefetchScalarGridSpec(
        num_scalar_prefetch=3,
        grid=(n // tme, nf),
        in_specs=[
            pl.BlockSpec((tme, d), lambda j, f, te, tv, ts: (ts[j], 0)),
            pl.BlockSpec((None, d, tf), lambda j, f, te, tv, ts: (te[j], 0, f)),
            pl.BlockSpec((None, d, tf), lambda j, f, te, tv, ts: (te[j], 0, f)),
            pl.BlockSpec((None, tf, d), lambda j, f, te, tv, ts: (te[j], f, 0)),
        ],
        out_specs=pl.BlockSpec((tme, d), lambda j, f, te, tv, ts: (j, 0)),
        scratch_shapes=[pltpu.VMEM((tme, d), BF16), pltpu.VMEM((tme, d), F32)],
    )
    return pl.pallas_call(
        functools.partial(_gmm_kernel, nf=nf),
        grid_spec=grid_spec,
        out_shape=jax.ShapeDtypeStruct((n, d), F32),
        compiler_params=_params("arbitrary", "arbitrary"),
        name="moe_experts",
    )(tile_expert, tile_valid, tile_src, xs, w1, w3, w2)


def _combine_kernel(pos_ref, pos_next_ref, ys_hbm, x_ref, route_ref, mod_ref, o_ref, buf, sems, *, tpb, seq,
                    ctx_row):
    i = pl.program_id(0)
    tm, d = x_ref.shape

    def issue_all(pos, slot):
        def issue(g, carry):
            for k in range(ISSUE_UNROLL // 2):
                r = g * (ISSUE_UNROLL // 2) + k
                _row_copy(ys_hbm, buf.at[slot, 0], sems.at[slot, 0], pos[0, 0, 2 * r], r).start()
                _row_copy(ys_hbm, buf.at[slot, 1], sems.at[slot, 1], pos[0, 0, 2 * r + 1], r).start()
            return carry

        lax.fori_loop(0, tm // (ISSUE_UNROLL // 2), issue, 0)

    @pl.when(i == 0)
    def _():
        issue_all(pos_ref, 0)

    @pl.when(i + 1 < pl.num_programs(0))
    def _():
        issue_all(pos_next_ref, (i + 1) % 2)

    slot = i % 2
    pltpu.make_async_copy(ys_hbm.at[pl.ds(0, tm), :], buf.at[slot, 0], sems.at[slot, 0]).wait()
    pltpu.make_async_copy(ys_hbm.at[pl.ds(0, tm), :], buf.at[slot, 1], sems.at[slot, 1]).wait()
    g1 = route_ref[:, 2:3]
    g2 = route_ref[:, 3:4]
    gate = _mod_rows(mod_ref, 5, i, tm, tpb, seq, ctx_row, d)
    o_ref[...] = x_ref[...] + gate * (g1 * buf[slot, 0] + g2 * buf[slot, 1])


def _combine(ys, pos, x_all, route, mod_l, *, tm, tpb, seq, ctx_row):
    t, d = x_all.shape
    steps = t // tm
    assert tm % (ISSUE_UNROLL // 2) == 0
    pos3 = pos.reshape(steps, 1, 2 * tm)
    return pl.pallas_call(
        functools.partial(_combine_kernel, tpb=tpb, seq=seq, ctx_row=ctx_row),
        grid=(steps,),
        in_specs=[
            pl.BlockSpec((1, 1, 2 * tm), lambda i: (i, 0, 0), memory_space=pltpu.SMEM),
            pl.BlockSpec((1, 1, 2 * tm), lambda i: (jnp.minimum(i + 1, steps - 1), 0, 0), memory_space=pltpu.SMEM),
            pl.BlockSpec(memory_space=pl.ANY),
            pl.BlockSpec((tm, d), lambda i: (i, 0)),
            pl.BlockSpec((tm, LANES), lambda i: (i, 0)),
            pl.BlockSpec(mod_l.shape, lambda i: (0, 0)),
        ],
        out_specs=pl.BlockSpec((tm, d), lambda i: (i, 0)),
        out_shape=jax.ShapeDtypeStruct((t, d), F32),
        scratch_shapes=[pltpu.VMEM((2, 2, tm, d), F32), pltpu.SemaphoreType.DMA((2, 2))],
        compiler_params=_params("arbitrary"),
        name="moe_combine",
    )(pos3, pos3, ys, x_all, route, mod_l)


def _moe_ffn(x_all, g, mod_l, router, w1, w3, w2, *, tm, tpb, seq, ctx_row, tme):
    t = x_all.shape[0]
    tokens, route = _route(x_all, g, mod_l, router, tm=tm, tpb=tpb, seq=seq, ctx_row=ctx_row)
    e_flat = route[:, 0:2].astype(jnp.int32).reshape(-1)
    onehot = (e_flat[:, None] == jnp.arange(N_EXPERTS, dtype=jnp.int32)[None, :]).astype(jnp.int32)
    csum = jnp.cumsum(onehot, axis=0)
    rank = jnp.take_along_axis(csum, e_flat[:, None], axis=1)[:, 0] - 1
    counts = csum[-1]
    padded = ((counts + tme - 1) // tme) * tme
    seg_end = jnp.cumsum(padded)
    seg_start = seg_end - padded
    pos = seg_start[e_flat] + rank
    n_tiles = -(-(2 * t + N_EXPERTS * (tme - 1)) // tme)
    tile_start = jnp.arange(n_tiles, dtype=jnp.int32) * tme
    tile_expert = jnp.minimum(jnp.sum((tile_start[:, None] >= seg_end[None, :]).astype(jnp.int32), axis=1),
                              N_EXPERTS - 1)
    tile_valid = (tile_start < seg_end[-1]).astype(jnp.int32)
    tile_src = jnp.minimum(jnp.arange(n_tiles, dtype=jnp.int32), seg_end[-1] // tme - 1)

    n_rows = n_tiles * tme
    fill_lo = jnp.concatenate([seg_start + counts, seg_end[-1:]]).astype(jnp.int32)
    fill_hi = jnp.concatenate([seg_end, jnp.full((1,), n_rows, jnp.int32)]).astype(jnp.int32)
    xs = _dispatch_rows(tokens, pos, fill_lo, fill_hi, tm=tm, n_rows=n_rows)
    ys = _grouped_swiglu(xs, tile_expert, tile_valid, tile_src, w1, w3, w2, tme=tme)
    return _combine(ys, pos, x_all, route, mod_l, tm=tm, tpb=tpb, seq=seq, ctx_row=ctx_row)


def _final_kernel(x_ref, g_ref, o_ref):
    x = x_ref[...]
    o_ref[...] = x * lax.rsqrt(jnp.mean(x * x, axis=-1, keepdims=True) + NORM_EPS) * g_ref[...]


def _final_norm(x3, g, *, seq):
    batch, _, d = x3.shape
    tr = _chunk(seq, 1024)
    return pl.pallas_call(
        _final_kernel,
        grid=(batch, seq // tr),
        in_specs=[pl.BlockSpec((None, tr, d), lambda b, j: (b, j, 0)), pl.BlockSpec((1, d), lambda b, j: (0, 0))],
        out_specs=pl.BlockSpec((None, tr, d), lambda b, j: (b, j, 0)),
        out_shape=jax.ShapeDtypeStruct((batch, seq, d), F32),
        compiler_params=_params("arbitrary", "arbitrary"),
        name="final_norm",
    )(x3, g.reshape(1, d))


def kernel(x, c, ctx, c_ctx, w_mod, b_mod, g_mix, g_ffn, w_in, w_out, ssm_a_re, ssm_a_im, ssm_log_step, ssm_b_re, ssm_b_im, ssm_c_re, ssm_c_im, ssm_d, glu_w, glu_b, na_rpb, ffn_w1, ffn_w3, ffn_w2, moe_router, moe_w1, moe_w3, moe_w2, g_final):
    batch, seq, d = x.shape
    n_ctx = ctx.shape[1]
    depth = w_mod.shape[0]
    ssm_w = ssm_d.shape[1]
    att_w = (w_in.shape[2] - ssm_w) // 3
    rpb_rows = seq + n_ctx
    assert batch == SUBLANES and batch < MOD_ROWS
    assert seq % GRID_W == 0 and att_w % LANES == 0 and ssm_w % SSM_BLOCK_IN == 0
    tm = _token_tile(rpb_rows)
    tpb = rpb_rows // tm
    lc = math.gcd(math.gcd(seq, n_ctx), 64)
    tme = 1024
    common = dict(tm=tm, tpb=tpb, seq=seq, ctx_row=batch)

    cvec = jnp.zeros((MOD_ROWS, d), F32).at[:batch].set(c.astype(F32)).at[batch].set(c_ctx.astype(F32))
    mod = _mod_table(cvec, w_mod.astype(F32), b_mod.astype(F32))
    x_all = jnp.concatenate([x, ctx], axis=1).astype(F32).reshape(batch * rpb_rows, d)

    for l in range(depth):
        mod_l = mod[l]
        u, q, k, v = _in_proj(x_all, g_mix[l].astype(F32), mod_l, w_in[l].astype(BF16), ssm_w=ssm_w, att_w=att_w,
                              **common)

        u_tm = u.reshape(batch, rpb_rows, ssm_w).transpose(1, 0, 2)
        y_tm = None
        for direction in range(2):
            lam_re, lam_im, bc, cc = _ssm_params(ssm_a_re[l, direction], ssm_a_im[l, direction],
                                                 ssm_log_step[l, direction], ssm_b_re[l, direction],
                                                 ssm_b_im[l, direction], ssm_c_re[l, direction],
                                                 ssm_c_im[l, direction])
            y_tm = _ssm_direction(u_tm, y_tm, lam_re, lam_im, bc, cc, ssm_d[l].astype(F32).reshape(1, ssm_w),
                                  lc=lc, seq=seq, n_ctx=n_ctx, reverse=direction == 1)
        y_ssm = y_tm.transpose(1, 0, 2).reshape(batch * rpb_rows, ssm_w)

        bias = _attention_bias(na_rpb[l], seq // GRID_W)
        y_att = _attention(q, k, v, bias, batch=batch, rpb_rows=rpb_rows, seq=seq, n_ctx=n_ctx)

        x_all = _mix_out(y_ssm, y_att, x_all, mod_l, glu_w[l].astype(BF16), glu_b[l].astype(F32),
                         w_out[l].astype(BF16), **common)

        if l % 2 == 0:
            x_all = _dense_ffn(x_all, g_ffn[l].astype(F32), mod_l, ffn_w1[l // 2].astype(BF16),
                               ffn_w3[l // 2].astype(BF16), ffn_w2[l // 2].astype(BF16), **common)
        else:
            x_all = _moe_ffn(x_all, g_ffn[l].astype(F32), mod_l, moe_router[l // 2],
                             moe_w1[l // 2].astype(BF16), moe_w3[l // 2].astype(BF16),
                             moe_w2[l // 2].astype(BF16), tme=tme, **common)

    return _final_norm(x_all.reshape(batch, rpb_rows, d), g_final.astype(F32), seq=seq).astype(x.dtype)
```

```python
import functools
import math

import jax
import jax.numpy as jnp
from jax import lax
from jax.experimental import pallas as pl
from jax.experimental.pallas import tpu as pltpu

F32 = jnp.float32
BF16 = jnp.bfloat16

GRID_W = 64
SSM_GROUP = 16
SSM_STATE = 64
HEAD_DIM = 64
NA_ROWS_MAX = 8
NA_COLS = 16
N_EXPERTS = 8
NORM_EPS = 1e-6
A_RE_MAX = -1e-4
MASK_VALUE = -1e30

LANES = 128
SUBLANES = 8
V7X_VMEM_LIMIT_BYTES = 56 * 1024 * 1024

MOD_ROWS = 16
SSM_BLOCK_IN = 128
SSM_BLOCK_STATE = 512


def _params(*semantics):
    return pltpu.CompilerParams(dimension_semantics=semantics, vmem_limit_bytes=V7X_VMEM_LIMIT_BYTES)


def _token_tile(rows_per_batch):
    for parts in range(1, rows_per_batch + 1):
        if rows_per_batch % parts == 0:
            tm = rows_per_batch // parts
            if tm <= 1152 and tm % 16 == 0:
                return tm
    raise ValueError("no token tile for %d rows" % rows_per_batch)


def _chunk(total, target):
    best = None
    for c in range(LANES, min(total, target) + 1, LANES):
        if total % c == 0:
            best = c
    if best is None:
        raise ValueError("no lane-aligned chunk for %d" % total)
    return best


def _mod_kernel(c_ref, w_ref, b_ref, o_ref):
    a = jax.nn.silu(c_ref[...])
    o_ref[...] = jnp.dot(a, w_ref[...], preferred_element_type=F32) + b_ref[...]


def _mod_table(cvec, w_mod, b_mod):
    depth, d, n = w_mod.shape
    tn = _chunk(n, 1536)
    return pl.pallas_call(
        _mod_kernel,
        grid=(depth, n // tn),
        in_specs=[
            pl.BlockSpec((MOD_ROWS, d), lambda l, j: (0, 0)),
            pl.BlockSpec((None, d, tn), lambda l, j: (l, 0, j)),
            pl.BlockSpec((None, 1, tn), lambda l, j: (l, 0, j)),
        ],
        out_specs=pl.BlockSpec((None, MOD_ROWS, tn), lambda l, j: (l, 0, j)),
        out_shape=jax.ShapeDtypeStruct((depth, MOD_ROWS, n), F32),
        compiler_params=_params("arbitrary", "arbitrary"),
        name="mod_table",
    )(cvec, w_mod, b_mod.reshape(depth, 1, n))


def _mod_rows(mod_ref, slab, tile_idx, tm, tpb, seq, ctx_row, d):
    b = tile_idx // tpb
    v_b = mod_ref[pl.ds(b, 1), slab * d:(slab + 1) * d]
    v_c = mod_ref[ctx_row:ctx_row + 1, slab * d:(slab + 1) * d]
    row = (tile_idx % tpb) * tm + lax.broadcasted_iota(jnp.int32, (tm, 1), 0)
    return jnp.where(row >= seq, v_c, v_b)


def _norm_mod(x, g, mod_ref, slab, tile_idx, tpb, seq, ctx_row):
    tm, d = x.shape
    shift = _mod_rows(mod_ref, slab, tile_idx, tm, tpb, seq, ctx_row, d)
    scale = _mod_rows(mod_ref, slab + 1, tile_idx, tm, tpb, seq, ctx_row, d)
    rs = lax.rsqrt(jnp.mean(x * x, axis=-1, keepdims=True) + NORM_EPS)
    return (x * rs) * g * (1.0 + scale) + shift


def _in_kernel(x_ref, g_ref, mod_ref, w_ref, u_ref, q_ref, k_ref, v_ref, *, tpb, seq, ctx_row, ssm_w, att_w):
    i = pl.program_id(0)
    h = _norm_mod(x_ref[...], g_ref[...], mod_ref, 0, i, tpb, seq, ctx_row).astype(BF16)
    c1 = ssm_w + att_w
    c2 = c1 + att_w
    u_ref[...] = jnp.dot(h, w_ref[:, 0:ssm_w], preferred_element_type=F32)
    q_ref[...] = (jnp.dot(h, w_ref[:, ssm_w:c1], preferred_element_type=F32) * (HEAD_DIM ** -0.5)).astype(BF16)
    k_ref[...] = jnp.dot(h, w_ref[:, c1:c2], preferred_element_type=F32).astype(BF16)
    v_ref[...] = jnp.dot(h, w_ref[:, c2:c2 + att_w], preferred_element_type=F32).astype(BF16)


def _in_proj(x_all, g, mod_l, w_in, *, tm, tpb, seq, ctx_row, ssm_w, att_w):
    t, d = x_all.shape
    n = w_in.shape[1]
    row_block = lambda width: pl.BlockSpec((tm, width), lambda i: (i, 0))
    return pl.pallas_call(
        functools.partial(_in_kernel, tpb=tpb, seq=seq, ctx_row=ctx_row, ssm_w=ssm_w, att_w=att_w),
        grid=(t // tm,),
        in_specs=[
            row_block(d),
            pl.BlockSpec((1, d), lambda i: (0, 0)),
            pl.BlockSpec(mod_l.shape, lambda i: (0, 0)),
            pl.BlockSpec((d, n), lambda i: (0, 0)),
        ],
        out_specs=[row_block(ssm_w), row_block(att_w), row_block(att_w), row_block(att_w)],
        out_shape=[
            jax.ShapeDtypeStruct((t, ssm_w), F32),
            jax.ShapeDtypeStruct((t, att_w), BF16),
            jax.ShapeDtypeStruct((t, att_w), BF16),
            jax.ShapeDtypeStruct((t, att_w), BF16),
        ],
        compiler_params=_params("arbitrary"),
        name="in_proj",
    )(x_all, g.reshape(1, d), mod_l, w_in)


def _att_kernel(q_ref, k_ref, v_ref, bias_ref, o_ref, plat_s, pctx_s, den_s, *, seq, n_ctx, rows, win, group):
    lane = lax.broadcasted_iota(jnp.int32, (1, LANES), 1)
    first_head = lane < HEAD_DIM
    mask0 = first_head.astype(BF16)
    mask1 = 1 - mask0
    trans_b = (((1,), (1,)), ((), ()))
    lane_tiles = lambda arrs: [a[:, c:c + LANES] for a in arrs for c in range(0, a.shape[1], LANES)]

    def stack(q):
        return jnp.concatenate([q * mask0, q * mask1], axis=0)

    def unstack(o):
        m_rows = o.shape[0] // 2
        return jnp.where(first_head, o[:m_rows], o[m_rows:])

    def probabilities(q, parts):
        q2 = stack(q)
        scores = []
        for keys, bias in parts:
            s = lax.dot_general(q2, keys, trans_b, preferred_element_type=F32)
            scores.append(s if bias is None else s + bias)
        m = jnp.max(functools.reduce(jnp.maximum, lane_tiles(scores)), axis=-1, keepdims=True)
        probs = [jnp.exp(s - m) for s in scores]
        den = jnp.sum(functools.reduce(lambda a, b: a + b, lane_tiles(probs)), axis=-1, keepdims=True)
        return [p.astype(BF16) for p in probs], den

    def window_start(r):
        return pl.multiple_of(jnp.clip(r - win // 2, 0, rows - win) * GRID_W, GRID_W)

    def score_stage(r, slot):
        q0 = pl.multiple_of(r * GRID_W, GRID_W)
        r0 = jnp.clip(r - win // 2, 0, rows - win)
        kw = k_ref[pl.ds(window_start(r), win * GRID_W), :]
        kc = k_ref[seq:seq + n_ctx, :]
        (p_lat, p_ctx), den = probabilities(q_ref[pl.ds(q0, GRID_W), :], [(kw, bias_ref[r - r0]), (kc, None)])
        plat_s[slot] = p_lat
        pctx_s[slot] = p_ctx
        den_s[slot] = jnp.broadcast_to(den, den_s.shape[1:])

    def value_stage(r, slot):
        q0 = pl.multiple_of(r * GRID_W, GRID_W)
        vw = v_ref[pl.ds(window_start(r), win * GRID_W), :]
        vc = v_ref[seq:seq + n_ctx, :]
        o = (jnp.dot(plat_s[slot], vw, preferred_element_type=F32)
             + jnp.dot(pctx_s[slot], vc, preferred_element_type=F32)) / den_s[slot]
        o_ref[pl.ds(q0, GRID_W), :] = unstack(o).astype(o_ref.dtype)

    n_groups = rows // group
    for g in range(group):
        score_stage(g, g)

    def pipelined(it, carry):
        cur = (it % 2) * group
        nxt = group - cur
        for g in range(group):
            value_stage(it * group + g, cur + g)
        for g in range(group):
            score_stage((it + 1) * group + g, nxt + g)
        return carry

    lax.fori_loop(0, n_groups - 1, pipelined, 0)
    last = ((n_groups - 1) % 2) * group
    for g in range(group):
        value_stage((n_groups - 1) * group + g, last + g)

    kc = k_ref[seq:seq + n_ctx, :]
    vc = v_ref[seq:seq + n_ctx, :]
    (p_ctx,), den = probabilities(q_ref[seq:seq + n_ctx, :], [(kc, None)])
    o = jnp.dot(p_ctx, vc, preferred_element_type=F32) / den
    o_ref[seq:seq + n_ctx, :] = unstack(o).astype(o_ref.dtype)


def _attention(q, k, v, bias, *, batch, rpb_rows, seq, n_ctx):
    t, att_w = q.shape
    rows = seq // GRID_W
    win = min(NA_ROWS_MAX, rows)
    n_pairs = att_w // LANES
    group = 4
    assert rows % group == 0
    blk = pl.BlockSpec((rpb_rows, LANES), lambda b, p: (b, p))
    return pl.pallas_call(
        functools.partial(_att_kernel, seq=seq, n_ctx=n_ctx, rows=rows, win=win, group=group),
        grid=(batch, n_pairs),
        in_specs=[blk, blk, blk,
                  pl.BlockSpec((None, win, 2 * GRID_W, win * GRID_W), lambda b, p: (p, 0, 0, 0))],
        out_specs=blk,
        out_shape=jax.ShapeDtypeStruct((t, att_w), BF16),
        scratch_shapes=[
            pltpu.VMEM((2 * group, 2 * GRID_W, win * GRID_W), BF16),
            pltpu.VMEM((2 * group, 2 * GRID_W, n_ctx), BF16),
            pltpu.VMEM((2 * group, 2 * GRID_W, LANES), F32),
        ],
        compiler_params=_params("arbitrary", "arbitrary"),
        name="attention",
    )(q, k, v, bias)


def _attention_bias(rpb, rows):
    n_heads = rpb.shape[0]
    win = min(NA_ROWS_MAX, rows)
    col = jnp.arange(GRID_W)
    col_start = jnp.clip(col - NA_COLS // 2, 0, GRID_W - NA_COLS)
    kcol = jnp.arange(GRID_W)
    valid = (kcol[None, :] >= col_start[:, None]) & (kcol[None, :] < col_start[:, None] + NA_COLS)
    col_rel = kcol[None, :] - col[:, None] + NA_COLS - 1
    var = jnp.arange(win)
    rr = jnp.arange(win)
    row_rel = rr[None, :] - var[:, None] + NA_ROWS_MAX - 1
    by_row = rpb.astype(F32)[:, row_rel]
    pick = (col_rel[:, :, None] == jnp.arange(2 * NA_COLS - 1)[None, None, :]).astype(F32)
    by_row = by_row.reshape(n_heads // 2, 2, win, win, 2 * NA_COLS - 1)
    tab = jnp.einsum("pavrj,ckj->pvacrk", by_row, pick, precision=lax.Precision.HIGHEST)
    tab = jnp.where(valid[None, None, None, :, None, :], tab, MASK_VALUE)
    return tab.reshape(n_heads // 2, win, 2 * GRID_W, win * GRID_W)


def _ssm_kernel(*refs, lc, reverse, first):
    if first:
        u_ref, lre_ref, lim_ref, bc_ref, cc_ref, dsk_ref, y_ref, hre_s, him_s, xre_s, xim_s = refs
        yprev_ref = None
    else:
        u_ref, yprev_ref, lre_ref, lim_ref, bc_ref, cc_ref, y_ref, hre_s, him_s, xre_s, xim_s = refs
    nb = u_ref.shape[1]
    width = u_ref.shape[2]
    n_state = xre_s.shape[2]
    n_blocks = width // SSM_BLOCK_IN

    @pl.when(pl.program_id(0) == 0)
    def _():
        hre_s[...] = jnp.zeros_like(hre_s)
        him_s[...] = jnp.zeros_like(him_s)

    u = u_ref[...].reshape(lc * nb, width)
    ub = u.astype(BF16)
    for jb in range(n_blocks):
        xb = jnp.dot(ub[:, jb * SSM_BLOCK_IN:(jb + 1) * SSM_BLOCK_IN], bc_ref[jb], preferred_element_type=F32)
        s0 = jb * SSM_BLOCK_STATE
        xre_s[:, :, s0:s0 + SSM_BLOCK_STATE] = xb[:, :SSM_BLOCK_STATE].reshape(lc, nb, SSM_BLOCK_STATE)
        xim_s[:, :, s0:s0 + SSM_BLOCK_STATE] = xb[:, SSM_BLOCK_STATE:].reshape(lc, nb, SSM_BLOCK_STATE)

    cw = 512
    for cb in range(n_state // cw):
        c0 = cb * cw
        lr = jnp.broadcast_to(lre_ref[:, c0:c0 + cw], (nb, cw))
        li = jnp.broadcast_to(lim_ref[:, c0:c0 + cw], (nb, cw))

        def step(t, carry, c0=c0, lr=lr, li=li):
            hr, hi = carry
            tt = (lc - 1 - t) if reverse else t
            nr = lr * hr - li * hi + xre_s[tt, :, c0:c0 + cw]
            ni = lr * hi + li * hr + xim_s[tt, :, c0:c0 + cw]
            xre_s[tt, :, c0:c0 + cw] = nr
            xim_s[tt, :, c0:c0 + cw] = ni
            return nr, ni

        hr, hi = lax.fori_loop(0, lc, step, (hre_s[:, c0:c0 + cw], him_s[:, c0:c0 + cw]), unroll=2)
        hre_s[:, c0:c0 + cw] = hr
        him_s[:, c0:c0 + cw] = hi

    if first:
        base = u * dsk_ref[...]
    else:
        base = yprev_ref[...].reshape(lc * nb, width)
    pair_state = 2 * SSM_BLOCK_STATE
    pair_out = 2 * SSM_BLOCK_IN
    for jp in range(n_blocks // 2):
        s0 = jp * pair_state
        h_re = xre_s[:, :, s0:s0 + pair_state].reshape(lc * nb, pair_state).astype(BF16)
        h_im = xim_s[:, :, s0:s0 + pair_state].reshape(lc * nb, pair_state).astype(BF16)
        yj = jnp.dot(jnp.concatenate([h_re, h_im], axis=1), cc_ref[jp], preferred_element_type=F32)
        o0 = jp * pair_out
        y_ref[:, :, o0:o0 + pair_out] = (base[:, o0:o0 + pair_out] + yj).reshape(lc, nb, pair_out)


def _ssm_direction(u_tm, y_prev, lam_re, lam_im, bc, cc, d_skip, *, lc, seq, n_ctx, reverse):
    total, nb, width = u_tm.shape
    n_state = lam_re.shape[1]
    n_ch = total // lc
    n_lat_ch = seq // lc
    n_ctx_ch = n_ctx // lc
    first = y_prev is None
    if reverse:
        chunk = lambda i: (n_ch - 1 - i, 0, 0)
    else:
        chunk = lambda i: (jnp.where(i < n_ctx_ch, n_lat_ch + i, i - n_ctx_ch), 0, 0)
    seq_block = pl.BlockSpec((lc, nb, width), chunk)
    whole = lambda a: pl.BlockSpec(a.shape, lambda i: (0,) * a.ndim)
    args = [u_tm] + ([] if first else [y_prev]) + [lam_re, lam_im, bc, cc] + ([d_skip] if first else [])
    in_specs = [seq_block] + ([] if first else [seq_block]) + [whole(lam_re), whole(lam_im), whole(bc), whole(cc)]
    if first:
        in_specs.append(whole(d_skip))
    return pl.pallas_call(
        functools.partial(_ssm_kernel, lc=lc, reverse=reverse, first=first),
        grid=(n_ch,),
        in_specs=in_specs,
        out_specs=seq_block,
        out_shape=jax.ShapeDtypeStruct((total, nb, width), F32),
        scratch_shapes=[
            pltpu.VMEM((nb, n_state), F32),
            pltpu.VMEM((nb, n_state), F32),
            pltpu.VMEM((lc, nb, n_state), F32),
            pltpu.VMEM((lc, nb, n_state), F32),
        ],
        compiler_params=_params("arbitrary"),
        name="ssm_bwd" if reverse else "ssm_fwd",
    )(*args)


def _ssm_params(a_re, a_im, log_step, b_re, b_im, c_re, c_im):
    g, p, h = b_re.shape
    a_re = jnp.minimum(a_re.astype(F32), A_RE_MAX)
    a_im = a_im.astype(F32)
    dt = jnp.exp(log_step.astype(F32))[:, None]
    mag = jnp.exp(a_re * dt)
    lam_re = mag * jnp.cos(a_im * dt)
    lam_im = mag * jnp.sin(a_im * dt)
    den = a_re * a_re + a_im * a_im
    z_re = ((lam_re - 1) * a_re + lam_im * a_im) / den
    z_im = (lam_im * a_re - (lam_re - 1) * a_im) / den
    b_re = b_re.astype(F32)
    b_im = b_im.astype(F32)
    bb_re = z_re[..., None] * b_re - z_im[..., None] * b_im
    bb_im = z_re[..., None] * b_im + z_im[..., None] * b_re
    gpb = SSM_BLOCK_IN // h
    nblk = g // gpb
    eye = jnp.eye(gpb, dtype=F32)

    def pack_in(bb):
        bb = bb.reshape(nblk, gpb, p, h)
        return jnp.einsum("ngph,gk->nghkp", bb, eye).reshape(nblk, gpb * h, gpb * p)

    def pack_out(c):
        c = c.reshape(nblk, gpb, h, p)
        return jnp.einsum("nghp,gk->ngpkh", c, eye).reshape(nblk, gpb * p, gpb * h)

    def pair_out(c):
        c = c.reshape(nblk // 2, 2, gpb * p, gpb * h)
        return jnp.einsum("nasb,ak->naskb", c, jnp.eye(2, dtype=F32)).reshape(nblk // 2, 2 * gpb * p, 2 * gpb * h)

    bc = jnp.concatenate([pack_in(bb_re), pack_in(bb_im)], axis=2).astype(BF16)
    cc = jnp.concatenate([pair_out(pack_out(c_re.astype(F32))), pair_out(-pack_out(c_im.astype(F32)))],
                         axis=1).astype(BF16)
    return lam_re.reshape(1, g * p), lam_im.reshape(1, g * p), bc, cc


def _mix_kernel(y_ref, att_ref, x_ref, mod_ref, gw_ref, gb_ref, wo_ref, o_ref, *, tpb, seq, ctx_row):
    i = pl.program_id(0)
    tm, d = x_ref.shape
    ssm_w = y_ref.shape[1]
    g = jax.nn.gelu(y_ref[...])
    z = jnp.dot(g.astype(BF16), gw_ref[...], preferred_element_type=F32) + gb_ref[...]
    s = g * jax.nn.sigmoid(z)
    o = (jnp.dot(s.astype(BF16), wo_ref[0:ssm_w, :], preferred_element_type=F32)
         + jnp.dot(att_ref[...], wo_ref[ssm_w:, :], preferred_element_type=F32))
    gate = _mod_rows(mod_ref, 2, i, tm, tpb, seq, ctx_row, d)
    o_ref[...] = x_ref[...] + gate * o


def _mix_out(y_ssm, y_att, x_all, mod_l, glu_w, glu_b, w_out, *, tm, tpb, seq, ctx_row):
    t, d = x_all.shape
    ssm_w = y_ssm.shape[1]
    att_w = y_att.shape[1]
    row_block = lambda width: pl.BlockSpec((tm, width), lambda i: (i, 0))
    whole = lambda a: pl.BlockSpec(a.shape, lambda i: (0, 0))
    gb = glu_b.reshape(1, ssm_w)
    return pl.pallas_call(
        functools.partial(_mix_kernel, tpb=tpb, seq=seq, ctx_row=ctx_row),
        grid=(t // tm,),
        in_specs=[row_block(ssm_w), row_block(att_w), row_block(d), whole(mod_l), whole(glu_w), whole(gb),
                  whole(w_out)],
        out_specs=row_block(d),
        out_shape=jax.ShapeDtypeStruct((t, d), F32),
        compiler_params=_params("arbitrary"),
        name="mix_out",
    )(y_ssm, y_att, x_all, mod_l, glu_w, gb, w_out)


def _ffn_kernel(x_ref, g_ref, mod_ref, w1_ref, w3_ref, w2_ref, o_ref, t_s, acc_s, *, tpb, seq, ctx_row):
    i = pl.program_id(0)
    tm, d = x_ref.shape
    nf = w1_ref.shape[0]
    t_s[...] = _norm_mod(x_ref[...], g_ref[...], mod_ref, 3, i, tpb, seq, ctx_row).astype(BF16)
    acc_s[...] = jnp.zeros_like(acc_s)

    def chunk(c, carry):
        t = t_s[...]
        h1 = jnp.dot(t, w1_ref[c], preferred_element_type=F32)
        h3 = jnp.dot(t, w3_ref[c], preferred_element_type=F32)
        a = (jax.nn.silu(h1) * h3).astype(BF16)
        acc_s[...] += jnp.dot(a, w2_ref[c], preferred_element_type=F32)
        return carry

    lax.fori_loop(0, nf, chunk, 0)
    gate = _mod_rows(mod_ref, 5, i, tm, tpb, seq, ctx_row, d)
    o_ref[...] = x_ref[...] + gate * acc_s[...]


def _dense_ffn(x_all, g, mod_l, w1, w3, w2, *, tm, tpb, seq, ctx_row):
    t, d = x_all.shape
    dff = w1.shape[1]
    tf = _chunk(dff, 256)
    nf = dff // tf
    w1c = w1.reshape(d, nf, tf).transpose(1, 0, 2)
    w3c = w3.reshape(d, nf, tf).transpose(1, 0, 2)
    w2c = w2.reshape(nf, tf, d)
    resident = lambda a: pl.BlockSpec(a.shape, lambda i: (0,) * a.ndim, pipeline_mode=pl.Buffered(1))
    return pl.pallas_call(
        functools.partial(_ffn_kernel, tpb=tpb, seq=seq, ctx_row=ctx_row),
        grid=(t // tm,),
        in_specs=[
            pl.BlockSpec((tm, d), lambda i: (i, 0)),
            pl.BlockSpec((1, d), lambda i: (0, 0)),
            pl.BlockSpec(mod_l.shape, lambda i: (0, 0)),
            resident(w1c), resident(w3c), resident(w2c),
        ],
        out_specs=pl.BlockSpec((tm, d), lambda i: (i, 0)),
        out_shape=jax.ShapeDtypeStruct((t, d), F32),
        scratch_shapes=[pltpu.VMEM((tm, d), BF16), pltpu.VMEM((tm, d), F32)],
        compiler_params=_params("arbitrary"),
        name="dense_ffn",
    )(x_all, g.reshape(1, d), mod_l, w1c, w3c, w2c)


def _route_kernel(x_ref, g_ref, mod_ref, r_ref, t_ref, route_ref, *, tpb, seq, ctx_row):
    i = pl.program_id(0)
    t = _norm_mod(x_ref[...], g_ref[...], mod_ref, 3, i, tpb, seq, ctx_row)
    t_ref[...] = t
    th = t.astype(BF16)
    tl = (t - th.astype(F32)).astype(BF16)
    r = r_ref[...]
    rh = r.astype(BF16)
    rl = (r - rh.astype(F32)).astype(BF16)
    dot = lambda a, b: jnp.dot(a, b, preferred_element_type=F32)
    logits = dot(th, rh) + (dot(th, rl) + dot(tl, rh)) + dot(tl, rl)
    tm = logits.shape[0]
    lane = lax.broadcasted_iota(jnp.int32, (tm, LANES), 1)
    lane_f = lane.astype(F32)
    neg_inf = jnp.float32(-jnp.inf)
    lg = jnp.where(lane < N_EXPERTS, logits, neg_inf)
    m1 = jnp.max(lg, axis=-1, keepdims=True)
    i1 = jnp.min(jnp.where(lg == m1, lane_f, float(LANES)), axis=-1, keepdims=True)
    lg2 = jnp.where(lane_f == i1, neg_inf, lg)
    m2 = jnp.max(lg2, axis=-1, keepdims=True)
    i2 = jnp.min(jnp.where(lg2 == m2, lane_f, float(LANES)), axis=-1, keepdims=True)
    e = jnp.exp(m2 - m1)
    g1 = 1.0 / (1.0 + e)
    g2 = e / (1.0 + e)
    route_ref[...] = jnp.where(lane == 0, i1, jnp.where(lane == 1, i2, jnp.where(lane == 2, g1, jnp.where(
        lane == 3, g2, 0.0))))


def _route(x_all, g, mod_l, router, *, tm, tpb, seq, ctx_row):
    t, d = x_all.shape
    r_pad = jnp.zeros((d, LANES), F32).at[:, :router.shape[1]].set(router.astype(F32))
    return pl.pallas_call(
        functools.partial(_route_kernel, tpb=tpb, seq=seq, ctx_row=ctx_row),
        grid=(t // tm,),
        in_specs=[
            pl.BlockSpec((tm, d), lambda i: (i, 0)),
            pl.BlockSpec((1, d), lambda i: (0, 0)),
            pl.BlockSpec(mod_l.shape, lambda i: (0, 0)),
            pl.BlockSpec((d, LANES), lambda i: (0, 0)),
        ],
        out_specs=[pl.BlockSpec((tm, d), lambda i: (i, 0)), pl.BlockSpec((tm, LANES), lambda i: (i, 0))],
        out_shape=[jax.ShapeDtypeStruct((t, d), F32), jax.ShapeDtypeStruct((t, LANES), F32)],
        compiler_params=_params("arbitrary"),
        name="moe_route",
    )(x_all, g.reshape(1, d), mod_l, r_pad)


def _row_copy(src_hbm, dst_vmem, sem, src_row, dst_row):
    return pltpu.make_async_copy(src_hbm.at[pl.ds(src_row, 1), :], dst_vmem.at[pl.ds(dst_row, 1), :], sem)


ISSUE_UNROLL = 8


ZERO_BLOCK_ROWS = 256


def _dispatch_kernel(fill_lo_ref, fill_hi_ref, pos_ref, t_ref, xs_hbm, zero_s, sems):
    i = pl.program_id(0)
    tm = t_ref.shape[0]
    per_trip = ISSUE_UNROLL // 2

    def issue(g, carry):
        for k in range(per_trip):
            r = g * per_trip + k
            for choice in range(2):
                pltpu.make_async_copy(t_ref.at[pl.ds(r, 1), :], xs_hbm.at[pl.ds(pos_ref[0, 0, 2 * r + choice], 1), :],
                                      sems.at[choice]).start()
        return carry

    lax.fori_loop(0, tm // per_trip, issue, 0)

    @pl.when(i == 0)
    def _():
        zero_s[...] = jnp.zeros_like(zero_s)
        zero_row = lambda row: pltpu.make_async_copy(zero_s.at[pl.ds(0, 1), :], xs_hbm.at[pl.ds(row, 1), :],
                                                     sems.at[2])
        zero_block = lambda blk: pltpu.make_async_copy(
            zero_s, xs_hbm.at[pl.ds(pl.multiple_of(blk * ZERO_BLOCK_ROWS, ZERO_BLOCK_ROWS), ZERO_BLOCK_ROWS), :],
            sems.at[2])

        def start_then_wait(copy, lo, hi):
            def start(k, carry):
                copy(k).start()
                return carry

            def wait(k, carry):
                copy(k).wait()
                return carry

            lax.fori_loop(lo, hi, start, 0)
            lax.fori_loop(lo, hi, wait, 0)

        for e in range(N_EXPERTS):
            start_then_wait(zero_row, fill_lo_ref[e], fill_hi_ref[e])
        start_then_wait(zero_block, fill_lo_ref[N_EXPERTS] // ZERO_BLOCK_ROWS,
                        fill_hi_ref[N_EXPERTS] // ZERO_BLOCK_ROWS)

    for choice in range(2):
        pltpu.make_async_copy(t_ref, xs_hbm.at[pl.ds(0, tm), :], sems.at[choice]).wait()


def _dispatch_rows(tokens, pos, fill_lo, fill_hi, *, tm, n_rows):
    t, d = tokens.shape
    steps = t // tm
    assert tm % (ISSUE_UNROLL // 2) == 0 and n_rows % ZERO_BLOCK_ROWS == 0
    grid_spec = pltpu.PrefetchScalarGridSpec(
        num_scalar_prefetch=2,
        grid=(steps,),
        in_specs=[
            pl.BlockSpec((1, 1, 2 * tm), lambda i, lo, hi: (i, 0, 0), memory_space=pltpu.SMEM),
            pl.BlockSpec((tm, d), lambda i, lo, hi: (i, 0)),
        ],
        out_specs=pl.BlockSpec(memory_space=pl.ANY),
        scratch_shapes=[pltpu.VMEM((ZERO_BLOCK_ROWS, d), tokens.dtype), pltpu.SemaphoreType.DMA((3,))],
    )
    return pl.pallas_call(
        _dispatch_kernel,
        grid_spec=grid_spec,
        out_shape=jax.ShapeDtypeStruct((n_rows, d), tokens.dtype),
        compiler_params=_params("arbitrary"),
        name="moe_dispatch",
    )(fill_lo, fill_hi, pos.reshape(steps, 1, 2 * tm), tokens)


def _gmm_kernel(te_ref, tv_ref, ts_ref, xs_ref, w1_ref, w3_ref, w2_ref, o_ref, xb_s, acc_s, *, nf):
    j = pl.program_id(0)
    f = pl.program_id(1)
    valid = tv_ref[j] > 0

    @pl.when(valid)
    def _():
        @pl.when(f == 0)
        def _():
            xb_s[...] = xs_ref[...].astype(BF16)
            acc_s[...] = jnp.zeros_like(acc_s)

        x = xb_s[...]
        h1 = jnp.dot(x, w1_ref[...], preferred_element_type=F32)
        h3 = jnp.dot(x, w3_ref[...], preferred_element_type=F32)
        a = (jax.nn.silu(h1) * h3).astype(BF16)
        acc_s[...] += jnp.dot(a, w2_ref[...], preferred_element_type=F32)

        @pl.when(f == nf - 1)
        def _():
            o_ref[...] = acc_s[...]

    @pl.when(jnp.logical_and(jnp.logical_not(valid), f == nf - 1))
    def _():
        o_ref[...] = jnp.zeros_like(o_ref)


def _grouped_swiglu(xs, tile_expert, tile_valid, tile_src, w1, w3, w2, *, tme, layer):
    n, d = xs.shape
    dff = w1.shape[3]
    tf = _chunk(dff, 896)
    nf = dff // tf
    grid_spec = pltpu.PrefetchScalarGridSpec(
        num_scalar_prefetch=3,
        grid=(n // tme, nf),
        in_specs=[
            pl.BlockSpec((tme, d), lambda j, f, te, tv, ts: (ts[j], 0)),
            pl.BlockSpec((None, None, d, tf), lambda j, f, te, tv, ts: (layer, te[j], 0, f)),
            pl.BlockSpec((None, None, d, tf), lambda j, f, te, tv, ts: (layer, te[j], 0, f)),
            pl.BlockSpec((None, None, tf, d), lambda j, f, te, tv, ts: (layer, te[j], f, 0)),
        ],
        out_specs=pl.BlockSpec((tme, d), lambda j, f, te, tv, ts: (j, 0)),
        scratch_shapes=[pltpu.VMEM((tme, d), BF16), pltpu.VMEM((tme, d), F32)],
    )
    return pl.pallas_call(
        functools.partial(_gmm_kernel, nf=nf),
        grid_spec=grid_spec,
        out_shape=jax.ShapeDtypeStruct((n, d), F32),
        compiler_params=_params("arbitrary", "arbitrary"),
        name="moe_experts",
    )(tile_expert, tile_valid, tile_src, xs, w1, w3, w2)


def _combine_kernel(pos_ref, pos_next_ref, ys_hbm, x_ref, route_ref, mod_ref, o_ref, buf, sems, *, tpb, seq,
                    ctx_row):
    i = pl.program_id(0)
    tm, d = x_ref.shape

    def issue_all(pos, slot):
        def issue(g, carry):
            for k in range(ISSUE_UNROLL // 2):
                r = g * (ISSUE_UNROLL // 2) + k
                _row_copy(ys_hbm, buf.at[slot, 0], sems.at[slot, 0], pos[0, 0, 2 * r], r).start()
                _row_copy(ys_hbm, buf.at[slot, 1], sems.at[slot, 1], pos[0, 0, 2 * r + 1], r).start()
            return carry

        lax.fori_loop(0, tm // (ISSUE_UNROLL // 2), issue, 0)

    @pl.when(i == 0)
    def _():
        issue_all(pos_ref, 0)

    @pl.when(i + 1 < pl.num_programs(0))
    def _():
        issue_all(pos_next_ref, (i + 1) % 2)

    slot = i % 2
    pltpu.make_async_copy(ys_hbm.at[pl.ds(0, tm), :], buf.at[slot, 0], sems.at[slot, 0]).wait()
    pltpu.make_async_copy(ys_hbm.at[pl.ds(0, tm), :], buf.at[slot, 1], sems.at[slot, 1]).wait()
    g1 = route_ref[:, 2:3]
    g2 = route_ref[:, 3:4]
    gate = _mod_rows(mod_ref, 5, i, tm, tpb, seq, ctx_row, d)
    o_ref[...] = x_ref[...] + gate * (g1 * buf[slot, 0] + g2 * buf[slot, 1])


def _combine(ys, pos, x_all, route, mod_l, *, tm, tpb, seq, ctx_row):
    t, d = x_all.shape
    steps = t // tm
    assert tm % (ISSUE_UNROLL // 2) == 0
    pos3 = pos.reshape(steps, 1, 2 * tm)
    return pl.pallas_call(
        functools.partial(_combine_kernel, tpb=tpb, seq=seq, ctx_row=ctx_row),
        grid=(steps,),
        in_specs=[
            pl.BlockSpec((1, 1, 2 * tm), lambda i: (i, 0, 0), memory_space=pltpu.SMEM),
            pl.BlockSpec((1, 1, 2 * tm), lambda i: (jnp.minimum(i + 1, steps - 1), 0, 0), memory_space=pltpu.SMEM),
            pl.BlockSpec(memory_space=pl.ANY),
            pl.BlockSpec((tm, d), lambda i: (i, 0)),
            pl.BlockSpec((tm, LANES), lambda i: (i, 0)),
            pl.BlockSpec(mod_l.shape, lambda i: (0, 0)),
        ],
        out_specs=pl.BlockSpec((tm, d), lambda i: (i, 0)),
        out_shape=jax.ShapeDtypeStruct((t, d), F32),
        scratch_shapes=[pltpu.VMEM((2, 2, tm, d), F32), pltpu.SemaphoreType.DMA((2, 2))],
        compiler_params=_params("arbitrary"),
        name="moe_combine",
    )(pos3, pos3, ys, x_all, route, mod_l)


def _moe_ffn(x_all, g, mod_l, router, w1, w3, w2, *, tm, tpb, seq, ctx_row, tme, layer):
    t = x_all.shape[0]
    tokens, route = _route(x_all, g, mod_l, router, tm=tm, tpb=tpb, seq=seq, ctx_row=ctx_row)
    e_flat = route[:, 0:2].astype(jnp.int32).reshape(-1)
    onehot = (e_flat[:, None] == jnp.arange(N_EXPERTS, dtype=jnp.int32)[None, :]).astype(jnp.int32)
    csum = jnp.cumsum(onehot, axis=0)
    rank = jnp.take_along_axis(csum, e_flat[:, None], axis=1)[:, 0] - 1
    counts = csum[-1]
    padded = ((counts + tme - 1) // tme) * tme
    seg_end = jnp.cumsum(padded)
    seg_start = seg_end - padded
    pos = seg_start[e_flat] + rank
    n_tiles = -(-(2 * t + N_EXPERTS * (tme - 1)) // tme)
    tile_start = jnp.arange(n_tiles, dtype=jnp.int32) * tme
    tile_expert = jnp.minimum(jnp.sum((tile_start[:, None] >= seg_end[None, :]).astype(jnp.int32), axis=1),
                              N_EXPERTS - 1)
    tile_valid = (tile_start < seg_end[-1]).astype(jnp.int32)
    tile_src = jnp.minimum(jnp.arange(n_tiles, dtype=jnp.int32), seg_end[-1] // tme - 1)

    n_rows = n_tiles * tme
    fill_lo = jnp.concatenate([seg_start + counts, seg_end[-1:]]).astype(jnp.int32)
    fill_hi = jnp.concatenate([seg_end, jnp.full((1,), n_rows, jnp.int32)]).astype(jnp.int32)
    xs = _dispatch_rows(tokens, pos, fill_lo, fill_hi, tm=tm, n_rows=n_rows)
    ys = _grouped_swiglu(xs, tile_expert, tile_valid, tile_src, w1, w3, w2, tme=tme, layer=layer)
    return _combine(ys, pos, x_all, route, mod_l, tm=tm, tpb=tpb, seq=seq, ctx_row=ctx_row)


def _final_kernel(x_ref, g_ref, o_ref):
    x = x_ref[...]
    o_ref[...] = x * lax.rsqrt(jnp.mean(x * x, axis=-1, keepdims=True) + NORM_EPS) * g_ref[...]


def _final_norm(x3, g, *, seq):
    batch, _, d = x3.shape
    tr = _chunk(seq, 1024)
    return pl.pallas_call(
        _final_kernel,
        grid=(batch, seq // tr),
        in_specs=[pl.BlockSpec((None, tr, d), lambda b, j: (b, j, 0)), pl.BlockSpec((1, d), lambda b, j: (0, 0))],
        out_specs=pl.BlockSpec((None, tr, d), lambda b, j: (b, j, 0)),
        out_shape=jax.ShapeDtypeStruct((batch, seq, d), F32),
        compiler_params=_params("arbitrary", "arbitrary"),
        name="final_norm",
    )(x3, g.reshape(1, d))


def kernel(x, c, ctx, c_ctx, w_mod, b_mod, g_mix, g_ffn, w_in, w_out, ssm_a_re, ssm_a_im, ssm_log_step, ssm_b_re, ssm_b_im, ssm_c_re, ssm_c_im, ssm_d, glu_w, glu_b, na_rpb, ffn_w1, ffn_w3, ffn_w2, moe_router, moe_w1, moe_w3, moe_w2, g_final):
    batch, seq, d = x.shape
    n_ctx = ctx.shape[1]
    depth = w_mod.shape[0]
    ssm_w = ssm_d.shape[1]
    att_w = (w_in.shape[2] - ssm_w) // 3
    rpb_rows = seq + n_ctx
    assert batch == SUBLANES and batch < MOD_ROWS
    assert seq % GRID_W == 0 and att_w % LANES == 0 and ssm_w % SSM_BLOCK_IN == 0
    tm = _token_tile(rpb_rows)
    tpb = rpb_rows // tm
    lc = math.gcd(math.gcd(seq, n_ctx), 64)
    tme = 1024
    common = dict(tm=tm, tpb=tpb, seq=seq, ctx_row=batch)

    cvec = jnp.zeros((MOD_ROWS, d), F32).at[:batch].set(c.astype(F32)).at[batch].set(c_ctx.astype(F32))
    mod = _mod_table(cvec, w_mod.astype(F32), b_mod.astype(F32))
    x_all = jnp.concatenate([x, ctx], axis=1).astype(F32).reshape(batch * rpb_rows, d)

    moe_w1_b, moe_w3_b, moe_w2_b = moe_w1.astype(BF16), moe_w3.astype(BF16), moe_w2.astype(BF16)

    for l in range(depth):
        mod_l = mod[l]
        u, q, k, v = _in_proj(x_all, g_mix[l].astype(F32), mod_l, w_in[l].astype(BF16), ssm_w=ssm_w, att_w=att_w,
                              **common)

        u_tm = u.reshape(batch, rpb_rows, ssm_w).transpose(1, 0, 2)
        y_tm = None
        for direction in range(2):
            lam_re, lam_im, bc, cc = _ssm_params(ssm_a_re[l, direction], ssm_a_im[l, direction],
                                                 ssm_log_step[l, direction], ssm_b_re[l, direction],
                                                 ssm_b_im[l, direction], ssm_c_re[l, direction],
                                                 ssm_c_im[l, direction])
            y_tm = _ssm_direction(u_tm, y_tm, lam_re, lam_im, bc, cc, ssm_d[l].astype(F32).reshape(1, ssm_w),
                                  lc=lc, seq=seq, n_ctx=n_ctx, reverse=direction == 1)
        y_ssm = y_tm.transpose(1, 0, 2).reshape(batch * rpb_rows, ssm_w)

        bias = _attention_bias(na_rpb[l], seq // GRID_W)
        y_att = _attention(q, k, v, bias, batch=batch, rpb_rows=rpb_rows, seq=seq, n_ctx=n_ctx)

        x_all = _mix_out(y_ssm, y_att, x_all, mod_l, glu_w[l].astype(BF16), glu_b[l].astype(F32),
                         w_out[l].astype(BF16), **common)

        if l % 2 == 0:
            x_all = _dense_ffn(x_all, g_ffn[l].astype(F32), mod_l, ffn_w1[l // 2].astype(BF16),
                               ffn_w3[l // 2].astype(BF16), ffn_w2[l // 2].astype(BF16), **common)
        else:
            x_all = _moe_ffn(x_all, g_ffn[l].astype(F32), mod_l, moe_router[l // 2], moe_w1_b, moe_w3_b, moe_w2_b,
                             tme=tme, layer=l // 2, **common)

    return _final_norm(x_all.reshape(batch, rpb_rows, d), g_final.astype(F32), seq=seq).astype(x.dtype)
```

```python
import functools
import math

import jax
import jax.numpy as jnp
from jax import lax
from jax.experimental import pallas as pl
from jax.experimental.pallas import tpu as pltpu

F32 = jnp.float32
BF16 = jnp.bfloat16

GRID_W = 64
SSM_GROUP = 16
SSM_STATE = 64
HEAD_DIM = 64
NA_ROWS_MAX = 8
NA_COLS = 16
N_EXPERTS = 8
NORM_EPS = 1e-6
A_RE_MAX = -1e-4
MASK_VALUE = -1e30

LANES = 128
SUBLANES = 8
V7X_VMEM_LIMIT_BYTES = 56 * 1024 * 1024

MOD_ROWS = 16
SSM_BLOCK_IN = 128
SSM_BLOCK_STATE = 512


def _params(*semantics):
    return pltpu.CompilerParams(dimension_semantics=semantics, vmem_limit_bytes=V7X_VMEM_LIMIT_BYTES)


def _token_tile(rows_per_batch):
    for parts in range(1, rows_per_batch + 1):
        if rows_per_batch % parts == 0:
            tm = rows_per_batch // parts
            if tm <= 1152 and tm % 16 == 0:
                return tm
    raise ValueError("no token tile for %d rows" % rows_per_batch)


def _chunk(total, target):
    best = None
    for c in range(LANES, min(total, target) + 1, LANES):
        if total % c == 0:
            best = c
    if best is None:
        raise ValueError("no lane-aligned chunk for %d" % total)
    return best


def _mod_kernel(c_ref, w_ref, b_ref, o_ref):
    a = jax.nn.silu(c_ref[...])
    o_ref[...] = jnp.dot(a, w_ref[...], preferred_element_type=F32) + b_ref[...]


def _mod_table(cvec, w_mod, b_mod):
    depth, d, n = w_mod.shape
    tn = _chunk(n, 1536)
    return pl.pallas_call(
        _mod_kernel,
        grid=(depth, n // tn),
        in_specs=[
            pl.BlockSpec((MOD_ROWS, d), lambda l, j: (0, 0)),
            pl.BlockSpec((None, d, tn), lambda l, j: (l, 0, j)),
            pl.BlockSpec((None, 1, tn), lambda l, j: (l, 0, j)),
        ],
        out_specs=pl.BlockSpec((None, MOD_ROWS, tn), lambda l, j: (l, 0, j)),
        out_shape=jax.ShapeDtypeStruct((depth, MOD_ROWS, n), F32),
        compiler_params=_params("arbitrary", "arbitrary"),
        name="mod_table",
    )(cvec, w_mod, b_mod.reshape(depth, 1, n))


def _mod_rows(mod_ref, slab, tile_idx, tm, tpb, seq, ctx_row, d):
    b = tile_idx // tpb
    v_b = mod_ref[pl.ds(b, 1), slab * d:(slab + 1) * d]
    v_c = mod_ref[ctx_row:ctx_row + 1, slab * d:(slab + 1) * d]
    row = (tile_idx % tpb) * tm + lax.broadcasted_iota(jnp.int32, (tm, 1), 0)
    return jnp.where(row >= seq, v_c, v_b)


def _norm_mod(x, g, mod_ref, slab, tile_idx, tpb, seq, ctx_row):
    tm, d = x.shape
    shift = _mod_rows(mod_ref, slab, tile_idx, tm, tpb, seq, ctx_row, d)
    scale = _mod_rows(mod_ref, slab + 1, tile_idx, tm, tpb, seq, ctx_row, d)
    rs = lax.rsqrt(jnp.mean(x * x, axis=-1, keepdims=True) + NORM_EPS)
    return (x * rs) * g * (1.0 + scale) + shift


def _in_kernel(x_ref, g_ref, mod_ref, w_ref, u_ref, q_ref, k_ref, v_ref, *, tpb, seq, ctx_row, ssm_w, att_w):
    i = pl.program_id(0)
    h = _norm_mod(x_ref[...], g_ref[...], mod_ref, 0, i, tpb, seq, ctx_row).astype(BF16)
    c1 = ssm_w + att_w
    c2 = c1 + att_w
    u_ref[...] = jnp.dot(h, w_ref[:, 0:ssm_w], preferred_element_type=F32)
    q_ref[...] = (jnp.dot(h, w_ref[:, ssm_w:c1], preferred_element_type=F32) * (HEAD_DIM ** -0.5)).astype(BF16)
    k_ref[...] = jnp.dot(h, w_ref[:, c1:c2], preferred_element_type=F32).astype(BF16)
    v_ref[...] = jnp.dot(h, w_ref[:, c2:c2 + att_w], preferred_element_type=F32).astype(BF16)


def _in_proj(x_all, g, mod_l, w_in, *, tm, tpb, seq, ctx_row, ssm_w, att_w):
    t, d = x_all.shape
    n = w_in.shape[1]
    row_block = lambda width: pl.BlockSpec((tm, width), lambda i: (i, 0))
    return pl.pallas_call(
        functools.partial(_in_kernel, tpb=tpb, seq=seq, ctx_row=ctx_row, ssm_w=ssm_w, att_w=att_w),
        grid=(t // tm,),
        in_specs=[
            row_block(d),
            pl.BlockSpec((1, d), lambda i: (0, 0)),
            pl.BlockSpec(mod_l.shape, lambda i: (0, 0)),
            pl.BlockSpec((d, n), lambda i: (0, 0)),
        ],
        out_specs=[row_block(ssm_w), row_block(att_w), row_block(att_w), row_block(att_w)],
        out_shape=[
            jax.ShapeDtypeStruct((t, ssm_w), F32),
            jax.ShapeDtypeStruct((t, att_w), BF16),
            jax.ShapeDtypeStruct((t, att_w), BF16),
            jax.ShapeDtypeStruct((t, att_w), BF16),
        ],
        compiler_params=_params("arbitrary"),
        name="in_proj",
    )(x_all, g.reshape(1, d), mod_l, w_in)


def _att_kernel(q_ref, k_ref, v_ref, bias_ref, o_ref, plat_s, pctx_s, den_s, *, seq, n_ctx, rows, win, group):
    lane = lax.broadcasted_iota(jnp.int32, (1, LANES), 1)
    first_head = lane < HEAD_DIM
    mask0 = first_head.astype(BF16)
    mask1 = 1 - mask0
    trans_b = (((1,), (1,)), ((), ()))
    lane_tiles = lambda arrs: [a[:, c:c + LANES] for a in arrs for c in range(0, a.shape[1], LANES)]

    def stack(q):
        return jnp.concatenate([q * mask0, q * mask1], axis=0)

    def unstack(o):
        m_rows = o.shape[0] // 2
        return jnp.where(first_head, o[:m_rows], o[m_rows:])

    def probabilities(q, parts):
        q2 = stack(q)
        scores = []
        for keys, bias in parts:
            s = lax.dot_general(q2, keys, trans_b, preferred_element_type=F32)
            scores.append(s if bias is None else s + bias)
        m = jnp.max(functools.reduce(jnp.maximum, lane_tiles(scores)), axis=-1, keepdims=True)
        probs = [jnp.exp(s - m) for s in scores]
        den = jnp.sum(functools.reduce(lambda a, b: a + b, lane_tiles(probs)), axis=-1, keepdims=True)
        return [p.astype(BF16) for p in probs], den

    def window_start(r):
        return pl.multiple_of(jnp.clip(r - win // 2, 0, rows - win) * GRID_W, GRID_W)

    def score_stage(r, slot):
        q0 = pl.multiple_of(r * GRID_W, GRID_W)
        r0 = jnp.clip(r - win // 2, 0, rows - win)
        kw = k_ref[pl.ds(window_start(r), win * GRID_W), :]
        kc = k_ref[seq:seq + n_ctx, :]
        (p_lat, p_ctx), den = probabilities(q_ref[pl.ds(q0, GRID_W), :], [(kw, bias_ref[r - r0]), (kc, None)])
        plat_s[slot] = p_lat
        pctx_s[slot] = p_ctx
        den_s[slot] = jnp.broadcast_to(den, den_s.shape[1:])

    def value_stage(r, slot):
        q0 = pl.multiple_of(r * GRID_W, GRID_W)
        vw = v_ref[pl.ds(window_start(r), win * GRID_W), :]
        vc = v_ref[seq:seq + n_ctx, :]
        o = (jnp.dot(plat_s[slot], vw, preferred_element_type=F32)
             + jnp.dot(pctx_s[slot], vc, preferred_element_type=F32)) / den_s[slot]
        o_ref[pl.ds(q0, GRID_W), :] = unstack(o).astype(o_ref.dtype)

    n_groups = rows // group
    for g in range(group):
        score_stage(g, g)

    def pipelined(it, carry):
        cur = (it % 2) * group
        nxt = group - cur
        for g in range(group):
            value_stage(it * group + g, cur + g)
        for g in range(group):
            score_stage((it + 1) * group + g, nxt + g)
        return carry

    lax.fori_loop(0, n_groups - 1, pipelined, 0)
    last = ((n_groups - 1) % 2) * group
    for g in range(group):
        value_stage((n_groups - 1) * group + g, last + g)

    kc = k_ref[seq:seq + n_ctx, :]
    vc = v_ref[seq:seq + n_ctx, :]
    (p_ctx,), den = probabilities(q_ref[seq:seq + n_ctx, :], [(kc, None)])
    o = jnp.dot(p_ctx, vc, preferred_element_type=F32) / den
    o_ref[seq:seq + n_ctx, :] = unstack(o).astype(o_ref.dtype)


def _attention(q, k, v, bias, *, batch, rpb_rows, seq, n_ctx):
    t, att_w = q.shape
    rows = seq // GRID_W
    win = min(NA_ROWS_MAX, rows)
    n_pairs = att_w // LANES
    group = 8
    assert rows % group == 0
    blk = pl.BlockSpec((rpb_rows, LANES), lambda b, p: (b, p))
    return pl.pallas_call(
        functools.partial(_att_kernel, seq=seq, n_ctx=n_ctx, rows=rows, win=win, group=group),
        grid=(batch, n_pairs),
        in_specs=[blk, blk, blk,
                  pl.BlockSpec((None, win, 2 * GRID_W, win * GRID_W), lambda b, p: (p, 0, 0, 0))],
        out_specs=blk,
        out_shape=jax.ShapeDtypeStruct((t, att_w), BF16),
        scratch_shapes=[
            pltpu.VMEM((2 * group, 2 * GRID_W, win * GRID_W), BF16),
            pltpu.VMEM((2 * group, 2 * GRID_W, n_ctx), BF16),
            pltpu.VMEM((2 * group, 2 * GRID_W, LANES), F32),
        ],
        compiler_params=_params("arbitrary", "arbitrary"),
        name="attention",
    )(q, k, v, bias)


def _attention_bias(rpb, rows):
    n_heads = rpb.shape[0]
    win = min(NA_ROWS_MAX, rows)
    col = jnp.arange(GRID_W)
    col_start = jnp.clip(col - NA_COLS // 2, 0, GRID_W - NA_COLS)
    kcol = jnp.arange(GRID_W)
    valid = (kcol[None, :] >= col_start[:, None]) & (kcol[None, :] < col_start[:, None] + NA_COLS)
    col_rel = kcol[None, :] - col[:, None] + NA_COLS - 1
    var = jnp.arange(win)
    rr = jnp.arange(win)
    row_rel = rr[None, :] - var[:, None] + NA_ROWS_MAX - 1
    by_row = rpb.astype(F32)[:, row_rel]
    pick = (col_rel[:, :, None] == jnp.arange(2 * NA_COLS - 1)[None, None, :]).astype(F32)
    by_row = by_row.reshape(n_heads // 2, 2, win, win, 2 * NA_COLS - 1)
    tab = jnp.einsum("pavrj,ckj->pvacrk", by_row, pick, precision=lax.Precision.HIGHEST)
    tab = jnp.where(valid[None, None, None, :, None, :], tab, MASK_VALUE)
    return tab.reshape(n_heads // 2, win, 2 * GRID_W, win * GRID_W)


def _ssm_kernel(*refs, lc, reverse, first):
    if first:
        u_ref, lre_ref, lim_ref, bc_ref, cc_ref, dsk_ref, y_ref, hre_s, him_s, xre_s, xim_s = refs
        yprev_ref = None
    else:
        u_ref, yprev_ref, lre_ref, lim_ref, bc_ref, cc_ref, y_ref, hre_s, him_s, xre_s, xim_s = refs
    nb = u_ref.shape[1]
    width = u_ref.shape[2]
    n_state = xre_s.shape[2]
    n_blocks = width // SSM_BLOCK_IN

    @pl.when(pl.program_id(0) == 0)
    def _():
        hre_s[...] = jnp.zeros_like(hre_s)
        him_s[...] = jnp.zeros_like(him_s)

    u = u_ref[...].reshape(lc * nb, width)
    ub = u.astype(BF16)
    for jb in range(n_blocks):
        xb = jnp.dot(ub[:, jb * SSM_BLOCK_IN:(jb + 1) * SSM_BLOCK_IN], bc_ref[jb], preferred_element_type=F32)
        s0 = jb * SSM_BLOCK_STATE
        xre_s[:, :, s0:s0 + SSM_BLOCK_STATE] = xb[:, :SSM_BLOCK_STATE].reshape(lc, nb, SSM_BLOCK_STATE)
        xim_s[:, :, s0:s0 + SSM_BLOCK_STATE] = xb[:, SSM_BLOCK_STATE:].reshape(lc, nb, SSM_BLOCK_STATE)

    cw = 512
    for cb in range(n_state // cw):
        c0 = cb * cw
        lr = jnp.broadcast_to(lre_ref[:, c0:c0 + cw], (nb, cw))
        li = jnp.broadcast_to(lim_ref[:, c0:c0 + cw], (nb, cw))

        def step(t, carry, c0=c0, lr=lr, li=li):
            hr, hi = carry
            tt = (lc - 1 - t) if reverse else t
            nr = lr * hr - li * hi + xre_s[tt, :, c0:c0 + cw]
            ni = lr * hi + li * hr + xim_s[tt, :, c0:c0 + cw]
            xre_s[tt, :, c0:c0 + cw] = nr
            xim_s[tt, :, c0:c0 + cw] = ni
            return nr, ni

        hr, hi = lax.fori_loop(0, lc, step, (hre_s[:, c0:c0 + cw], him_s[:, c0:c0 + cw]), unroll=2)
        hre_s[:, c0:c0 + cw] = hr
        him_s[:, c0:c0 + cw] = hi

    if first:
        base = u * dsk_ref[...]
    else:
        base = yprev_ref[...].reshape(lc * nb, width)
    pair_state = 2 * SSM_BLOCK_STATE
    pair_out = 2 * SSM_BLOCK_IN
    for jp in range(n_blocks // 2):
        s0 = jp * pair_state
        h_re = xre_s[:, :, s0:s0 + pair_state].reshape(lc * nb, pair_state).astype(BF16)
        h_im = xim_s[:, :, s0:s0 + pair_state].reshape(lc * nb, pair_state).astype(BF16)
        yj = jnp.dot(jnp.concatenate([h_re, h_im], axis=1), cc_ref[jp], preferred_element_type=F32)
        o0 = jp * pair_out
        y_ref[:, :, o0:o0 + pair_out] = (base[:, o0:o0 + pair_out] + yj).reshape(lc, nb, pair_out)


def _ssm_direction(u_tm, y_prev, lam_re, lam_im, bc, cc, d_skip, *, lc, seq, n_ctx, reverse):
    total, nb, width = u_tm.shape
    n_state = lam_re.shape[1]
    n_ch = total // lc
    n_lat_ch = seq // lc
    n_ctx_ch = n_ctx // lc
    first = y_prev is None
    if reverse:
        chunk = lambda i: (n_ch - 1 - i, 0, 0)
    else:
        chunk = lambda i: (jnp.where(i < n_ctx_ch, n_lat_ch + i, i - n_ctx_ch), 0, 0)
    seq_block = pl.BlockSpec((lc, nb, width), chunk)
    whole = lambda a: pl.BlockSpec(a.shape, lambda i: (0,) * a.ndim)
    args = [u_tm] + ([] if first else [y_prev]) + [lam_re, lam_im, bc, cc] + ([d_skip] if first else [])
    in_specs = [seq_block] + ([] if first else [seq_block]) + [whole(lam_re), whole(lam_im), whole(bc), whole(cc)]
    if first:
        in_specs.append(whole(d_skip))
    return pl.pallas_call(
        functools.partial(_ssm_kernel, lc=lc, reverse=reverse, first=first),
        grid=(n_ch,),
        in_specs=in_specs,
        out_specs=seq_block,
        out_shape=jax.ShapeDtypeStruct((total, nb, width), F32),
        scratch_shapes=[
            pltpu.VMEM((nb, n_state), F32),
            pltpu.VMEM((nb, n_state), F32),
            pltpu.VMEM((lc, nb, n_state), F32),
            pltpu.VMEM((lc, nb, n_state), F32),
        ],
        compiler_params=_params("arbitrary"),
        name="ssm_bwd" if reverse else "ssm_fwd",
    )(*args)


def _ssm_params(a_re, a_im, log_step, b_re, b_im, c_re, c_im):
    g, p, h = b_re.shape
    a_re = jnp.minimum(a_re.astype(F32), A_RE_MAX)
    a_im = a_im.astype(F32)
    dt = jnp.exp(log_step.astype(F32))[:, None]
    mag = jnp.exp(a_re * dt)
    lam_re = mag * jnp.cos(a_im * dt)
    lam_im = mag * jnp.sin(a_im * dt)
    den = a_re * a_re + a_im * a_im
    z_re = ((lam_re - 1) * a_re + lam_im * a_im) / den
    z_im = (lam_im * a_re - (lam_re - 1) * a_im) / den
    b_re = b_re.astype(F32)
    b_im = b_im.astype(F32)
    bb_re = z_re[..., None] * b_re - z_im[..., None] * b_im
    bb_im = z_re[..., None] * b_im + z_im[..., None] * b_re
    gpb = SSM_BLOCK_IN // h
    nblk = g // gpb
    eye = jnp.eye(gpb, dtype=F32)

    def pack_in(bb):
        bb = bb.reshape(nblk, gpb, p, h)
        return jnp.einsum("ngph,gk->nghkp", bb, eye).reshape(nblk, gpb * h, gpb * p)

    def pack_out(c):
        c = c.reshape(nblk, gpb, h, p)
        return jnp.einsum("nghp,gk->ngpkh", c, eye).reshape(nblk, gpb * p, gpb * h)

    def pair_out(c):
        c = c.reshape(nblk // 2, 2, gpb * p, gpb * h)
        return jnp.einsum("nasb,ak->naskb", c, jnp.eye(2, dtype=F32)).reshape(nblk // 2, 2 * gpb * p, 2 * gpb * h)

    bc = jnp.concatenate([pack_in(bb_re), pack_in(bb_im)], axis=2).astype(BF16)
    cc = jnp.concatenate([pair_out(pack_out(c_re.astype(F32))), pair_out(-pack_out(c_im.astype(F32)))],
                         axis=1).astype(BF16)
    return lam_re.reshape(1, g * p), lam_im.reshape(1, g * p), bc, cc


def _mix_kernel(y_ref, att_ref, x_ref, mod_ref, gw_ref, gb_ref, wo_ref, o_ref, *, tpb, seq, ctx_row):
    i = pl.program_id(0)
    tm, d = x_ref.shape
    ssm_w = y_ref.shape[1]
    g = jax.nn.gelu(y_ref[...])
    z = jnp.dot(g.astype(BF16), gw_ref[...], preferred_element_type=F32) + gb_ref[...]
    s = g * jax.nn.sigmoid(z)
    o = (jnp.dot(s.astype(BF16), wo_ref[0:ssm_w, :], preferred_element_type=F32)
         + jnp.dot(att_ref[...], wo_ref[ssm_w:, :], preferred_element_type=F32))
    gate = _mod_rows(mod_ref, 2, i, tm, tpb, seq, ctx_row, d)
    o_ref[...] = x_ref[...] + gate * o


def _mix_out(y_ssm, y_att, x_all, mod_l, glu_w, glu_b, w_out, *, tm, tpb, seq, ctx_row):
    t, d = x_all.shape
    ssm_w = y_ssm.shape[1]
    att_w = y_att.shape[1]
    row_block = lambda width: pl.BlockSpec((tm, width), lambda i: (i, 0))
    whole = lambda a: pl.BlockSpec(a.shape, lambda i: (0, 0))
    gb = glu_b.reshape(1, ssm_w)
    return pl.pallas_call(
        functools.partial(_mix_kernel, tpb=tpb, seq=seq, ctx_row=ctx_row),
        grid=(t // tm,),
        in_specs=[row_block(ssm_w), row_block(att_w), row_block(d), whole(mod_l), whole(glu_w), whole(gb),
                  whole(w_out)],
        out_specs=row_block(d),
        out_shape=jax.ShapeDtypeStruct((t, d), F32),
        compiler_params=_params("arbitrary"),
        name="mix_out",
    )(y_ssm, y_att, x_all, mod_l, glu_w, gb, w_out)


def _ffn_kernel(x_ref, g_ref, mod_ref, w1_ref, w3_ref, w2_ref, o_ref, t_s, acc_s, *, tpb, seq, ctx_row):
    i = pl.program_id(0)
    tm, d = x_ref.shape
    nf = w1_ref.shape[0]
    t_s[...] = _norm_mod(x_ref[...], g_ref[...], mod_ref, 3, i, tpb, seq, ctx_row).astype(BF16)
    acc_s[...] = jnp.zeros_like(acc_s)

    def chunk(c, carry):
        t = t_s[...]
        h1 = jnp.dot(t, w1_ref[c], preferred_element_type=F32)
        h3 = jnp.dot(t, w3_ref[c], preferred_element_type=F32)
        a = (jax.nn.silu(h1) * h3).astype(BF16)
        acc_s[...] += jnp.dot(a, w2_ref[c], preferred_element_type=F32)
        return carry

    lax.fori_loop(0, nf, chunk, 0)
    gate = _mod_rows(mod_ref, 5, i, tm, tpb, seq, ctx_row, d)
    o_ref[...] = x_ref[...] + gate * acc_s[...]


def _dense_ffn(x_all, g, mod_l, w1, w3, w2, *, tm, tpb, seq, ctx_row):
    t, d = x_all.shape
    dff = w1.shape[1]
    tf = _chunk(dff, 256)
    nf = dff // tf
    w1c = w1.reshape(d, nf, tf).transpose(1, 0, 2)
    w3c = w3.reshape(d, nf, tf).transpose(1, 0, 2)
    w2c = w2.reshape(nf, tf, d)
    resident = lambda a: pl.BlockSpec(a.shape, lambda i: (0,) * a.ndim, pipeline_mode=pl.Buffered(1))
    return pl.pallas_call(
        functools.partial(_ffn_kernel, tpb=tpb, seq=seq, ctx_row=ctx_row),
        grid=(t // tm,),
        in_specs=[
            pl.BlockSpec((tm, d), lambda i: (i, 0)),
            pl.BlockSpec((1, d), lambda i: (0, 0)),
            pl.BlockSpec(mod_l.shape, lambda i: (0, 0)),
            resident(w1c), resident(w3c), resident(w2c),
        ],
        out_specs=pl.BlockSpec((tm, d), lambda i: (i, 0)),
        out_shape=jax.ShapeDtypeStruct((t, d), F32),
        scratch_shapes=[pltpu.VMEM((tm, d), BF16), pltpu.VMEM((tm, d), F32)],
        compiler_params=_params("arbitrary"),
        name="dense_ffn",
    )(x_all, g.reshape(1, d), mod_l, w1c, w3c, w2c)


def _route_kernel(x_ref, g_ref, mod_ref, r_ref, t_ref, route_ref, *, tpb, seq, ctx_row):
    i = pl.program_id(0)
    t = _norm_mod(x_ref[...], g_ref[...], mod_ref, 3, i, tpb, seq, ctx_row)
    t_ref[...] = t
    th = t.astype(BF16)
    tl = (t - th.astype(F32)).astype(BF16)
    r = r_ref[...]
    rh = r.astype(BF16)
    rl = (r - rh.astype(F32)).astype(BF16)
    dot = lambda a, b: jnp.dot(a, b, preferred_element_type=F32)
    logits = dot(th, rh) + (dot(th, rl) + dot(tl, rh)) + dot(tl, rl)
    tm = logits.shape[0]
    lane = lax.broadcasted_iota(jnp.int32, (tm, LANES), 1)
    lane_f = lane.astype(F32)
    neg_inf = jnp.float32(-jnp.inf)
    lg = jnp.where(lane < N_EXPERTS, logits, neg_inf)
    m1 = jnp.max(lg, axis=-1, keepdims=True)
    i1 = jnp.min(jnp.where(lg == m1, lane_f, float(LANES)), axis=-1, keepdims=True)
    lg2 = jnp.where(lane_f == i1, neg_inf, lg)
    m2 = jnp.max(lg2, axis=-1, keepdims=True)
    i2 = jnp.min(jnp.where(lg2 == m2, lane_f, float(LANES)), axis=-1, keepdims=True)
    e = jnp.exp(m2 - m1)
    g1 = 1.0 / (1.0 + e)
    g2 = e / (1.0 + e)
    route_ref[...] = jnp.where(lane == 0, i1, jnp.where(lane == 1, i2, jnp.where(lane == 2, g1, jnp.where(
        lane == 3, g2, 0.0))))


def _route(x_all, g, mod_l, router, *, tm, tpb, seq, ctx_row):
    t, d = x_all.shape
    r_pad = jnp.zeros((d, LANES), F32).at[:, :router.shape[1]].set(router.astype(F32))
    return pl.pallas_call(
        functools.partial(_route_kernel, tpb=tpb, seq=seq, ctx_row=ctx_row),
        grid=(t // tm,),
        in_specs=[
            pl.BlockSpec((tm, d), lambda i: (i, 0)),
            pl.BlockSpec((1, d), lambda i: (0, 0)),
            pl.BlockSpec(mod_l.shape, lambda i: (0, 0)),
            pl.BlockSpec((d, LANES), lambda i: (0, 0)),
        ],
        out_specs=[pl.BlockSpec((tm, d), lambda i: (i, 0)), pl.BlockSpec((tm, LANES), lambda i: (i, 0))],
        out_shape=[jax.ShapeDtypeStruct((t, d), F32), jax.ShapeDtypeStruct((t, LANES), F32)],
        compiler_params=_params("arbitrary"),
        name="moe_route",
    )(x_all, g.reshape(1, d), mod_l, r_pad)


def _row_copy(src_hbm, dst_vmem, sem, src_row, dst_row):
    return pltpu.make_async_copy(src_hbm.at[pl.ds(src_row, 1), :], dst_vmem.at[pl.ds(dst_row, 1), :], sem)


ISSUE_UNROLL = 8


ZERO_BLOCK_ROWS = 256


def _dispatch_kernel(fill_lo_ref, fill_hi_ref, pos_ref, t_ref, xs_hbm, zero_s, sems):
    i = pl.program_id(0)
    tm = t_ref.shape[0]
    per_trip = ISSUE_UNROLL // 2

    def issue(g, carry):
        for k in range(per_trip):
            r = g * per_trip + k
            for choice in range(2):
                pltpu.make_async_copy(t_ref.at[pl.ds(r, 1), :], xs_hbm.at[pl.ds(pos_ref[0, 0, 2 * r + choice], 1), :],
                                      sems.at[choice]).start()
        return carry

    lax.fori_loop(0, tm // per_trip, issue, 0)

    @pl.when(i == 0)
    def _():
        zero_s[...] = jnp.zeros_like(zero_s)
        zero_row = lambda row: pltpu.make_async_copy(zero_s.at[pl.ds(0, 1), :], xs_hbm.at[pl.ds(row, 1), :],
                                                     sems.at[2])
        zero_block = lambda blk: pltpu.make_async_copy(
            zero_s, xs_hbm.at[pl.ds(pl.multiple_of(blk * ZERO_BLOCK_ROWS, ZERO_BLOCK_ROWS), ZERO_BLOCK_ROWS), :],
            sems.at[2])

        def start_then_wait(copy, lo, hi):
            def start(k, carry):
                copy(k).start()
                return carry

            def wait(k, carry):
                copy(k).wait()
                return carry

            lax.fori_loop(lo, hi, start, 0)
            lax.fori_loop(lo, hi, wait, 0)

        for e in range(N_EXPERTS):
            start_then_wait(zero_row, fill_lo_ref[e], fill_hi_ref[e])
        start_then_wait(zero_block, fill_lo_ref[N_EXPERTS] // ZERO_BLOCK_ROWS,
                        fill_hi_ref[N_EXPERTS] // ZERO_BLOCK_ROWS)

    for choice in range(2):
        pltpu.make_async_copy(t_ref, xs_hbm.at[pl.ds(0, tm), :], sems.at[choice]).wait()


def _dispatch_rows(tokens, pos, fill_lo, fill_hi, *, tm, n_rows):
    t, d = tokens.shape
    steps = t // tm
    assert tm % (ISSUE_UNROLL // 2) == 0 and n_rows % ZERO_BLOCK_ROWS == 0
    grid_spec = pltpu.PrefetchScalarGridSpec(
        num_scalar_prefetch=2,
        grid=(steps,),
        in_specs=[
            pl.BlockSpec((1, 1, 2 * tm), lambda i, lo, hi: (i, 0, 0), memory_space=pltpu.SMEM),
            pl.BlockSpec((tm, d), lambda i, lo, hi: (i, 0)),
        ],
        out_specs=pl.BlockSpec(memory_space=pl.ANY),
        scratch_shapes=[pltpu.VMEM((ZERO_BLOCK_ROWS, d), tokens.dtype), pltpu.SemaphoreType.DMA((3,))],
    )
    return pl.pallas_call(
        _dispatch_kernel,
        grid_spec=grid_spec,
        out_shape=jax.ShapeDtypeStruct((n_rows, d), tokens.dtype),
        compiler_params=_params("arbitrary"),
        name="moe_dispatch",
    )(fill_lo, fill_hi, pos.reshape(steps, 1, 2 * tm), tokens)


def _gmm_kernel(te_ref, tv_ref, ts_ref, xs_ref, w1_ref, w3_ref, w2_ref, o_ref, xb_s, acc_s, *, nf):
    j = pl.program_id(0)
    f = pl.program_id(1)
    valid = tv_ref[j] > 0

    @pl.when(valid)
    def _():
        @pl.when(f == 0)
        def _():
            xb_s[...] = xs_ref[...].astype(BF16)
            acc_s[...] = jnp.zeros_like(acc_s)

        x = xb_s[...]
        h1 = jnp.dot(x, w1_ref[...].astype(BF16), preferred_element_type=F32)
        h3 = jnp.dot(x, w3_ref[...].astype(BF16), preferred_element_type=F32)
        a = (jax.nn.silu(h1) * h3).astype(BF16)
        acc_s[...] += jnp.dot(a, w2_ref[...].astype(BF16), preferred_element_type=F32)

        @pl.when(f == nf - 1)
        def _():
            o_ref[...] = acc_s[...]

    @pl.when(jnp.logical_and(jnp.logical_not(valid), f == nf - 1))
    def _():
        o_ref[...] = jnp.zeros_like(o_ref)


def _grouped_swiglu(xs, tile_expert, tile_valid, tile_src, w1, w3, w2, *, tme, layer):
    n, d = xs.shape
    dff = w1.shape[3]
    tf = _chunk(dff, 512)
    nf = dff // tf
    grid_spec = pltpu.PrefetchScalarGridSpec(
        num_scalar_prefetch=3,
        grid=(n // tme, nf),
        in_specs=[
            pl.BlockSpec((tme, d), lambda j, f, te, tv, ts: (ts[j], 0)),
            pl.BlockSpec((None, None, d, tf), lambda j, f, te, tv, ts: (layer, te[j], 0, f)),
            pl.BlockSpec((None, None, d, tf), lambda j, f, te, tv, ts: (layer, te[j], 0, f)),
            pl.BlockSpec((None, None, tf, d), lambda j, f, te, tv, ts: (layer, te[j], f, 0)),
        ],
        out_specs=pl.BlockSpec((tme, d), lambda j, f, te, tv, ts: (j, 0)),
        scratch_shapes=[pltpu.VMEM((tme, d), BF16), pltpu.VMEM((tme, d), F32)],
    )
    return pl.pallas_call(
        functools.partial(_gmm_kernel, nf=nf),
        grid_spec=grid_spec,
        out_shape=jax.ShapeDtypeStruct((n, d), F32),
        compiler_params=_params("arbitrary", "arbitrary"),
        name="moe_experts",
    )(tile_expert, tile_valid, tile_src, xs, w1, w3, w2)


def _combine_kernel(pos_ref, pos_next_ref, ys_hbm, x_ref, route_ref, mod_ref, o_ref, buf, sems, *, tpb, seq,
                    ctx_row):
    i = pl.program_id(0)
    tm, d = x_ref.shape

    def issue_all(pos, slot):
        def issue(g, carry):
            for k in range(ISSUE_UNROLL // 2):
                r = g * (ISSUE_UNROLL // 2) + k
                _row_copy(ys_hbm, buf.at[slot, 0], sems.at[slot, 0], pos[0, 0, 2 * r], r).start()
                _row_copy(ys_hbm, buf.at[slot, 1], sems.at[slot, 1], pos[0, 0, 2 * r + 1], r).start()
            return carry

        lax.fori_loop(0, tm // (ISSUE_UNROLL // 2), issue, 0)

    @pl.when(i == 0)
    def _():
        issue_all(pos_ref, 0)

    @pl.when(i + 1 < pl.num_programs(0))
    def _():
        issue_all(pos_next_ref, (i + 1) % 2)

    slot = i % 2
    pltpu.make_async_copy(ys_hbm.at[pl.ds(0, tm), :], buf.at[slot, 0], sems.at[slot, 0]).wait()
    pltpu.make_async_copy(ys_hbm.at[pl.ds(0, tm), :], buf.at[slot, 1], sems.at[slot, 1]).wait()
    g1 = route_ref[:, 2:3]
    g2 = route_ref[:, 3:4]
    gate = _mod_rows(mod_ref, 5, i, tm, tpb, seq, ctx_row, d)
    o_ref[...] = x_ref[...] + gate * (g1 * buf[slot, 0] + g2 * buf[slot, 1])


def _combine(ys, pos, x_all, route, mod_l, *, tm, tpb, seq, ctx_row):
    t, d = x_all.shape
    steps = t // tm
    assert tm % (ISSUE_UNROLL // 2) == 0
    pos3 = pos.reshape(steps, 1, 2 * tm)
    return pl.pallas_call(
        functools.partial(_combine_kernel, tpb=tpb, seq=seq, ctx_row=ctx_row),
        grid=(steps,),
        in_specs=[
            pl.BlockSpec((1, 1, 2 * tm), lambda i: (i, 0, 0), memory_space=pltpu.SMEM),
            pl.BlockSpec((1, 1, 2 * tm), lambda i: (jnp.minimum(i + 1, steps - 1), 0, 0), memory_space=pltpu.SMEM),
            pl.BlockSpec(memory_space=pl.ANY),
            pl.BlockSpec((tm, d), lambda i: (i, 0)),
            pl.BlockSpec((tm, LANES), lambda i: (i, 0)),
            pl.BlockSpec(mod_l.shape, lambda i: (0, 0)),
        ],
        out_specs=pl.BlockSpec((tm, d), lambda i: (i, 0)),
        out_shape=jax.ShapeDtypeStruct((t, d), F32),
        scratch_shapes=[pltpu.VMEM((2, 2, tm, d), F32), pltpu.SemaphoreType.DMA((2, 2))],
        compiler_params=_params("arbitrary"),
        name="moe_combine",
    )(pos3, pos3, ys, x_all, route, mod_l)


def _moe_ffn(x_all, g, mod_l, router, w1, w3, w2, *, tm, tpb, seq, ctx_row, tme, layer):
    t = x_all.shape[0]
    tokens, route = _route(x_all, g, mod_l, router, tm=tm, tpb=tpb, seq=seq, ctx_row=ctx_row)
    e_flat = route[:, 0:2].astype(jnp.int32).reshape(-1)
    onehot = (e_flat[:, None] == jnp.arange(N_EXPERTS, dtype=jnp.int32)[None, :]).astype(jnp.int32)
    csum = jnp.cumsum(onehot, axis=0)
    rank = jnp.take_along_axis(csum, e_flat[:, None], axis=1)[:, 0] - 1
    counts = csum[-1]
    padded = ((counts + tme - 1) // tme) * tme
    seg_end = jnp.cumsum(padded)
    seg_start = seg_end - padded
    pos = seg_start[e_flat] + rank
    n_tiles = -(-(2 * t + N_EXPERTS * (tme - 1)) // tme)
    tile_start = jnp.arange(n_tiles, dtype=jnp.int32) * tme
    tile_expert = jnp.minimum(jnp.sum((tile_start[:, None] >= seg_end[None, :]).astype(jnp.int32), axis=1),
                              N_EXPERTS - 1)
    tile_valid = (tile_start < seg_end[-1]).astype(jnp.int32)
    tile_src = jnp.minimum(jnp.arange(n_tiles, dtype=jnp.int32), seg_end[-1] // tme - 1)

    n_rows = n_tiles * tme
    fill_lo = jnp.concatenate([seg_start + counts, seg_end[-1:]]).astype(jnp.int32)
    fill_hi = jnp.concatenate([seg_end, jnp.full((1,), n_rows, jnp.int32)]).astype(jnp.int32)
    xs = _dispatch_rows(tokens, pos, fill_lo, fill_hi, tm=tm, n_rows=n_rows)
    ys = _grouped_swiglu(xs, tile_expert, tile_valid, tile_src, w1, w3, w2, tme=tme, layer=layer)
    return _combine(ys, pos, x_all, route, mod_l, tm=tm, tpb=tpb, seq=seq, ctx_row=ctx_row)


def _final_kernel(x_ref, g_ref, o_ref):
    x = x_ref[...]
    o_ref[...] = x * lax.rsqrt(jnp.mean(x * x, axis=-1, keepdims=True) + NORM_EPS) * g_ref[...]


def _final_norm(x3, g, *, seq):
    batch, _, d = x3.shape
    tr = _chunk(seq, 1024)
    return pl.pallas_call(
        _final_kernel,
        grid=(batch, seq // tr),
        in_specs=[pl.BlockSpec((None, tr, d), lambda b, j: (b, j, 0)), pl.BlockSpec((1, d), lambda b, j: (0, 0))],
        out_specs=pl.BlockSpec((None, tr, d), lambda b, j: (b, j, 0)),
        out_shape=jax.ShapeDtypeStruct((batch, seq, d), F32),
        compiler_params=_params("arbitrary", "arbitrary"),
        name="final_norm",
    )(x3, g.reshape(1, d))


def kernel(x, c, ctx, c_ctx, w_mod, b_mod, g_mix, g_ffn, w_in, w_out, ssm_a_re, ssm_a_im, ssm_log_step, ssm_b_re, ssm_b_im, ssm_c_re, ssm_c_im, ssm_d, glu_w, glu_b, na_rpb, ffn_w1, ffn_w3, ffn_w2, moe_router, moe_w1, moe_w3, moe_w2, g_final):
    batch, seq, d = x.shape
    n_ctx = ctx.shape[1]
    depth = w_mod.shape[0]
    ssm_w = ssm_d.shape[1]
    att_w = (w_in.shape[2] - ssm_w) // 3
    rpb_rows = seq + n_ctx
    assert batch == SUBLANES and batch < MOD_ROWS
    assert seq % GRID_W == 0 and att_w % LANES == 0 and ssm_w % SSM_BLOCK_IN == 0
    tm = _token_tile(rpb_rows)
    tpb = rpb_rows // tm
    lc = math.gcd(math.gcd(seq, n_ctx), 64)
    tme = 1024
    common = dict(tm=tm, tpb=tpb, seq=seq, ctx_row=batch)

    cvec = jnp.zeros((MOD_ROWS, d), F32).at[:batch].set(c.astype(F32)).at[batch].set(c_ctx.astype(F32))
    mod = _mod_table(cvec, w_mod.astype(F32), b_mod.astype(F32))
    x_all = jnp.concatenate([x, ctx], axis=1).astype(F32).reshape(batch * rpb_rows, d)


    for l in range(depth):
        mod_l = mod[l]
        u, q, k, v = _in_proj(x_all, g_mix[l].astype(F32), mod_l, w_in[l].astype(BF16), ssm_w=ssm_w, att_w=att_w,
                              **common)

        u_tm = u.reshape(batch, rpb_rows, ssm_w).transpose(1, 0, 2)
        y_tm = None
        for direction in range(2):
            lam_re, lam_im, bc, cc = _ssm_params(ssm_a_re[l, direction], ssm_a_im[l, direction],
                                                 ssm_log_step[l, direction], ssm_b_re[l, direction],
                                                 ssm_b_im[l, direction], ssm_c_re[l, direction],
                                                 ssm_c_im[l, direction])
            y_tm = _ssm_direction(u_tm, y_tm, lam_re, lam_im, bc, cc, ssm_d[l].astype(F32).reshape(1, ssm_w),
                                  lc=lc, seq=seq, n_ctx=n_ctx, reverse=direction == 1)
        y_ssm = y_tm.transpose(1, 0, 2).reshape(batch * rpb_rows, ssm_w)

        bias = _attention_bias(na_rpb[l], seq // GRID_W)
        y_att = _attention(q, k, v, bias, batch=batch, rpb_rows=rpb_rows, seq=seq, n_ctx=n_ctx)

        x_all = _mix_out(y_ssm, y_att, x_all, mod_l, glu_w[l].astype(BF16), glu_b[l].astype(F32),
                         w_out[l].astype(BF16), **common)

        if l % 2 == 0:
            x_all = _dense_ffn(x_all, g_ffn[l].astype(F32), mod_l, ffn_w1[l // 2].astype(BF16),
                               ffn_w3[l // 2].astype(BF16), ffn_w2[l // 2].astype(BF16), **common)
        else:
            x_all = _moe_ffn(x_all, g_ffn[l].astype(F32), mod_l, moe_router[l // 2], moe_w1, moe_w3, moe_w2,
                             tme=tme, layer=l // 2, **common)

    return _final_norm(x_all.reshape(batch, rpb_rows, d), g_final.astype(F32), seq=seq).astype(x.dtype)
```

```python
import functools
import math

import jax
import jax.numpy as jnp
from jax import lax
from jax.experimental import pallas as pl
from jax.experimental.pallas import tpu as pltpu

F32 = jnp.float32
BF16 = jnp.bfloat16

GRID_W = 64
SSM_GROUP = 16
SSM_STATE = 64
HEAD_DIM = 64
NA_ROWS_MAX = 8
NA_COLS = 16
N_EXPERTS = 8
NORM_EPS = 1e-6
A_RE_MAX = -1e-4
MASK_VALUE = -1e30

LANES = 128
SUBLANES = 8
V7X_VMEM_LIMIT_BYTES = 56 * 1024 * 1024

MOD_ROWS = 16
SSM_BLOCK_IN = 128
SSM_BLOCK_STATE = 512


def _params(*semantics):
    return pltpu.CompilerParams(dimension_semantics=semantics, vmem_limit_bytes=V7X_VMEM_LIMIT_BYTES)


def _token_tile(rows_per_batch):
    for parts in range(1, rows_per_batch + 1):
        if rows_per_batch % parts == 0:
            tm = rows_per_batch // parts
            if tm <= 1152 and tm % 16 == 0:
                return tm
    raise ValueError("no token tile for %d rows" % rows_per_batch)


def _chunk(total, target):
    best = None
    for c in range(LANES, min(total, target) + 1, LANES):
        if total % c == 0:
            best = c
    if best is None:
        raise ValueError("no lane-aligned chunk for %d" % total)
    return best


def _mod_kernel(c_ref, w_ref, b_ref, o_ref):
    a = jax.nn.silu(c_ref[...])
    o_ref[...] = jnp.dot(a, w_ref[...], preferred_element_type=F32) + b_ref[...]


def _mod_table(cvec, w_mod, b_mod):
    depth, d, n = w_mod.shape
    tn = _chunk(n, 1536)
    return pl.pallas_call(
        _mod_kernel,
        grid=(depth, n // tn),
        in_specs=[
            pl.BlockSpec((MOD_ROWS, d), lambda l, j: (0, 0)),
            pl.BlockSpec((None, d, tn), lambda l, j: (l, 0, j)),
            pl.BlockSpec((None, 1, tn), lambda l, j: (l, 0, j)),
        ],
        out_specs=pl.BlockSpec((None, MOD_ROWS, tn), lambda l, j: (l, 0, j)),
        out_shape=jax.ShapeDtypeStruct((depth, MOD_ROWS, n), F32),
        compiler_params=_params("arbitrary", "arbitrary"),
        name="mod_table",
    )(cvec, w_mod, b_mod.reshape(depth, 1, n))


def _mod_rows(mod_ref, slab, tile_idx, tm, tpb, seq, ctx_row, d):
    b = tile_idx // tpb
    v_b = mod_ref[pl.ds(b, 1), slab * d:(slab + 1) * d]
    v_c = mod_ref[ctx_row:ctx_row + 1, slab * d:(slab + 1) * d]
    row = (tile_idx % tpb) * tm + lax.broadcasted_iota(jnp.int32, (tm, 1), 0)
    return jnp.where(row >= seq, v_c, v_b)


def _norm_mod(x, g, mod_ref, slab, tile_idx, tpb, seq, ctx_row):
    tm, d = x.shape
    shift = _mod_rows(mod_ref, slab, tile_idx, tm, tpb, seq, ctx_row, d)
    scale = _mod_rows(mod_ref, slab + 1, tile_idx, tm, tpb, seq, ctx_row, d)
    rs = lax.rsqrt(jnp.mean(x * x, axis=-1, keepdims=True) + NORM_EPS)
    return (x * rs) * g * (1.0 + scale) + shift


def _in_kernel(x_ref, g_ref, mod_ref, w_ref, u_ref, q_ref, k_ref, v_ref, *, tpb, seq, ctx_row, ssm_w, att_w):
    i = pl.program_id(0)
    h = _norm_mod(x_ref[...], g_ref[...], mod_ref, 0, i, tpb, seq, ctx_row).astype(BF16)
    c1 = ssm_w + att_w
    c2 = c1 + att_w
    u_ref[...] = jnp.dot(h, w_ref[:, 0:ssm_w], preferred_element_type=F32)
    q_ref[...] = (jnp.dot(h, w_ref[:, ssm_w:c1], preferred_element_type=F32) * (HEAD_DIM ** -0.5)).astype(BF16)
    k_ref[...] = jnp.dot(h, w_ref[:, c1:c2], preferred_element_type=F32).astype(BF16)
    v_ref[...] = jnp.dot(h, w_ref[:, c2:c2 + att_w], preferred_element_type=F32).astype(BF16)


def _in_proj(x_all, g, mod_l, w_in, *, tm, tpb, seq, ctx_row, ssm_w, att_w):
    t, d = x_all.shape
    n = w_in.shape[1]
    row_block = lambda width: pl.BlockSpec((tm, width), lambda i: (i, 0))
    return pl.pallas_call(
        functools.partial(_in_kernel, tpb=tpb, seq=seq, ctx_row=ctx_row, ssm_w=ssm_w, att_w=att_w),
        grid=(t // tm,),
        in_specs=[
            row_block(d),
            pl.BlockSpec((1, d), lambda i: (0, 0)),
            pl.BlockSpec(mod_l.shape, lambda i: (0, 0)),
            pl.BlockSpec((d, n), lambda i: (0, 0)),
        ],
        out_specs=[row_block(ssm_w), row_block(att_w), row_block(att_w), row_block(att_w)],
        out_shape=[
            jax.ShapeDtypeStruct((t, ssm_w), F32),
            jax.ShapeDtypeStruct((t, att_w), BF16),
            jax.ShapeDtypeStruct((t, att_w), BF16),
            jax.ShapeDtypeStruct((t, att_w), BF16),
        ],
        compiler_params=_params("arbitrary"),
        name="in_proj",
    )(x_all, g.reshape(1, d), mod_l, w_in)


def _att_kernel(q_ref, k_ref, v_ref, bias_ref, o_ref, plat_s, pctx_s, den_s, *, seq, n_ctx, rows, win, group):
    lane = lax.broadcasted_iota(jnp.int32, (1, LANES), 1)
    first_head = lane < HEAD_DIM
    mask0 = first_head.astype(BF16)
    mask1 = 1 - mask0
    trans_b = (((1,), (1,)), ((), ()))
    lane_tiles = lambda arrs: [a[:, c:c + LANES] for a in arrs for c in range(0, a.shape[1], LANES)]

    def stack(q):
        return jnp.concatenate([q * mask0, q * mask1], axis=0)

    def unstack(o):
        m_rows = o.shape[0] // 2
        return jnp.where(first_head, o[:m_rows], o[m_rows:])

    def probabilities(q, parts):
        q2 = stack(q)
        scores = []
        for keys, bias in parts:
            s = lax.dot_general(q2, keys, trans_b, preferred_element_type=F32)
            scores.append(s if bias is None else s + bias)
        m = jnp.max(functools.reduce(jnp.maximum, lane_tiles(scores)), axis=-1, keepdims=True)
        probs = [jnp.exp(s - m) for s in scores]
        den = jnp.sum(functools.reduce(lambda a, b: a + b, lane_tiles(probs)), axis=-1, keepdims=True)
        return [p.astype(BF16) for p in probs], den

    def window_start(r):
        return pl.multiple_of(jnp.clip(r - win // 2, 0, rows - win) * GRID_W, GRID_W)

    def score_stage(r, slot):
        q0 = pl.multiple_of(r * GRID_W, GRID_W)
        r0 = jnp.clip(r - win // 2, 0, rows - win)
        kw = k_ref[pl.ds(window_start(r), win * GRID_W), :]
        kc = k_ref[seq:seq + n_ctx, :]
        (p_lat, p_ctx), den = probabilities(q_ref[pl.ds(q0, GRID_W), :], [(kw, bias_ref[r - r0]), (kc, None)])
        plat_s[slot] = p_lat
        pctx_s[slot] = p_ctx
        den_s[slot] = jnp.broadcast_to(den, den_s.shape[1:])

    def value_stage(r, slot):
        q0 = pl.multiple_of(r * GRID_W, GRID_W)
        vw = v_ref[pl.ds(window_start(r), win * GRID_W), :]
        vc = v_ref[seq:seq + n_ctx, :]
        o = (jnp.dot(plat_s[slot], vw, preferred_element_type=F32)
             + jnp.dot(pctx_s[slot], vc, preferred_element_type=F32)) / den_s[slot]
        o_ref[pl.ds(q0, GRID_W), :] = unstack(o).astype(o_ref.dtype)

    n_groups = rows // group
    for g in range(group):
        score_stage(g, g)

    def pipelined(it, carry):
        cur = (it % 2) * group
        nxt = group - cur
        for g in range(group):
            value_stage(it * group + g, cur + g)
        for g in range(group):
            score_stage((it + 1) * group + g, nxt + g)
        return carry

    lax.fori_loop(0, n_groups - 1, pipelined, 0)
    last = ((n_groups - 1) % 2) * group
    for g in range(group):
        value_stage((n_groups - 1) * group + g, last + g)

    kc = k_ref[seq:seq + n_ctx, :]
    vc = v_ref[seq:seq + n_ctx, :]
    (p_ctx,), den = probabilities(q_ref[seq:seq + n_ctx, :], [(kc, None)])
    o = jnp.dot(p_ctx, vc, preferred_element_type=F32) / den
    o_ref[seq:seq + n_ctx, :] = unstack(o).astype(o_ref.dtype)


def _attention(q, k, v, bias, *, batch, rpb_rows, seq, n_ctx):
    t, att_w = q.shape
    rows = seq // GRID_W
    win = min(NA_ROWS_MAX, rows)
    n_pairs = att_w // LANES
    group = 8
    assert rows % group == 0
    blk = pl.BlockSpec((rpb_rows, LANES), lambda b, p: (b, p))
    return pl.pallas_call(
        functools.partial(_att_kernel, seq=seq, n_ctx=n_ctx, rows=rows, win=win, group=group),
        grid=(batch, n_pairs),
        in_specs=[blk, blk, blk,
                  pl.BlockSpec((None, win, 2 * GRID_W, win * GRID_W), lambda b, p: (p, 0, 0, 0))],
        out_specs=blk,
        out_shape=jax.ShapeDtypeStruct((t, att_w), BF16),
        scratch_shapes=[
            pltpu.VMEM((2 * group, 2 * GRID_W, win * GRID_W), BF16),
            pltpu.VMEM((2 * group, 2 * GRID_W, n_ctx), BF16),
            pltpu.VMEM((2 * group, 2 * GRID_W, LANES), F32),
        ],
        compiler_params=_params("arbitrary", "arbitrary"),
        name="attention",
    )(q, k, v, bias)


def _attention_bias(rpb, rows):
    n_heads = rpb.shape[0]
    win = min(NA_ROWS_MAX, rows)
    col = jnp.arange(GRID_W)
    col_start = jnp.clip(col - NA_COLS // 2, 0, GRID_W - NA_COLS)
    kcol = jnp.arange(GRID_W)
    valid = (kcol[None, :] >= col_start[:, None]) & (kcol[None, :] < col_start[:, None] + NA_COLS)
    col_rel = kcol[None, :] - col[:, None] + NA_COLS - 1
    var = jnp.arange(win)
    rr = jnp.arange(win)
    row_rel = rr[None, :] - var[:, None] + NA_ROWS_MAX - 1
    by_row = rpb.astype(F32)[:, row_rel]
    pick = (col_rel[:, :, None] == jnp.arange(2 * NA_COLS - 1)[None, None, :]).astype(F32)
    by_row = by_row.reshape(n_heads // 2, 2, win, win, 2 * NA_COLS - 1)
    tab = jnp.einsum("pavrj,ckj->pvacrk", by_row, pick, precision=lax.Precision.HIGHEST)
    tab = jnp.where(valid[None, None, None, :, None, :], tab, MASK_VALUE)
    return tab.reshape(n_heads // 2, win, 2 * GRID_W, win * GRID_W)


def _ssm_kernel(*refs, lc, reverse, first):
    if first:
        u_ref, lre_ref, lim_ref, bc_ref, cc_ref, dsk_ref, y_ref, hre_s, him_s, xre_s, xim_s = refs
        yprev_ref = None
    else:
        u_ref, yprev_ref, lre_ref, lim_ref, bc_ref, cc_ref, y_ref, hre_s, him_s, xre_s, xim_s = refs
    nb = u_ref.shape[1]
    width = u_ref.shape[2]
    n_state = xre_s.shape[2]
    n_blocks = width // SSM_BLOCK_IN

    @pl.when(pl.program_id(0) == 0)
    def _():
        hre_s[...] = jnp.zeros_like(hre_s)
        him_s[...] = jnp.zeros_like(him_s)

    u = u_ref[...].reshape(lc * nb, width)
    ub = u.astype(BF16)
    for jb in range(n_blocks):
        xb = jnp.dot(ub[:, jb * SSM_BLOCK_IN:(jb + 1) * SSM_BLOCK_IN], bc_ref[jb], preferred_element_type=F32)
        s0 = jb * SSM_BLOCK_STATE
        xre_s[:, :, s0:s0 + SSM_BLOCK_STATE] = xb[:, :SSM_BLOCK_STATE].reshape(lc, nb, SSM_BLOCK_STATE)
        xim_s[:, :, s0:s0 + SSM_BLOCK_STATE] = xb[:, SSM_BLOCK_STATE:].reshape(lc, nb, SSM_BLOCK_STATE)

    cw = 512
    for cb in range(n_state // cw):
        c0 = cb * cw
        lr = jnp.broadcast_to(lre_ref[:, c0:c0 + cw], (nb, cw))
        li = jnp.broadcast_to(lim_ref[:, c0:c0 + cw], (nb, cw))

        def step(t, carry, c0=c0, lr=lr, li=li):
            hr, hi = carry
            tt = (lc - 1 - t) if reverse else t
            nr = lr * hr - li * hi + xre_s[tt, :, c0:c0 + cw]
            ni = lr * hi + li * hr + xim_s[tt, :, c0:c0 + cw]
            xre_s[tt, :, c0:c0 + cw] = nr
            xim_s[tt, :, c0:c0 + cw] = ni
            return nr, ni

        hr, hi = lax.fori_loop(0, lc, step, (hre_s[:, c0:c0 + cw], him_s[:, c0:c0 + cw]), unroll=2)
        hre_s[:, c0:c0 + cw] = hr
        him_s[:, c0:c0 + cw] = hi

    if first:
        base = u * dsk_ref[...]
    else:
        base = yprev_ref[...].reshape(lc * nb, width)
    pair_state = 2 * SSM_BLOCK_STATE
    pair_out = 2 * SSM_BLOCK_IN
    for jp in range(n_blocks // 2):
        s0 = jp * pair_state
        h_re = xre_s[:, :, s0:s0 + pair_state].reshape(lc * nb, pair_state).astype(BF16)
        h_im = xim_s[:, :, s0:s0 + pair_state].reshape(lc * nb, pair_state).astype(BF16)
        yj = jnp.dot(jnp.concatenate([h_re, h_im], axis=1), cc_ref[jp], preferred_element_type=F32)
        o0 = jp * pair_out
        y_ref[:, :, o0:o0 + pair_out] = (base[:, o0:o0 + pair_out] + yj).reshape(lc, nb, pair_out)


def _ssm_direction(u_tm, y_prev, lam_re, lam_im, bc, cc, d_skip, *, lc, seq, n_ctx, reverse):
    total, nb, width = u_tm.shape
    n_state = lam_re.shape[1]
    n_ch = total // lc
    n_lat_ch = seq // lc
    n_ctx_ch = n_ctx // lc
    first = y_prev is None
    if reverse:
        chunk = lambda i: (n_ch - 1 - i, 0, 0)
    else:
        chunk = lambda i: (jnp.where(i < n_ctx_ch, n_lat_ch + i, i - n_ctx_ch), 0, 0)
    seq_block = pl.BlockSpec((lc, nb, width), chunk)
    whole = lambda a: pl.BlockSpec(a.shape, lambda i: (0,) * a.ndim)
    args = [u_tm] + ([] if first else [y_prev]) + [lam_re, lam_im, bc, cc] + ([d_skip] if first else [])
    in_specs = [seq_block] + ([] if first else [seq_block]) + [whole(lam_re), whole(lam_im), whole(bc), whole(cc)]
    if first:
        in_specs.append(whole(d_skip))
    return pl.pallas_call(
        functools.partial(_ssm_kernel, lc=lc, reverse=reverse, first=first),
        grid=(n_ch,),
        in_specs=in_specs,
        out_specs=seq_block,
        out_shape=jax.ShapeDtypeStruct((total, nb, width), F32),
        scratch_shapes=[
            pltpu.VMEM((nb, n_state), F32),
            pltpu.VMEM((nb, n_state), F32),
            pltpu.VMEM((lc, nb, n_state), F32),
            pltpu.VMEM((lc, nb, n_state), F32),
        ],
        compiler_params=_params("arbitrary"),
        name="ssm_bwd" if reverse else "ssm_fwd",
    )(*args)


def _ssm_params(a_re, a_im, log_step, b_re, b_im, c_re, c_im):
    g, p, h = b_re.shape
    a_re = jnp.minimum(a_re.astype(F32), A_RE_MAX)
    a_im = a_im.astype(F32)
    dt = jnp.exp(log_step.astype(F32))[:, None]
    mag = jnp.exp(a_re * dt)
    lam_re = mag * jnp.cos(a_im * dt)
    lam_im = mag * jnp.sin(a_im * dt)
    den = a_re * a_re + a_im * a_im
    z_re = ((lam_re - 1) * a_re + lam_im * a_im) / den
    z_im = (lam_im * a_re - (lam_re - 1) * a_im) / den
    b_re = b_re.astype(F32)
    b_im = b_im.astype(F32)
    bb_re = z_re[..., None] * b_re - z_im[..., None] * b_im
    bb_im = z_re[..., None] * b_im + z_im[..., None] * b_re
    gpb = SSM_BLOCK_IN // h
    nblk = g // gpb
    eye = jnp.eye(gpb, dtype=F32)

    def pack_in(bb):
        bb = bb.reshape(nblk, gpb, p, h)
        return jnp.einsum("ngph,gk->nghkp", bb, eye).reshape(nblk, gpb * h, gpb * p)

    def pack_out(c):
        c = c.reshape(nblk, gpb, h, p)
        return jnp.einsum("nghp,gk->ngpkh", c, eye).reshape(nblk, gpb * p, gpb * h)

    def pair_out(c):
        c = c.reshape(nblk // 2, 2, gpb * p, gpb * h)
        return jnp.einsum("nasb,ak->naskb", c, jnp.eye(2, dtype=F32)).reshape(nblk // 2, 2 * gpb * p, 2 * gpb * h)

    bc = jnp.concatenate([pack_in(bb_re), pack_in(bb_im)], axis=2).astype(BF16)
    cc = jnp.concatenate([pair_out(pack_out(c_re.astype(F32))), pair_out(-pack_out(c_im.astype(F32)))],
                         axis=1).astype(BF16)
    return lam_re.reshape(1, g * p), lam_im.reshape(1, g * p), bc, cc


def _mix_kernel(y_ref, att_ref, x_ref, mod_ref, gw_ref, gb_ref, wo_ref, o_ref, *, tpb, seq, ctx_row):
    i = pl.program_id(0)
    tm, d = x_ref.shape
    ssm_w = y_ref.shape[1]
    g = jax.nn.gelu(y_ref[...])
    z = jnp.dot(g.astype(BF16), gw_ref[...], preferred_element_type=F32) + gb_ref[...]
    s = g * jax.nn.sigmoid(z)
    o = (jnp.dot(s.astype(BF16), wo_ref[0:ssm_w, :], preferred_element_type=F32)
         + jnp.dot(att_ref[...], wo_ref[ssm_w:, :], preferred_element_type=F32))
    gate = _mod_rows(mod_ref, 2, i, tm, tpb, seq, ctx_row, d)
    o_ref[...] = x_ref[...] + gate * o


def _mix_out(y_ssm, y_att, x_all, mod_l, glu_w, glu_b, w_out, *, tm, tpb, seq, ctx_row):
    t, d = x_all.shape
    ssm_w = y_ssm.shape[1]
    att_w = y_att.shape[1]
    row_block = lambda width: pl.BlockSpec((tm, width), lambda i: (i, 0))
    whole = lambda a: pl.BlockSpec(a.shape, lambda i: (0, 0))
    gb = glu_b.reshape(1, ssm_w)
    return pl.pallas_call(
        functools.partial(_mix_kernel, tpb=tpb, seq=seq, ctx_row=ctx_row),
        grid=(t // tm,),
        in_specs=[row_block(ssm_w), row_block(att_w), row_block(d), whole(mod_l), whole(glu_w), whole(gb),
                  whole(w_out)],
        out_specs=row_block(d),
        out_shape=jax.ShapeDtypeStruct((t, d), F32),
        compiler_params=_params("arbitrary"),
        name="mix_out",
    )(y_ssm, y_att, x_all, mod_l, glu_w, gb, w_out)


def _ffn_kernel(x_ref, g_ref, mod_ref, w1_ref, w3_ref, w2_ref, o_ref, t_s, acc_s, *, tpb, seq, ctx_row):
    i = pl.program_id(0)
    tm, d = x_ref.shape
    nf = w1_ref.shape[0]
    t_s[...] = _norm_mod(x_ref[...], g_ref[...], mod_ref, 3, i, tpb, seq, ctx_row).astype(BF16)
    acc_s[...] = jnp.zeros_like(acc_s)

    def chunk(c, carry):
        t = t_s[...]
        h1 = jnp.dot(t, w1_ref[c], preferred_element_type=F32)
        h3 = jnp.dot(t, w3_ref[c], preferred_element_type=F32)
        a = (jax.nn.silu(h1) * h3).astype(BF16)
        acc_s[...] += jnp.dot(a, w2_ref[c], preferred_element_type=F32)
        return carry

    lax.fori_loop(0, nf, chunk, 0, unroll=2)
    gate = _mod_rows(mod_ref, 5, i, tm, tpb, seq, ctx_row, d)
    o_ref[...] = x_ref[...] + gate * acc_s[...]


def _dense_ffn(x_all, g, mod_l, w1, w3, w2, *, tm, tpb, seq, ctx_row):
    t, d = x_all.shape
    dff = w1.shape[1]
    tf = _chunk(dff, 256)
    nf = dff // tf
    w1c = w1.reshape(d, nf, tf).transpose(1, 0, 2)
    w3c = w3.reshape(d, nf, tf).transpose(1, 0, 2)
    w2c = w2.reshape(nf, tf, d)
    resident = lambda a: pl.BlockSpec(a.shape, lambda i: (0,) * a.ndim, pipeline_mode=pl.Buffered(1))
    return pl.pallas_call(
        functools.partial(_ffn_kernel, tpb=tpb, seq=seq, ctx_row=ctx_row),
        grid=(t // tm,),
        in_specs=[
            pl.BlockSpec((tm, d), lambda i: (i, 0)),
            pl.BlockSpec((1, d), lambda i: (0, 0)),
            pl.BlockSpec(mod_l.shape, lambda i: (0, 0)),
            resident(w1c), resident(w3c), resident(w2c),
        ],
        out_specs=pl.BlockSpec((tm, d), lambda i: (i, 0)),
        out_shape=jax.ShapeDtypeStruct((t, d), F32),
        scratch_shapes=[pltpu.VMEM((tm, d), BF16), pltpu.VMEM((tm, d), F32)],
        compiler_params=_params("arbitrary"),
        name="dense_ffn",
    )(x_all, g.reshape(1, d), mod_l, w1c, w3c, w2c)


def _route_kernel(x_ref, g_ref, mod_ref, r_ref, t_ref, route_ref, *, tpb, seq, ctx_row):
    i = pl.program_id(0)
    t = _norm_mod(x_ref[...], g_ref[...], mod_ref, 3, i, tpb, seq, ctx_row)
    t_ref[...] = t
    th = t.astype(BF16)
    tl = (t - th.astype(F32)).astype(BF16)
    r = r_ref[...]
    rh = r.astype(BF16)
    rl = (r - rh.astype(F32)).astype(BF16)
    dot = lambda a, b: jnp.dot(a, b, preferred_element_type=F32)
    logits = dot(th, rh) + (dot(th, rl) + dot(tl, rh)) + dot(tl, rl)
    tm = logits.shape[0]
    lane = lax.broadcasted_iota(jnp.int32, (tm, LANES), 1)
    lane_f = lane.astype(F32)
    neg_inf = jnp.float32(-jnp.inf)
    lg = jnp.where(lane < N_EXPERTS, logits, neg_inf)
    m1 = jnp.max(lg, axis=-1, keepdims=True)
    i1 = jnp.min(jnp.where(lg == m1, lane_f, float(LANES)), axis=-1, keepdims=True)
    lg2 = jnp.where(lane_f == i1, neg_inf, lg)
    m2 = jnp.max(lg2, axis=-1, keepdims=True)
    i2 = jnp.min(jnp.where(lg2 == m2, lane_f, float(LANES)), axis=-1, keepdims=True)
    e = jnp.exp(m2 - m1)
    g1 = 1.0 / (1.0 + e)
    g2 = e / (1.0 + e)
    route_ref[...] = jnp.where(lane == 0, i1, jnp.where(lane == 1, i2, jnp.where(lane == 2, g1, jnp.where(
        lane == 3, g2, 0.0))))


def _route(x_all, g, mod_l, router, *, tm, tpb, seq, ctx_row):
    t, d = x_all.shape
    r_pad = jnp.zeros((d, LANES), F32).at[:, :router.shape[1]].set(router.astype(F32))
    return pl.pallas_call(
        functools.partial(_route_kernel, tpb=tpb, seq=seq, ctx_row=ctx_row),
        grid=(t // tm,),
        in_specs=[
            pl.BlockSpec((tm, d), lambda i: (i, 0)),
            pl.BlockSpec((1, d), lambda i: (0, 0)),
            pl.BlockSpec(mod_l.shape, lambda i: (0, 0)),
            pl.BlockSpec((d, LANES), lambda i: (0, 0)),
        ],
        out_specs=[pl.BlockSpec((tm, d), lambda i: (i, 0)), pl.BlockSpec((tm, LANES), lambda i: (i, 0))],
        out_shape=[jax.ShapeDtypeStruct((t, d), F32), jax.ShapeDtypeStruct((t, LANES), F32)],
        compiler_params=_params("arbitrary"),
        name="moe_route",
    )(x_all, g.reshape(1, d), mod_l, r_pad)


def _row_copy(src_hbm, dst_vmem, sem, src_row, dst_row):
    return pltpu.make_async_copy(src_hbm.at[pl.ds(src_row, 1), :], dst_vmem.at[pl.ds(dst_row, 1), :], sem)


ISSUE_UNROLL = 8


ZERO_BLOCK_ROWS = 256


def _dispatch_kernel(fill_lo_ref, fill_hi_ref, pos_ref, t_ref, xs_hbm, zero_s, sems):
    i = pl.program_id(0)
    tm = t_ref.shape[0]
    per_trip = ISSUE_UNROLL // 2

    def issue(g, carry):
        for k in range(per_trip):
            r = g * per_trip + k
            for choice in range(2):
                pltpu.make_async_copy(t_ref.at[pl.ds(r, 1), :], xs_hbm.at[pl.ds(pos_ref[0, 0, 2 * r + choice], 1), :],
                                      sems.at[choice]).start()
        return carry

    lax.fori_loop(0, tm // per_trip, issue, 0)

    @pl.when(i == 0)
    def _():
        zero_s[...] = jnp.zeros_like(zero_s)
        zero_row = lambda row: pltpu.make_async_copy(zero_s.at[pl.ds(0, 1), :], xs_hbm.at[pl.ds(row, 1), :],
                                                     sems.at[2])
        zero_block = lambda blk: pltpu.make_async_copy(
            zero_s, xs_hbm.at[pl.ds(pl.multiple_of(blk * ZERO_BLOCK_ROWS, ZERO_BLOCK_ROWS), ZERO_BLOCK_ROWS), :],
            sems.at[2])

        def start_then_wait(copy, lo, hi):
            def start(k, carry):
                copy(k).start()
                return carry

            def wait(k, carry):
                copy(k).wait()
                return carry

            lax.fori_loop(lo, hi, start, 0)
            lax.fori_loop(lo, hi, wait, 0)

        for e in range(N_EXPERTS):
            start_then_wait(zero_row, fill_lo_ref[e], fill_hi_ref[e])
        start_then_wait(zero_block, fill_lo_ref[N_EXPERTS] // ZERO_BLOCK_ROWS,
                        fill_hi_ref[N_EXPERTS] // ZERO_BLOCK_ROWS)

    for choice in range(2):
        pltpu.make_async_copy(t_ref, xs_hbm.at[pl.ds(0, tm), :], sems.at[choice]).wait()


def _dispatch_rows(tokens, pos, fill_lo, fill_hi, *, tm, n_rows):
    t, d = tokens.shape
    steps = t // tm
    assert tm % (ISSUE_UNROLL // 2) == 0 and n_rows % ZERO_BLOCK_ROWS == 0
    grid_spec = pltpu.PrefetchScalarGridSpec(
        num_scalar_prefetch=2,
        grid=(steps,),
        in_specs=[
            pl.BlockSpec((1, 1, 2 * tm), lambda i, lo, hi: (i, 0, 0), memory_space=pltpu.SMEM),
            pl.BlockSpec((tm, d), lambda i, lo, hi: (i, 0)),
        ],
        out_specs=pl.BlockSpec(memory_space=pl.ANY),
        scratch_shapes=[pltpu.VMEM((ZERO_BLOCK_ROWS, d), tokens.dtype), pltpu.SemaphoreType.DMA((3,))],
    )
    return pl.pallas_call(
        _dispatch_kernel,
        grid_spec=grid_spec,
        out_shape=jax.ShapeDtypeStruct((n_rows, d), tokens.dtype),
        compiler_params=_params("arbitrary"),
        name="moe_dispatch",
    )(fill_lo, fill_hi, pos.reshape(steps, 1, 2 * tm), tokens)


def _gmm_kernel(te_ref, tv_ref, ts_ref, xs_ref, w1_ref, w3_ref, w2_ref, o_ref, xb_s, acc_s, *, nf, n_sub):
    j = pl.program_id(0)
    f = pl.program_id(1)
    valid = tv_ref[j] > 0

    @pl.when(valid)
    def _():
        @pl.when(f == 0)
        def _():
            xb_s[...] = xs_ref[...].astype(BF16)
            acc_s[...] = jnp.zeros_like(acc_s)

        x = xb_s[...]
        tf = w1_ref.shape[1]
        sub = tf // n_sub
        part = None
        for c0 in range(0, tf, sub):
            h1 = jnp.dot(x, w1_ref[:, c0:c0 + sub].astype(BF16), preferred_element_type=F32)
            h3 = jnp.dot(x, w3_ref[:, c0:c0 + sub].astype(BF16), preferred_element_type=F32)
            a = (jax.nn.silu(h1) * h3).astype(BF16)
            y = jnp.dot(a, w2_ref[c0:c0 + sub, :].astype(BF16), preferred_element_type=F32)
            part = y if part is None else part + y
        acc_s[...] += part

        @pl.when(f == nf - 1)
        def _():
            o_ref[...] = acc_s[...]

    @pl.when(jnp.logical_and(jnp.logical_not(valid), f == nf - 1))
    def _():
        o_ref[...] = jnp.zeros_like(o_ref)


def _grouped_swiglu(xs, tile_expert, tile_valid, tile_src, w1, w3, w2, *, tme, layer):
    n, d = xs.shape
    dff = w1.shape[3]
    tf = _chunk(dff, 512)
    nf = dff // tf
    grid_spec = pltpu.PrefetchScalarGridSpec(
        num_scalar_prefetch=3,
        grid=(n // tme, nf),
        in_specs=[
            pl.BlockSpec((tme, d), lambda j, f, te, tv, ts: (ts[j], 0)),
            pl.BlockSpec((None, None, d, tf), lambda j, f, te, tv, ts: (layer, te[j], 0, f)),
            pl.BlockSpec((None, None, d, tf), lambda j, f, te, tv, ts: (layer, te[j], 0, f)),
            pl.BlockSpec((None, None, tf, d), lambda j, f, te, tv, ts: (layer, te[j], f, 0)),
        ],
        out_specs=pl.BlockSpec((tme, d), lambda j, f, te, tv, ts: (j, 0)),
        scratch_shapes=[pltpu.VMEM((tme, d), BF16), pltpu.VMEM((tme, d), F32)],
    )
    return pl.pallas_call(
        functools.partial(_gmm_kernel, nf=nf, n_sub=2),
        grid_spec=grid_spec,
        out_shape=jax.ShapeDtypeStruct((n, d), F32),
        compiler_params=_params("arbitrary", "arbitrary"),
        name="moe_experts",
    )(tile_expert, tile_valid, tile_src, xs, w1, w3, w2)


def _combine_kernel(pos_ref, pos_next_ref, ys_hbm, x_ref, route_ref, mod_ref, o_ref, buf, sems, *, tpb, seq,
                    ctx_row):
    i = pl.program_id(0)
    tm, d = x_ref.shape

    def issue_all(pos, slot):
        def issue(g, carry):
            for k in range(ISSUE_UNROLL // 2):
                r = g * (ISSUE_UNROLL // 2) + k
                _row_copy(ys_hbm, buf.at[slot, 0], sems.at[slot, 0], pos[0, 0, 2 * r], r).start()
                _row_copy(ys_hbm, buf.at[slot, 1], sems.at[slot, 1], pos[0, 0, 2 * r + 1], r).start()
            return carry

        lax.fori_loop(0, tm // (ISSUE_UNROLL // 2), issue, 0)

    @pl.when(i == 0)
    def _():
        issue_all(pos_ref, 0)

    @pl.when(i + 1 < pl.num_programs(0))
    def _():
        issue_all(pos_next_ref, (i + 1) % 2)

    slot = i % 2
    pltpu.make_async_copy(ys_hbm.at[pl.ds(0, tm), :], buf.at[slot, 0], sems.at[slot, 0]).wait()
    pltpu.make_async_copy(ys_hbm.at[pl.ds(0, tm), :], buf.at[slot, 1], sems.at[slot, 1]).wait()
    g1 = route_ref[:, 2:3]
    g2 = route_ref[:, 3:4]
    gate = _mod_rows(mod_ref, 5, i, tm, tpb, seq, ctx_row, d)
    o_ref[...] = x_ref[...] + gate * (g1 * buf[slot, 0] + g2 * buf[slot, 1])


def _combine(ys, pos, x_all, route, mod_l, *, tm, tpb, seq, ctx_row):
    t, d = x_all.shape
    steps = t // tm
    assert tm % (ISSUE_UNROLL // 2) == 0
    pos3 = pos.reshape(steps, 1, 2 * tm)
    return pl.pallas_call(
        functools.partial(_combine_kernel, tpb=tpb, seq=seq, ctx_row=ctx_row),
        grid=(steps,),
        in_specs=[
            pl.BlockSpec((1, 1, 2 * tm), lambda i: (i, 0, 0), memory_space=pltpu.SMEM),
            pl.BlockSpec((1, 1, 2 * tm), lambda i: (jnp.minimum(i + 1, steps - 1), 0, 0), memory_space=pltpu.SMEM),
            pl.BlockSpec(memory_space=pl.ANY),
            pl.BlockSpec((tm, d), lambda i: (i, 0)),
            pl.BlockSpec((tm, LANES), lambda i: (i, 0)),
            pl.BlockSpec(mod_l.shape, lambda i: (0, 0)),
        ],
        out_specs=pl.BlockSpec((tm, d), lambda i: (i, 0)),
        out_shape=jax.ShapeDtypeStruct((t, d), F32),
        scratch_shapes=[pltpu.VMEM((2, 2, tm, d), F32), pltpu.SemaphoreType.DMA((2, 2))],
        compiler_params=_params("arbitrary"),
        name="moe_combine",
    )(pos3, pos3, ys, x_all, route, mod_l)


def _moe_ffn(x_all, g, mod_l, router, w1, w3, w2, *, tm, tpb, seq, ctx_row, tme, layer):
    t = x_all.shape[0]
    tokens, route = _route(x_all, g, mod_l, router, tm=tm, tpb=tpb, seq=seq, ctx_row=ctx_row)
    e_flat = route[:, 0:2].astype(jnp.int32).reshape(-1)
    onehot = (e_flat[:, None] == jnp.arange(N_EXPERTS, dtype=jnp.int32)[None, :]).astype(jnp.int32)
    csum = jnp.cumsum(onehot, axis=0)
    rank = jnp.take_along_axis(csum, e_flat[:, None], axis=1)[:, 0] - 1
    counts = csum[-1]
    padded = ((counts + tme - 1) // tme) * tme
    seg_end = jnp.cumsum(padded)
    seg_start = seg_end - padded
    pos = seg_start[e_flat] + rank
    n_tiles = -(-(2 * t + N_EXPERTS * (tme - 1)) // tme)
    tile_start = jnp.arange(n_tiles, dtype=jnp.int32) * tme
    tile_expert = jnp.minimum(jnp.sum((tile_start[:, None] >= seg_end[None, :]).astype(jnp.int32), axis=1),
                              N_EXPERTS - 1)
    tile_valid = (tile_start < seg_end[-1]).astype(jnp.int32)
    tile_src = jnp.minimum(jnp.arange(n_tiles, dtype=jnp.int32), seg_end[-1] // tme - 1)

    n_rows = n_tiles * tme
    fill_lo = jnp.concatenate([seg_start + counts, seg_end[-1:]]).astype(jnp.int32)
    fill_hi = jnp.concatenate([seg_end, jnp.full((1,), n_rows, jnp.int32)]).astype(jnp.int32)
    xs = _dispatch_rows(tokens, pos, fill_lo, fill_hi, tm=tm, n_rows=n_rows)
    ys = _grouped_swiglu(xs, tile_expert, tile_valid, tile_src, w1, w3, w2, tme=tme, layer=layer)
    return _combine(ys, pos, x_all, route, mod_l, tm=tm, tpb=tpb, seq=seq, ctx_row=ctx_row)


def _final_kernel(x_ref, g_ref, o_ref):
    x = x_ref[...]
    o_ref[...] = x * lax.rsqrt(jnp.mean(x * x, axis=-1, keepdims=True) + NORM_EPS) * g_ref[...]


def _final_norm(x3, g, *, seq):
    batch, _, d = x3.shape
    tr = _chunk(seq, 1024)
    return pl.pallas_call(
        _final_kernel,
        grid=(batch, seq // tr),
        in_specs=[pl.BlockSpec((None, tr, d), lambda b, j: (b, j, 0)), pl.BlockSpec((1, d), lambda b, j: (0, 0))],
        out_specs=pl.BlockSpec((None, tr, d), lambda b, j: (b, j, 0)),
        out_shape=jax.ShapeDtypeStruct((batch, seq, d), F32),
        compiler_params=_params("arbitrary", "arbitrary"),
        name="final_norm",
    )(x3, g.reshape(1, d))


def kernel(x, c, ctx, c_ctx, w_mod, b_mod, g_mix, g_ffn, w_in, w_out, ssm_a_re, ssm_a_im, ssm_log_step, ssm_b_re, ssm_b_im, ssm_c_re, ssm_c_im, ssm_d, glu_w, glu_b, na_rpb, ffn_w1, ffn_w3, ffn_w2, moe_router, moe_w1, moe_w3, moe_w2, g_final):
    batch, seq, d = x.shape
    n_ctx = ctx.shape[1]
    depth = w_mod.shape[0]
    ssm_w = ssm_d.shape[1]
    att_w = (w_in.shape[2] - ssm_w) // 3
    rpb_rows = seq + n_ctx
    assert batch == SUBLANES and batch < MOD_ROWS
    assert seq % GRID_W == 0 and att_w % LANES == 0 and ssm_w % SSM_BLOCK_IN == 0
    tm = _token_tile(rpb_rows)
    tpb = rpb_rows // tm
    lc = math.gcd(math.gcd(seq, n_ctx), 128)
    tme = 1024
    common = dict(tm=tm, tpb=tpb, seq=seq, ctx_row=batch)

    cvec = jnp.zeros((MOD_ROWS, d), F32).at[:batch].set(c.astype(F32)).at[batch].set(c_ctx.astype(F32))
    mod = _mod_table(cvec, w_mod.astype(F32), b_mod.astype(F32))
    x_all = jnp.concatenate([x, ctx], axis=1).astype(F32).reshape(batch * rpb_rows, d)


    for l in range(depth):
        mod_l = mod[l]
        u, q, k, v = _in_proj(x_all, g_mix[l].astype(F32), mod_l, w_in[l].astype(BF16), ssm_w=ssm_w, att_w=att_w,
                              **common)

        u_tm = u.reshape(batch, rpb_rows, ssm_w).transpose(1, 0, 2)
        y_tm = None
        for direction in range(2):
            lam_re, lam_im, bc, cc = _ssm_params(ssm_a_re[l, direction], ssm_a_im[l, direction],
                                                 ssm_log_step[l, direction], ssm_b_re[l, direction],
                                                 ssm_b_im[l, direction], ssm_c_re[l, direction],
                                                 ssm_c_im[l, direction])
            y_tm = _ssm_direction(u_tm, y_tm, lam_re, lam_im, bc, cc, ssm_d[l].astype(F32).reshape(1, ssm_w),
                                  lc=lc, seq=seq, n_ctx=n_ctx, reverse=direction == 1)
        y_ssm = y_tm.transpose(1, 0, 2).reshape(batch * rpb_rows, ssm_w)

        bias = _attention_bias(na_rpb[l], seq // GRID_W)
        y_att = _attention(q, k, v, bias, batch=batch, rpb_rows=rpb_rows, seq=seq, n_ctx=n_ctx)

        x_all = _mix_out(y_ssm, y_att, x_all, mod_l, glu_w[l].astype(BF16), glu_b[l].astype(F32),
                         w_out[l].astype(BF16), **common)

        if l % 2 == 0:
            x_all = _dense_ffn(x_all, g_ffn[l].astype(F32), mod_l, ffn_w1[l // 2].astype(BF16),
                               ffn_w3[l // 2].astype(BF16), ffn_w2[l // 2].astype(BF16), **common)
        else:
            x_all = _moe_ffn(x_all, g_ffn[l].astype(F32), mod_l, moe_router[l // 2], moe_w1, moe_w3, moe_w2,
                             tme=tme, layer=l // 2, **common)

    return _final_norm(x_all.reshape(batch, rpb_rows, d), g_final.astype(F32), seq=seq).astype(x.dtype)
```

```python
import functools
import math

import jax
import jax.numpy as jnp
from jax import lax
from jax.experimental import pallas as pl
from jax.experimental.pallas import tpu as pltpu

F32 = jnp.float32
BF16 = jnp.bfloat16

GRID_W = 64
SSM_GROUP = 16
SSM_STATE = 64
HEAD_DIM = 64
NA_ROWS_MAX = 8
NA_COLS = 16
N_EXPERTS = 8
NORM_EPS = 1e-6
A_RE_MAX = -1e-4
MASK_VALUE = -1e30

LANES = 128
SUBLANES = 8
V7X_VMEM_LIMIT_BYTES = 56 * 1024 * 1024

MOD_ROWS = 16


def _params(*semantics):
    return pltpu.CompilerParams(dimension_semantics=semantics, vmem_limit_bytes=V7X_VMEM_LIMIT_BYTES)


def _token_tile(rows_per_batch):
    for parts in range(1, rows_per_batch + 1):
        if rows_per_batch % parts == 0:
            tm = rows_per_batch // parts
            if tm <= 1152 and tm % 16 == 0:
                return tm
    raise ValueError("no token tile for %d rows" % rows_per_batch)


def _chunk(total, target):
    best = None
    for c in range(LANES, min(total, target) + 1, LANES):
        if total % c == 0:
            best = c
    if best is None:
        raise ValueError("no lane-aligned chunk for %d" % total)
    return best


def _mod_kernel(c_ref, w_ref, b_ref, o_ref):
    a = jax.nn.silu(c_ref[...])
    o_ref[...] = jnp.dot(a, w_ref[...], preferred_element_type=F32) + b_ref[...]


def _mod_table(cvec, w_mod, b_mod):
    depth, d, n = w_mod.shape
    tn = _chunk(n, 1536)
    return pl.pallas_call(
        _mod_kernel,
        grid=(depth, n // tn),
        in_specs=[
            pl.BlockSpec((MOD_ROWS, d), lambda l, j: (0, 0)),
            pl.BlockSpec((None, d, tn), lambda l, j: (l, 0, j)),
            pl.BlockSpec((None, 1, tn), lambda l, j: (l, 0, j)),
        ],
        out_specs=pl.BlockSpec((None, MOD_ROWS, tn), lambda l, j: (l, 0, j)),
        out_shape=jax.ShapeDtypeStruct((depth, MOD_ROWS, n), F32),
        compiler_params=_params("arbitrary", "arbitrary"),
        name="mod_table",
    )(cvec, w_mod, b_mod.reshape(depth, 1, n))


def _mod_rows(mod_ref, slab, tile_idx, tm, tpb, seq, ctx_row, d):
    b = tile_idx // tpb
    v_b = mod_ref[pl.ds(b, 1), slab * d:(slab + 1) * d]
    v_c = mod_ref[ctx_row:ctx_row + 1, slab * d:(slab + 1) * d]
    row = (tile_idx % tpb) * tm + lax.broadcasted_iota(jnp.int32, (tm, 1), 0)
    return jnp.where(row >= seq, v_c, v_b)


def _norm_mod(x, g, mod_ref, slab, tile_idx, tpb, seq, ctx_row):
    tm, d = x.shape
    shift = _mod_rows(mod_ref, slab, tile_idx, tm, tpb, seq, ctx_row, d)
    scale = _mod_rows(mod_ref, slab + 1, tile_idx, tm, tpb, seq, ctx_row, d)
    rs = lax.rsqrt(jnp.mean(x * x, axis=-1, keepdims=True) + NORM_EPS)
    return (x * rs) * g * (1.0 + scale) + shift


def _in_kernel(x_ref, g_ref, mod_ref, w_ref, u_ref, q_ref, k_ref, v_ref, *, tpb, seq, ctx_row, ssm_w, att_w):
    i = pl.program_id(0)
    h = _norm_mod(x_ref[...], g_ref[...], mod_ref, 0, i, tpb, seq, ctx_row).astype(BF16)
    c1 = ssm_w + att_w
    c2 = c1 + att_w
    u_ref[...] = jnp.dot(h, w_ref[:, 0:ssm_w], preferred_element_type=F32)
    q_ref[...] = (jnp.dot(h, w_ref[:, ssm_w:c1], preferred_element_type=F32) * (HEAD_DIM ** -0.5)).astype(BF16)
    k_ref[...] = jnp.dot(h, w_ref[:, c1:c2], preferred_element_type=F32).astype(BF16)
    v_ref[...] = jnp.dot(h, w_ref[:, c2:c2 + att_w], preferred_element_type=F32).astype(BF16)


def _in_proj(x_all, g, mod_l, w_in, *, tm, tpb, seq, ctx_row, ssm_w, att_w):
    t, d = x_all.shape
    n = w_in.shape[1]
    row_block = lambda width: pl.BlockSpec((tm, width), lambda i: (i, 0))
    return pl.pallas_call(
        functools.partial(_in_kernel, tpb=tpb, seq=seq, ctx_row=ctx_row, ssm_w=ssm_w, att_w=att_w),
        grid=(t // tm,),
        in_specs=[
            row_block(d),
            pl.BlockSpec((1, d), lambda i: (0, 0)),
            pl.BlockSpec(mod_l.shape, lambda i: (0, 0)),
            pl.BlockSpec((d, n), lambda i: (0, 0)),
        ],
        out_specs=[row_block(ssm_w), row_block(att_w), row_block(att_w), row_block(att_w)],
        out_shape=[
            jax.ShapeDtypeStruct((t, ssm_w), F32),
            jax.ShapeDtypeStruct((t, att_w), BF16),
            jax.ShapeDtypeStruct((t, att_w), BF16),
            jax.ShapeDtypeStruct((t, att_w), BF16),
        ],
        compiler_params=_params("arbitrary"),
        name="in_proj",
    )(x_all, g.reshape(1, d), mod_l, w_in)


def _att_kernel(q_ref, k_ref, v_ref, bias_ref, o_ref, plat_s, pctx_s, den_s, *, seq, n_ctx, rows, win, group):
    lane = lax.broadcasted_iota(jnp.int32, (1, LANES), 1)
    first_head = lane < HEAD_DIM
    mask0 = first_head.astype(BF16)
    mask1 = 1 - mask0
    trans_b = (((1,), (1,)), ((), ()))
    lane_tiles = lambda arrs: [a[:, c:c + LANES] for a in arrs for c in range(0, a.shape[1], LANES)]

    def stack(q):
        return jnp.concatenate([q * mask0, q * mask1], axis=0)

    def unstack(o):
        m_rows = o.shape[0] // 2
        return jnp.where(first_head, o[:m_rows], o[m_rows:])

    def probabilities(q, parts):
        q2 = stack(q)
        scores = []
        for keys, bias in parts:
            s = lax.dot_general(q2, keys, trans_b, preferred_element_type=F32)
            scores.append(s if bias is None else s + bias)
        m = jnp.max(functools.reduce(jnp.maximum, lane_tiles(scores)), axis=-1, keepdims=True)
        probs = [jnp.exp(s - m) for s in scores]
        den = jnp.sum(functools.reduce(lambda a, b: a + b, lane_tiles(probs)), axis=-1, keepdims=True)
        return [p.astype(BF16) for p in probs], den

    def window_start(r):
        return pl.multiple_of(jnp.clip(r - win // 2, 0, rows - win) * GRID_W, GRID_W)

    def score_stage(r, slot):
        q0 = pl.multiple_of(r * GRID_W, GRID_W)
        r0 = jnp.clip(r - win // 2, 0, rows - win)
        kw = k_ref[pl.ds(window_start(r), win * GRID_W), :]
        kc = k_ref[seq:seq + n_ctx, :]
        (p_lat, p_ctx), den = probabilities(q_ref[pl.ds(q0, GRID_W), :], [(kw, bias_ref[r - r0]), (kc, None)])
        plat_s[slot] = p_lat
        pctx_s[slot] = p_ctx
        den_s[slot] = jnp.broadcast_to(den, den_s.shape[1:])

    def value_stage(r, slot):
        q0 = pl.multiple_of(r * GRID_W, GRID_W)
        vw = v_ref[pl.ds(window_start(r), win * GRID_W), :]
        vc = v_ref[seq:seq + n_ctx, :]
        o = (jnp.dot(plat_s[slot], vw, preferred_element_type=F32)
             + jnp.dot(pctx_s[slot], vc, preferred_element_type=F32)) / den_s[slot]
        o_ref[pl.ds(q0, GRID_W), :] = unstack(o).astype(o_ref.dtype)

    n_groups = rows // group
    for g in range(group):
        score_stage(g, g)

    def pipelined(it, carry):
        cur = (it % 2) * group
        nxt = group - cur
        for g in range(group):
            value_stage(it * group + g, cur + g)
        for g in range(group):
            score_stage((it + 1) * group + g, nxt + g)
        return carry

    lax.fori_loop(0, n_groups - 1, pipelined, 0)
    last = ((n_groups - 1) % 2) * group
    for g in range(group):
        value_stage((n_groups - 1) * group + g, last + g)

    kc = k_ref[seq:seq + n_ctx, :]
    vc = v_ref[seq:seq + n_ctx, :]
    (p_ctx,), den = probabilities(q_ref[seq:seq + n_ctx, :], [(kc, None)])
    o = jnp.dot(p_ctx, vc, preferred_element_type=F32) / den
    o_ref[seq:seq + n_ctx, :] = unstack(o).astype(o_ref.dtype)


def _attention(q, k, v, bias, *, batch, rpb_rows, seq, n_ctx):
    t, att_w = q.shape
    rows = seq // GRID_W
    win = min(NA_ROWS_MAX, rows)
    n_pairs = att_w // LANES
    group = 8
    assert rows % group == 0
    blk = pl.BlockSpec((rpb_rows, LANES), lambda b, p: (b, p))
    return pl.pallas_call(
        functools.partial(_att_kernel, seq=seq, n_ctx=n_ctx, rows=rows, win=win, group=group),
        grid=(batch, n_pairs),
        in_specs=[blk, blk, blk,
                  pl.BlockSpec((None, win, 2 * GRID_W, win * GRID_W), lambda b, p: (p, 0, 0, 0))],
        out_specs=blk,
        out_shape=jax.ShapeDtypeStruct((t, att_w), BF16),
        scratch_shapes=[
            pltpu.VMEM((2 * group, 2 * GRID_W, win * GRID_W), BF16),
            pltpu.VMEM((2 * group, 2 * GRID_W, n_ctx), BF16),
            pltpu.VMEM((2 * group, 2 * GRID_W, LANES), F32),
        ],
        compiler_params=_params("arbitrary", "arbitrary"),
        name="attention",
    )(q, k, v, bias)


def _attention_bias(rpb, rows):
    n_heads = rpb.shape[0]
    win = min(NA_ROWS_MAX, rows)
    col = jnp.arange(GRID_W)
    col_start = jnp.clip(col - NA_COLS // 2, 0, GRID_W - NA_COLS)
    kcol = jnp.arange(GRID_W)
    valid = (kcol[None, :] >= col_start[:, None]) & (kcol[None, :] < col_start[:, None] + NA_COLS)
    col_rel = kcol[None, :] - col[:, None] + NA_COLS - 1
    var = jnp.arange(win)
    rr = jnp.arange(win)
    row_rel = rr[None, :] - var[:, None] + NA_ROWS_MAX - 1
    by_row = rpb.astype(F32)[:, row_rel]
    pick = (col_rel[:, :, None] == jnp.arange(2 * NA_COLS - 1)[None, None, :]).astype(F32)
    by_row = by_row.reshape(n_heads // 2, 2, win, win, 2 * NA_COLS - 1)
    tab = jnp.einsum("pavrj,ckj->pvacrk", by_row, pick, precision=lax.Precision.HIGHEST)
    tab = jnp.where(valid[None, None, None, :, None, :], tab, MASK_VALUE)
    return tab.reshape(n_heads // 2, win, 2 * GRID_W, win * GRID_W)


SSM_CHUNK = 16
SLOT = SSM_GROUP
SLOTS = LANES // SLOT


def _slot_of(g, ti):
    return ((g % SLOTS) + (ti % SLOTS)) % SLOTS


def _ssm_pack_kernel(u_ref, t_ref, bre_ref, bim_ref, yi_ref, sre_ref, sim_ref):
    cb, ch, nb, width = u_ref.shape
    rows = cb * nb
    n_groups = width // SSM_GROUP
    lane = lax.broadcasted_iota(jnp.int32, (1, LANES), 1) // SLOT
    slot_mask = [lane == s for s in range(SLOTS)]
    rolled = [[pltpu.roll(u_ref[:, ti, :, j * LANES:(j + 1) * LANES].reshape(rows, LANES), (ti % SLOTS) * SLOT, axis=1)
               for j in range(width // LANES)] for ti in range(ch)]
    for g in range(n_groups):
        j = g // SLOTS
        tiles = []
        for q in range(ch // SLOTS):
            acc = jnp.zeros((rows, LANES), F32)
            for t8 in range(SLOTS):
                ti = q * SLOTS + t8
                acc = jnp.where(slot_mask[_slot_of(g, ti)], rolled[ti][j], acc)
            tiles.append(acc)
        ug = jnp.concatenate(tiles, axis=1).astype(BF16)
        yi_ref[g] = jnp.dot(ug, t_ref[g], preferred_element_type=F32)
        sre_ref[:, g * LANES:(g + 1) * LANES] = jnp.dot(ug, bre_ref[g], preferred_element_type=F32)
        sim_ref[:, g * LANES:(g + 1) * LANES] = jnp.dot(ug, bim_ref[g], preferred_element_type=F32)


def _ssm_carry_kernel(sre_ref, sim_ref, are_ref, aim_ref, hfre_ref, hfim_ref, hbre_ref, hbim_ref, *, nb, n_lat_ch,
                      n_ctx_ch):
    n_ch = n_lat_ch + n_ctx_ch
    lanes = sre_ref.shape[1]
    fwd = (lax.broadcasted_iota(jnp.int32, (1, lanes), 1) % LANES) < (LANES // 2)
    a_re = jnp.broadcast_to(are_ref[...], (nb, lanes))
    a_im = jnp.broadcast_to(aim_ref[...], (nb, lanes))

    def step(k, carry):
        h_re, h_im = carry
        cf = jnp.where(k < n_ctx_ch, n_lat_ch + k, k - n_ctx_ch)
        cb = n_ch - 1 - k
        rf = pl.ds(pl.multiple_of(cf * nb, nb), nb)
        rb = pl.ds(pl.multiple_of(cb * nb, nb), nb)
        hfre_ref[rf, :] = h_re
        hfim_ref[rf, :] = h_im
        hbre_ref[rb, :] = h_re
        hbim_ref[rb, :] = h_im
        s_re = jnp.where(fwd, sre_ref[rf, :], sre_ref[rb, :])
        s_im = jnp.where(fwd, sim_ref[rf, :], sim_ref[rb, :])
        return a_re * h_re - a_im * h_im + s_re, a_re * h_im + a_im * h_re + s_im

    zero = jnp.zeros((nb, lanes), F32)
    lax.fori_loop(0, n_ch, step, (zero, zero))


def _ssm_unpack_kernel(yi_ref, hfre_ref, hfim_ref, hbre_ref, hbim_ref, co_ref, y_ref):
    cb, ch, nb, width = y_ref.shape
    rows = cb * nb
    n_groups = width // SSM_GROUP
    lane = lax.broadcasted_iota(jnp.int32, (1, LANES), 1) // SLOT
    slot_mask = [lane == s for s in range(SLOTS)]
    packed = []
    for g in range(n_groups):
        sl = slice(g * LANES, (g + 1) * LANES)
        h = jnp.concatenate([hfre_ref[:, sl], hfim_ref[:, sl], hbre_ref[:, sl], hbim_ref[:, sl]], axis=1).astype(BF16)
        packed.append(yi_ref[g] + jnp.dot(h, co_ref[g], preferred_element_type=F32))
    for ti in range(ch):
        q = ti // SLOTS
        for j in range(width // LANES):
            acc = jnp.zeros((rows, LANES), F32)
            for g8 in range(SLOTS):
                g = j * SLOTS + g8
                acc = jnp.where(slot_mask[_slot_of(g, ti)], packed[g][:, q * LANES:(q + 1) * LANES], acc)
            tile = pltpu.roll(acc, (LANES - (ti % SLOTS) * SLOT) % LANES, axis=1)
            y_ref[:, ti, :, j * LANES:(j + 1) * LANES] = tile.reshape(cb, nb, LANES)


def _ssm_mixer(u_tm, mats, *, seq, n_ctx):
    total, nb, width = u_tm.shape
    t_mat, b_re, b_im, c_out, a_re, a_im = mats
    n_groups = width // SSM_GROUP
    n_ch = total // SSM_CHUNK
    cblk = next(c for c in (17, 16, 8, 4, 2, 1) if n_ch % c == 0)
    rows = cblk * nb
    u4 = u_tm.reshape(n_ch, SSM_CHUNK, nb, width)
    state_w = n_groups * LANES
    resident = lambda a: pl.BlockSpec(a.shape, lambda i: (0,) * a.ndim, pipeline_mode=pl.Buffered(1))
    seq_block = pl.BlockSpec((cblk, SSM_CHUNK, nb, width), lambda i: (i, 0, 0, 0))
    packed_block = pl.BlockSpec((n_groups, rows, 2 * LANES), lambda i: (0, i, 0))
    state_block = pl.BlockSpec((rows, state_w), lambda i: (i, 0))
    state_shape = jax.ShapeDtypeStruct((n_ch * nb, state_w), F32)
    y_intra, s_re, s_im = pl.pallas_call(
        _ssm_pack_kernel,
        grid=(n_ch // cblk,),
        in_specs=[seq_block, resident(t_mat), resident(b_re), resident(b_im)],
        out_specs=[packed_block, state_block, state_block],
        out_shape=[jax.ShapeDtypeStruct((n_groups, n_ch * nb, 2 * LANES), F32), state_shape, state_shape],
        compiler_params=_params("arbitrary"),
        name="ssm_pack",
    )(u4, t_mat, b_re, b_im)

    lane_blk = 2 * LANES
    col_block = pl.BlockSpec((n_ch * nb, lane_blk), lambda i: (0, i))
    vec_block = pl.BlockSpec((1, lane_blk), lambda i: (0, i))
    h_states = pl.pallas_call(
        functools.partial(_ssm_carry_kernel, nb=nb, n_lat_ch=seq // SSM_CHUNK, n_ctx_ch=n_ctx // SSM_CHUNK),
        grid=(state_w // lane_blk,),
        in_specs=[col_block, col_block, vec_block, vec_block],
        out_specs=[col_block] * 4,
        out_shape=[state_shape] * 4,
        compiler_params=_params("arbitrary"),
        name="ssm_carry",
    )(s_re, s_im, a_re, a_im)

    y4 = pl.pallas_call(
        _ssm_unpack_kernel,
        grid=(n_ch // cblk,),
        in_specs=[packed_block] + [state_block] * 4 + [resident(c_out)],
        out_specs=seq_block,
        out_shape=jax.ShapeDtypeStruct(u4.shape, F32),
        compiler_params=_params("arbitrary"),
        name="ssm_unpack",
    )(y_intra, *h_states, c_out)
    return y4.reshape(total, nb, width)


def _ssm_matrices(a_re, a_im, log_step, b_re, b_im, c_re, c_im, d_skip):
    hp = lax.Precision.HIGHEST
    _, g, p, h = b_re.shape
    ch = SSM_CHUNK
    a_re = jnp.minimum(a_re.astype(F32), A_RE_MAX)
    a_im = a_im.astype(F32)
    dt = jnp.exp(log_step.astype(F32))[..., None]
    k = jnp.arange(ch + 1, dtype=F32)[:, None, None, None]
    mag = jnp.exp(a_re * dt * k)
    lr = mag * jnp.cos(a_im * dt * k)
    li = mag * jnp.sin(a_im * dt * k)
    lam_re, lam_im = lr[1], li[1]
    den = a_re * a_re + a_im * a_im
    z_re = ((lam_re - 1) * a_re + lam_im * a_im) / den
    z_im = (lam_im * a_re - (lam_re - 1) * a_im) / den
    b_re = b_re.astype(F32)
    b_im = b_im.astype(F32)
    bb_re = z_re[..., None] * b_re - z_im[..., None] * b_im
    bb_im = z_re[..., None] * b_im + z_im[..., None] * b_re
    c_re = c_re.astype(F32)
    c_im = c_im.astype(F32)
    w_re = c_re[None] * lr[:, :, :, None, :] - c_im[None] * li[:, :, :, None, :]
    w_im = c_re[None] * li[:, :, :, None, :] + c_im[None] * lr[:, :, :, None, :]
    kern = (jnp.einsum("dxgop,xgpi->dxgoi", w_re, bb_re, precision=hp)
            - jnp.einsum("dxgop,xgpi->dxgoi", w_im, bb_im, precision=hp))
    s_idx = jnp.arange(ch)[:, None]
    t_idx = jnp.arange(ch)[None, :]
    lag_f = jnp.clip(t_idx - s_idx, 0, ch)
    lag_b = jnp.clip(s_idx - t_idx, 0, ch)
    toe = (jnp.where((s_idx <= t_idx)[:, :, None, None, None], kern[lag_f, 0], 0.0)
           + jnp.where((s_idx >= t_idx)[:, :, None, None, None], kern[lag_b, 1], 0.0))
    skip = d_skip.astype(F32).reshape(g, h)
    toe = toe + (jnp.eye(ch, dtype=F32)[:, :, None, None, None] * jnp.eye(h, dtype=F32)[None, None, None]
                 * skip[None, None, :, :, None])
    t_nat = toe.transpose(2, 0, 4, 1, 3).reshape(g, ch * h, ch * h)

    pw_f = (ch - 1) - jnp.arange(ch)
    pw_b = jnp.arange(ch)
    cm = lambda ar, ai, br, bi: (ar * br - ai * bi, ar * bi + ai * br)
    mf_re, mf_im = cm(lr[pw_f, 0][..., None], li[pw_f, 0][..., None], bb_re[0][None], bb_im[0][None])
    mb_re, mb_im = cm(lr[pw_b, 1][..., None], li[pw_b, 1][..., None], bb_re[1][None], bb_im[1][None])
    pack_state = lambda f, b: jnp.concatenate([f, b], axis=2).transpose(1, 0, 3, 2).reshape(g, ch * h, 2 * p)
    bs_re = pack_state(mf_re, mb_re)
    bs_im = pack_state(mf_im, mb_im)

    pf = jnp.arange(ch) + 1
    pb = ch - jnp.arange(ch)
    rd = lambda w, d: w[:, d].transpose(1, 3, 0, 2).reshape(g, p, ch * h)
    zeros = jnp.zeros((g, p, ch * h), F32)
    co = jnp.concatenate([rd(w_re[pf], 0), zeros, -rd(w_im[pf], 0), zeros,
                          zeros, rd(w_re[pb], 1), zeros, -rd(w_im[pb], 1)], axis=1)

    gi = jnp.arange(g)[:, None, None]
    ti = jnp.arange(ch)[None, :, None]
    hi = jnp.arange(h)[None, None, :]
    pos = ((ti // SLOTS) * LANES + (((gi % SLOTS) + (ti % SLOTS)) % SLOTS) * SLOT + hi).reshape(g, ch * h)
    inv = jnp.argsort(pos, axis=1)
    rows_p = lambda m: jnp.take_along_axis(m, inv[:, :, None], axis=1)
    cols_p = lambda m: jnp.take_along_axis(m, inv[:, None, :], axis=2)
    t_mat = cols_p(rows_p(t_nat)).astype(BF16)
    a16_re = jnp.concatenate([lr[ch, 0], lr[ch, 1]], axis=1).reshape(1, g * 2 * p)
    a16_im = jnp.concatenate([li[ch, 0], li[ch, 1]], axis=1).reshape(1, g * 2 * p)
    return (t_mat, rows_p(bs_re).astype(BF16), rows_p(bs_im).astype(BF16), cols_p(co).astype(BF16), a16_re, a16_im)


def _mix_kernel(y_ref, att_ref, x_ref, mod_ref, gw_ref, gb_ref, wo_ref, o_ref, *, tpb, seq, ctx_row):
    i = pl.program_id(0)
    tm, d = x_ref.shape
    ssm_w = y_ref.shape[1]
    g = jax.nn.gelu(y_ref[...])
    z = jnp.dot(g.astype(BF16), gw_ref[...], preferred_element_type=F32) + gb_ref[...]
    s = g * jax.nn.sigmoid(z)
    o = (jnp.dot(s.astype(BF16), wo_ref[0:ssm_w, :], preferred_element_type=F32)
         + jnp.dot(att_ref[...], wo_ref[ssm_w:, :], preferred_element_type=F32))
    gate = _mod_rows(mod_ref, 2, i, tm, tpb, seq, ctx_row, d)
    o_ref[...] = x_ref[...] + gate * o


def _mix_out(y_ssm, y_att, x_all, mod_l, glu_w, glu_b, w_out, *, tm, tpb, seq, ctx_row):
    t, d = x_all.shape
    ssm_w = y_ssm.shape[1]
    att_w = y_att.shape[1]
    row_block = lambda width: pl.BlockSpec((tm, width), lambda i: (i, 0))
    whole = lambda a: pl.BlockSpec(a.shape, lambda i: (0, 0))
    gb = glu_b.reshape(1, ssm_w)
    return pl.pallas_call(
        functools.partial(_mix_kernel, tpb=tpb, seq=seq, ctx_row=ctx_row),
        grid=(t // tm,),
        in_specs=[row_block(ssm_w), row_block(att_w), row_block(d), whole(mod_l), whole(glu_w), whole(gb),
                  whole(w_out)],
        out_specs=row_block(d),
        out_shape=jax.ShapeDtypeStruct((t, d), F32),
        compiler_params=_params("arbitrary"),
        name="mix_out",
    )(y_ssm, y_att, x_all, mod_l, glu_w, gb, w_out)


def _ffn_kernel(x_ref, g_ref, mod_ref, w1_ref, w3_ref, w2_ref, o_ref, t_s, acc_s, *, tpb, seq, ctx_row):
    i = pl.program_id(0)
    tm, d = x_ref.shape
    nf = w1_ref.shape[0]
    t_s[...] = _norm_mod(x_ref[...], g_ref[...], mod_ref, 3, i, tpb, seq, ctx_row).astype(BF16)
    acc_s[...] = jnp.zeros_like(acc_s)

    def chunk(c, carry):
        t = t_s[...]
        h1 = jnp.dot(t, w1_ref[c], preferred_element_type=F32)
        h3 = jnp.dot(t, w3_ref[c], preferred_element_type=F32)
        a = (jax.nn.silu(h1) * h3).astype(BF16)
        acc_s[...] += jnp.dot(a, w2_ref[c], preferred_element_type=F32)
        return carry

    lax.fori_loop(0, nf, chunk, 0, unroll=2)
    gate = _mod_rows(mod_ref, 5, i, tm, tpb, seq, ctx_row, d)
    o_ref[...] = x_ref[...] + gate * acc_s[...]


def _dense_ffn(x_all, g, mod_l, w1, w3, w2, *, tm, tpb, seq, ctx_row):
    t, d = x_all.shape
    dff = w1.shape[1]
    tf = _chunk(dff, 256)
    nf = dff // tf
    w1c = w1.reshape(d, nf, tf).transpose(1, 0, 2)
    w3c = w3.reshape(d, nf, tf).transpose(1, 0, 2)
    w2c = w2.reshape(nf, tf, d)
    resident = lambda a: pl.BlockSpec(a.shape, lambda i: (0,) * a.ndim, pipeline_mode=pl.Buffered(1))
    return pl.pallas_call(
        functools.partial(_ffn_kernel, tpb=tpb, seq=seq, ctx_row=ctx_row),
        grid=(t // tm,),
        in_specs=[
            pl.BlockSpec((tm, d), lambda i: (i, 0)),
            pl.BlockSpec((1, d), lambda i: (0, 0)),
            pl.BlockSpec(mod_l.shape, lambda i: (0, 0)),
            resident(w1c), resident(w3c), resident(w2c),
        ],
        out_specs=pl.BlockSpec((tm, d), lambda i: (i, 0)),
        out_shape=jax.ShapeDtypeStruct((t, d), F32),
        scratch_shapes=[pltpu.VMEM((tm, d), BF16), pltpu.VMEM((tm, d), F32)],
        compiler_params=_params("arbitrary"),
        name="dense_ffn",
    )(x_all, g.reshape(1, d), mod_l, w1c, w3c, w2c)


def _route_kernel(x_ref, g_ref, mod_ref, r_ref, t_ref, route_ref, *, tpb, seq, ctx_row):
    i = pl.program_id(0)
    t = _norm_mod(x_ref[...], g_ref[...], mod_ref, 3, i, tpb, seq, ctx_row)
    t_ref[...] = t
    th = t.astype(BF16)
    tl = (t - th.astype(F32)).astype(BF16)
    r = r_ref[...]
    rh = r.astype(BF16)
    rl = (r - rh.astype(F32)).astype(BF16)
    dot = lambda a, b: jnp.dot(a, b, preferred_element_type=F32)
    logits = dot(th, rh) + (dot(th, rl) + dot(tl, rh)) + dot(tl, rl)
    tm = logits.shape[0]
    lane = lax.broadcasted_iota(jnp.int32, (tm, LANES), 1)
    lane_f = lane.astype(F32)
    neg_inf = jnp.float32(-jnp.inf)
    lg = jnp.where(lane < N_EXPERTS, logits, neg_inf)
    m1 = jnp.max(lg, axis=-1, keepdims=True)
    i1 = jnp.min(jnp.where(lg == m1, lane_f, float(LANES)), axis=-1, keepdims=True)
    lg2 = jnp.where(lane_f == i1, neg_inf, lg)
    m2 = jnp.max(lg2, axis=-1, keepdims=True)
    i2 = jnp.min(jnp.where(lg2 == m2, lane_f, float(LANES)), axis=-1, keepdims=True)
    e = jnp.exp(m2 - m1)
    g1 = 1.0 / (1.0 + e)
    g2 = e / (1.0 + e)
    route_ref[...] = jnp.where(lane == 0, i1, jnp.where(lane == 1, i2, jnp.where(lane == 2, g1, jnp.where(
        lane == 3, g2, 0.0))))


def _route(x_all, g, mod_l, router, *, tm, tpb, seq, ctx_row):
    t, d = x_all.shape
    r_pad = jnp.zeros((d, LANES), F32).at[:, :router.shape[1]].set(router.astype(F32))
    return pl.pallas_call(
        functools.partial(_route_kernel, tpb=tpb, seq=seq, ctx_row=ctx_row),
        grid=(t // tm,),
        in_specs=[
            pl.BlockSpec((tm, d), lambda i: (i, 0)),
            pl.BlockSpec((1, d), lambda i: (0, 0)),
            pl.BlockSpec(mod_l.shape, lambda i: (0, 0)),
            pl.BlockSpec((d, LANES), lambda i: (0, 0)),
        ],
        out_specs=[pl.BlockSpec((tm, d), lambda i: (i, 0)), pl.BlockSpec((tm, LANES), lambda i: (i, 0))],
        out_shape=[jax.ShapeDtypeStruct((t, d), F32), jax.ShapeDtypeStruct((t, LANES), F32)],
        compiler_params=_params("arbitrary"),
        name="moe_route",
    )(x_all, g.reshape(1, d), mod_l, r_pad)


def _row_copy(src_hbm, dst_vmem, sem, src_row, dst_row):
    return pltpu.make_async_copy(src_hbm.at[pl.ds(src_row, 1), :], dst_vmem.at[pl.ds(dst_row, 1), :], sem)


ISSUE_UNROLL = 8


ZERO_BLOCK_ROWS = 256


def _dispatch_kernel(fill_lo_ref, fill_hi_ref, pos_ref, t_ref, xs_hbm, zero_s, sems):
    i = pl.program_id(0)
    tm = t_ref.shape[0]
    per_trip = ISSUE_UNROLL // 2

    def issue(g, carry):
        for k in range(per_trip):
            r = g * per_trip + k
            for choice in range(2):
                pltpu.make_async_copy(t_ref.at[pl.ds(r, 1), :], xs_hbm.at[pl.ds(pos_ref[0, 0, 2 * r + choice], 1), :],
                                      sems.at[choice]).start()
        return carry

    lax.fori_loop(0, tm // per_trip, issue, 0)

    @pl.when(i == 0)
    def _():
        zero_s[...] = jnp.zeros_like(zero_s)
        zero_row = lambda row: pltpu.make_async_copy(zero_s.at[pl.ds(0, 1), :], xs_hbm.at[pl.ds(row, 1), :],
                                                     sems.at[2])
        zero_block = lambda blk: pltpu.make_async_copy(
            zero_s, xs_hbm.at[pl.ds(pl.multiple_of(blk * ZERO_BLOCK_ROWS, ZERO_BLOCK_ROWS), ZERO_BLOCK_ROWS), :],
            sems.at[2])

        def start_then_wait(copy, lo, hi):
            def start(k, carry):
                copy(k).start()
                return carry

            def wait(k, carry):
                copy(k).wait()
                return carry

            lax.fori_loop(lo, hi, start, 0)
            lax.fori_loop(lo, hi, wait, 0)

        for e in range(N_EXPERTS):
            start_then_wait(zero_row, fill_lo_ref[e], fill_hi_ref[e])
        start_then_wait(zero_block, fill_lo_ref[N_EXPERTS] // ZERO_BLOCK_ROWS,
                        fill_hi_ref[N_EXPERTS] // ZERO_BLOCK_ROWS)

    for choice in range(2):
        pltpu.make_async_copy(t_ref, xs_hbm.at[pl.ds(0, tm), :], sems.at[choice]).wait()


def _dispatch_rows(tokens, pos, fill_lo, fill_hi, *, tm, n_rows):
    t, d = tokens.shape
    steps = t // tm
    assert tm % (ISSUE_UNROLL // 2) == 0 and n_rows % ZERO_BLOCK_ROWS == 0
    grid_spec = pltpu.PrefetchScalarGridSpec(
        num_scalar_prefetch=2,
        grid=(steps,),
        in_specs=[
            pl.BlockSpec((1, 1, 2 * tm), lambda i, lo, hi: (i, 0, 0), memory_space=pltpu.SMEM),
            pl.BlockSpec((tm, d), lambda i, lo, hi: (i, 0)),
        ],
        out_specs=pl.BlockSpec(memory_space=pl.ANY),
        scratch_shapes=[pltpu.VMEM((ZERO_BLOCK_ROWS, d), tokens.dtype), pltpu.SemaphoreType.DMA((3,))],
    )
    return pl.pallas_call(
        _dispatch_kernel,
        grid_spec=grid_spec,
        out_shape=jax.ShapeDtypeStruct((n_rows, d), tokens.dtype),
        compiler_params=_params("arbitrary"),
        name="moe_dispatch",
    )(fill_lo, fill_hi, pos.reshape(steps, 1, 2 * tm), tokens)


def _gmm_kernel(te_ref, tv_ref, ts_ref, xs_ref, w1_ref, w3_ref, w2_ref, o_ref, xb_s, acc_s, *, nf, n_sub):
    j = pl.program_id(0)
    f = pl.program_id(1)
    valid = tv_ref[j] > 0

    @pl.when(valid)
    def _():
        @pl.when(f == 0)
        def _():
            xb_s[...] = xs_ref[...].astype(BF16)
            acc_s[...] = jnp.zeros_like(acc_s)

        x = xb_s[...]
        tf = w1_ref.shape[1]
        sub = tf // n_sub
        part = None
        for c0 in range(0, tf, sub):
            h1 = jnp.dot(x, w1_ref[:, c0:c0 + sub].astype(BF16), preferred_element_type=F32)
            h3 = jnp.dot(x, w3_ref[:, c0:c0 + sub].astype(BF16), preferred_element_type=F32)
            a = (jax.nn.silu(h1) * h3).astype(BF16)
            y = jnp.dot(a, w2_ref[c0:c0 + sub, :].astype(BF16), preferred_element_type=F32)
            part = y if part is None else part + y
        acc_s[...] += part

        @pl.when(f == nf - 1)
        def _():
            o_ref[...] = acc_s[...]

    @pl.when(jnp.logical_and(jnp.logical_not(valid), f == nf - 1))
    def _():
        o_ref[...] = jnp.zeros_like(o_ref)


def _grouped_swiglu(xs, tile_expert, tile_valid, tile_src, w1, w3, w2, *, tme, layer):
    n, d = xs.shape
    dff = w1.shape[3]
    tf = _chunk(dff, 512)
    nf = dff // tf
    grid_spec = pltpu.PrefetchScalarGridSpec(
        num_scalar_prefetch=3,
        grid=(n // tme, nf),
        in_specs=[
            pl.BlockSpec((tme, d), lambda j, f, te, tv, ts: (ts[j], 0)),
            pl.BlockSpec((None, None, d, tf), lambda j, f, te, tv, ts: (layer, te[j], 0, f)),
            pl.BlockSpec((None, None, d, tf), lambda j, f, te, tv, ts: (layer, te[j], 0, f)),
            pl.BlockSpec((None, None, tf, d), lambda j, f, te, tv, ts: (layer, te[j], f, 0)),
        ],
        out_specs=pl.BlockSpec((tme, d), lambda j, f, te, tv, ts: (j, 0)),
        scratch_shapes=[pltpu.VMEM((tme, d), BF16), pltpu.VMEM((tme, d), F32)],
    )
    return pl.pallas_call(
        functools.partial(_gmm_kernel, nf=nf, n_sub=2),
        grid_spec=grid_spec,
        out_shape=jax.ShapeDtypeStruct((n, d), F32),
        compiler_params=_params("arbitrary", "arbitrary"),
        name="moe_experts",
    )(tile_expert, tile_valid, tile_src, xs, w1, w3, w2)


def _combine_kernel(pos_ref, pos_next_ref, ys_hbm, x_ref, route_ref, mod_ref, o_ref, buf, sems, *, tpb, seq,
                    ctx_row):
    i = pl.program_id(0)
    tm, d = x_ref.shape

    def issue_all(pos, slot):
        def issue(g, carry):
            for k in range(ISSUE_UNROLL // 2):
                r = g * (ISSUE_UNROLL // 2) + k
                _row_copy(ys_hbm, buf.at[slot, 0], sems.at[slot, 0], pos[0, 0, 2 * r], r).start()
                _row_copy(ys_hbm, buf.at[slot, 1], sems.at[slot, 1], pos[0, 0, 2 * r + 1], r).start()
            return carry

        lax.fori_loop(0, tm // (ISSUE_UNROLL // 2), issue, 0)

    @pl.when(i == 0)
    def _():
        issue_all(pos_ref, 0)

    @pl.when(i + 1 < pl.num_programs(0))
    def _():
        issue_all(pos_next_ref, (i + 1) % 2)

    slot = i % 2
    pltpu.make_async_copy(ys_hbm.at[pl.ds(0, tm), :], buf.at[slot, 0], sems.at[slot, 0]).wait()
    pltpu.make_async_copy(ys_hbm.at[pl.ds(0, tm), :], buf.at[slot, 1], sems.at[slot, 1]).wait()
    g1 = route_ref[:, 2:3]
    g2 = route_ref[:, 3:4]
    gate = _mod_rows(mod_ref, 5, i, tm, tpb, seq, ctx_row, d)
    o_ref[...] = x_ref[...] + gate * (g1 * buf[slot, 0] + g2 * buf[slot, 1])


def _combine(ys, pos, x_all, route, mod_l, *, tm, tpb, seq, ctx_row):
    t, d = x_all.shape
    steps = t // tm
    assert tm % (ISSUE_UNROLL // 2) == 0
    pos3 = pos.reshape(steps, 1, 2 * tm)
    return pl.pallas_call(
        functools.partial(_combine_kernel, tpb=tpb, seq=seq, ctx_row=ctx_row),
        grid=(steps,),
        in_specs=[
            pl.BlockSpec((1, 1, 2 * tm), lambda i: (i, 0, 0), memory_space=pltpu.SMEM),
            pl.BlockSpec((1, 1, 2 * tm), lambda i: (jnp.minimum(i + 1, steps - 1), 0, 0), memory_space=pltpu.SMEM),
            pl.BlockSpec(memory_space=pl.ANY),
            pl.BlockSpec((tm, d), lambda i: (i, 0)),
            pl.BlockSpec((tm, LANES), lambda i: (i, 0)),
            pl.BlockSpec(mod_l.shape, lambda i: (0, 0)),
        ],
        out_specs=pl.BlockSpec((tm, d), lambda i: (i, 0)),
        out_shape=jax.ShapeDtypeStruct((t, d), F32),
        scratch_shapes=[pltpu.VMEM((2, 2, tm, d), F32), pltpu.SemaphoreType.DMA((2, 2))],
        compiler_params=_params("arbitrary"),
        name="moe_combine",
    )(pos3, pos3, ys, x_all, route, mod_l)


def _moe_ffn(x_all, g, mod_l, router, w1, w3, w2, *, tm, tpb, seq, ctx_row, tme, layer):
    t = x_all.shape[0]
    tokens, route = _route(x_all, g, mod_l, router, tm=tm, tpb=tpb, seq=seq, ctx_row=ctx_row)
    e_flat = route[:, 0:2].astype(jnp.int32).reshape(-1)
    onehot = (e_flat[:, None] == jnp.arange(N_EXPERTS, dtype=jnp.int32)[None, :]).astype(jnp.int32)
    csum = jnp.cumsum(onehot, axis=0)
    rank = jnp.take_along_axis(csum, e_flat[:, None], axis=1)[:, 0] - 1
    counts = csum[-1]
    padded = ((counts + tme - 1) // tme) * tme
    seg_end = jnp.cumsum(padded)
    seg_start = seg_end - padded
    pos = seg_start[e_flat] + rank
    n_tiles = -(-(2 * t + N_EXPERTS * (tme - 1)) // tme)
    tile_start = jnp.arange(n_tiles, dtype=jnp.int32) * tme
    tile_expert = jnp.minimum(jnp.sum((tile_start[:, None] >= seg_end[None, :]).astype(jnp.int32), axis=1),
                              N_EXPERTS - 1)
    tile_valid = (tile_start < seg_end[-1]).astype(jnp.int32)
    tile_src = jnp.minimum(jnp.arange(n_tiles, dtype=jnp.int32), seg_end[-1] // tme - 1)

    n_rows = n_tiles * tme
    fill_lo = jnp.concatenate([seg_start + counts, seg_end[-1:]]).astype(jnp.int32)
    fill_hi = jnp.concatenate([seg_end, jnp.full((1,), n_rows, jnp.int32)]).astype(jnp.int32)
    xs = _dispatch_rows(tokens, pos, fill_lo, fill_hi, tm=tm, n_rows=n_rows)
    ys = _grouped_swiglu(xs, tile_expert, tile_valid, tile_src, w1, w3, w2, tme=tme, layer=layer)
    return _combine(ys, pos, x_all, route, mod_l, tm=tm, tpb=tpb, seq=seq, ctx_row=ctx_row)


def _final_kernel(x_ref, g_ref, o_ref):
    x = x_ref[...]
    o_ref[...] = x * lax.rsqrt(jnp.mean(x * x, axis=-1, keepdims=True) + NORM_EPS) * g_ref[...]


def _final_norm(x3, g, *, seq):
    batch, _, d = x3.shape
    tr = _chunk(seq, 1024)
    return pl.pallas_call(
        _final_kernel,
        grid=(batch, seq // tr),
        in_specs=[pl.BlockSpec((None, tr, d), lambda b, j: (b, j, 0)), pl.BlockSpec((1, d), lambda b, j: (0, 0))],
        out_specs=pl.BlockSpec((None, tr, d), lambda b, j: (b, j, 0)),
        out_shape=jax.ShapeDtypeStruct((batch, seq, d), F32),
        compiler_params=_params("arbitrary", "arbitrary"),
        name="final_norm",
    )(x3, g.reshape(1, d))


def kernel(x, c, ctx, c_ctx, w_mod, b_mod, g_mix, g_ffn, w_in, w_out, ssm_a_re, ssm_a_im, ssm_log_step, ssm_b_re, ssm_b_im, ssm_c_re, ssm_c_im, ssm_d, glu_w, glu_b, na_rpb, ffn_w1, ffn_w3, ffn_w2, moe_router, moe_w1, moe_w3, moe_w2, g_final):
    batch, seq, d = x.shape
    n_ctx = ctx.shape[1]
    depth = w_mod.shape[0]
    ssm_w = ssm_d.shape[1]
    att_w = (w_in.shape[2] - ssm_w) // 3
    rpb_rows = seq + n_ctx
    assert batch == SUBLANES and batch < MOD_ROWS
    assert seq % GRID_W == 0 and att_w % LANES == 0 and seq % SSM_CHUNK == 0 and n_ctx % SSM_CHUNK == 0
    tm = _token_tile(rpb_rows)
    tpb = rpb_rows // tm
    tme = 1024
    common = dict(tm=tm, tpb=tpb, seq=seq, ctx_row=batch)

    cvec = jnp.zeros((MOD_ROWS, d), F32).at[:batch].set(c.astype(F32)).at[batch].set(c_ctx.astype(F32))
    mod = _mod_table(cvec, w_mod.astype(F32), b_mod.astype(F32))
    x_all = jnp.concatenate([x, ctx], axis=1).astype(F32).reshape(batch * rpb_rows, d)


    for l in range(depth):
        mod_l = mod[l]
        u, q, k, v = _in_proj(x_all, g_mix[l].astype(F32), mod_l, w_in[l].astype(BF16), ssm_w=ssm_w, att_w=att_w,
                              **common)

        u_tm = u.reshape(batch, rpb_rows, ssm_w).transpose(1, 0, 2)
        mats = _ssm_matrices(ssm_a_re[l], ssm_a_im[l], ssm_log_step[l], ssm_b_re[l], ssm_b_im[l], ssm_c_re[l],
                             ssm_c_im[l], ssm_d[l])
        y_tm = _ssm_mixer(u_tm, mats, seq=seq, n_ctx=n_ctx)
        y_ssm = y_tm.transpose(1, 0, 2).reshape(batch * rpb_rows, ssm_w)

        bias = _attention_bias(na_rpb[l], seq // GRID_W)
        y_att = _attention(q, k, v, bias, batch=batch, rpb_rows=rpb_rows, seq=seq, n_ctx=n_ctx)

        x_all = _mix_out(y_ssm, y_att, x_all, mod_l, glu_w[l].astype(BF16), glu_b[l].astype(F32),
                         w_out[l].astype(BF16), **common)

        if l % 2 == 0:
            x_all = _dense_ffn(x_all, g_ffn[l].astype(F32), mod_l, ffn_w1[l // 2].astype(BF16),
                               ffn_w3[l // 2].astype(BF16), ffn_w2[l // 2].astype(BF16), **common)
        else:
            x_all = _moe_ffn(x_all, g_ffn[l].astype(F32), mod_l, moe_router[l // 2], moe_w1, moe_w3, moe_w2,
                             tme=tme, layer=l // 2, **common)

    return _final_norm(x_all.reshape(batch, rpb_rows, d), g_final.astype(F32), seq=seq).astype(x.dtype)
```

```python
import functools
import math

import jax
import jax.numpy as jnp
import numpy as np
from jax import lax
from jax.experimental import pallas as pl
from jax.experimental.pallas import tpu as pltpu

F32 = jnp.float32
BF16 = jnp.bfloat16

GRID_W = 64
SSM_GROUP = 16
SSM_STATE = 64
HEAD_DIM = 64
NA_ROWS_MAX = 8
NA_COLS = 16
N_EXPERTS = 8
NORM_EPS = 1e-6
A_RE_MAX = -1e-4
MASK_VALUE = -1e30

LANES = 128
SUBLANES = 8
V7X_VMEM_LIMIT_BYTES = 56 * 1024 * 1024

MOD_ROWS = 16


def _params(*semantics):
    return pltpu.CompilerParams(dimension_semantics=semantics, vmem_limit_bytes=V7X_VMEM_LIMIT_BYTES)


def _token_tile(rows_per_batch):
    for parts in range(1, rows_per_batch + 1):
        if rows_per_batch % parts == 0:
            tm = rows_per_batch // parts
            if tm <= 1152 and tm % 16 == 0:
                return tm
    raise ValueError("no token tile for %d rows" % rows_per_batch)


def _chunk(total, target):
    best = None
    for c in range(LANES, min(total, target) + 1, LANES):
        if total % c == 0:
            best = c
    if best is None:
        raise ValueError("no lane-aligned chunk for %d" % total)
    return best


def _mod_kernel(c_ref, w_ref, b_ref, o_ref):
    a = jax.nn.silu(c_ref[...])
    o_ref[...] = jnp.dot(a, w_ref[...], preferred_element_type=F32) + b_ref[...]


def _mod_table(cvec, w_mod, b_mod):
    depth, d, n = w_mod.shape
    tn = _chunk(n, 1536)
    return pl.pallas_call(
        _mod_kernel,
        grid=(depth, n // tn),
        in_specs=[
            pl.BlockSpec((MOD_ROWS, d), lambda l, j: (0, 0)),
            pl.BlockSpec((None, d, tn), lambda l, j: (l, 0, j)),
            pl.BlockSpec((None, 1, tn), lambda l, j: (l, 0, j)),
        ],
        out_specs=pl.BlockSpec((None, MOD_ROWS, tn), lambda l, j: (l, 0, j)),
        out_shape=jax.ShapeDtypeStruct((depth, MOD_ROWS, n), F32),
        compiler_params=_params("arbitrary", "arbitrary"),
        name="mod_table",
    )(cvec, w_mod, b_mod.reshape(depth, 1, n))


def _mod_rows(mod_ref, slab, tile_idx, tm, tpb, seq, ctx_row, d):
    b = tile_idx // tpb
    v_b = mod_ref[pl.ds(b, 1), slab * d:(slab + 1) * d]
    v_c = mod_ref[ctx_row:ctx_row + 1, slab * d:(slab + 1) * d]
    row = (tile_idx % tpb) * tm + lax.broadcasted_iota(jnp.int32, (tm, 1), 0)
    return jnp.where(row >= seq, v_c, v_b)


def _norm_mod(x, g, mod_ref, slab, tile_idx, tpb, seq, ctx_row):
    tm, d = x.shape
    shift = _mod_rows(mod_ref, slab, tile_idx, tm, tpb, seq, ctx_row, d)
    scale = _mod_rows(mod_ref, slab + 1, tile_idx, tm, tpb, seq, ctx_row, d)
    rs = lax.rsqrt(jnp.mean(x * x, axis=-1, keepdims=True) + NORM_EPS)
    return (x * rs) * g * (1.0 + scale) + shift


def _in_kernel(x_ref, g_ref, mod_ref, w_ref, u_ref, q_ref, k_ref, v_ref, *, tpb, seq, ctx_row, ssm_w, att_w):
    i = pl.program_id(0)
    h = _norm_mod(x_ref[...], g_ref[...], mod_ref, 0, i, tpb, seq, ctx_row).astype(BF16)
    c1 = ssm_w + att_w
    c2 = c1 + att_w
    u_ref[...] = jnp.dot(h, w_ref[:, 0:ssm_w], preferred_element_type=F32)
    q_ref[...] = (jnp.dot(h, w_ref[:, ssm_w:c1], preferred_element_type=F32) * (HEAD_DIM ** -0.5)).astype(BF16)
    k_ref[...] = jnp.dot(h, w_ref[:, c1:c2], preferred_element_type=F32).astype(BF16)
    v_ref[...] = jnp.dot(h, w_ref[:, c2:c2 + att_w], preferred_element_type=F32).astype(BF16)


def _in_proj(x_all, g, mod_l, w_in, *, tm, tpb, seq, ctx_row, ssm_w, att_w):
    t, d = x_all.shape
    n = w_in.shape[1]
    row_block = lambda width: pl.BlockSpec((tm, width), lambda i: (i, 0))
    return pl.pallas_call(
        functools.partial(_in_kernel, tpb=tpb, seq=seq, ctx_row=ctx_row, ssm_w=ssm_w, att_w=att_w),
        grid=(t // tm,),
        in_specs=[
            row_block(d),
            pl.BlockSpec((1, d), lambda i: (0, 0)),
            pl.BlockSpec(mod_l.shape, lambda i: (0, 0)),
            pl.BlockSpec((d, n), lambda i: (0, 0)),
        ],
        out_specs=[row_block(ssm_w), row_block(att_w), row_block(att_w), row_block(att_w)],
        out_shape=[
            jax.ShapeDtypeStruct((t, ssm_w), F32),
            jax.ShapeDtypeStruct((t, att_w), BF16),
            jax.ShapeDtypeStruct((t, att_w), BF16),
            jax.ShapeDtypeStruct((t, att_w), BF16),
        ],
        compiler_params=_params("arbitrary"),
        name="in_proj",
    )(x_all, g.reshape(1, d), mod_l, w_in)


def _att_kernel(q_ref, k_ref, v_ref, bias_ref, o_ref, plat_s, pctx_s, den_s, *, seq, n_ctx, rows, win, group):
    lane = lax.broadcasted_iota(jnp.int32, (1, LANES), 1)
    first_head = lane < HEAD_DIM
    mask0 = first_head.astype(BF16)
    mask1 = 1 - mask0
    trans_b = (((1,), (1,)), ((), ()))
    lane_tiles = lambda arrs: [a[:, c:c + LANES] for a in arrs for c in range(0, a.shape[1], LANES)]

    def stack(q):
        return jnp.concatenate([q * mask0, q * mask1], axis=0)

    def unstack(o):
        m_rows = o.shape[0] // 2
        return jnp.where(first_head, o[:m_rows], o[m_rows:])

    def probabilities(q, parts):
        q2 = stack(q)
        scores = []
        for keys, bias in parts:
            s = lax.dot_general(q2, keys, trans_b, preferred_element_type=F32)
            scores.append(s if bias is None else s + bias)
        m = jnp.max(functools.reduce(jnp.maximum, lane_tiles(scores)), axis=-1, keepdims=True)
        probs = [jnp.exp(s - m) for s in scores]
        den = jnp.sum(functools.reduce(lambda a, b: a + b, lane_tiles(probs)), axis=-1, keepdims=True)
        return [p.astype(BF16) for p in probs], den

    def window_start(r):
        return pl.multiple_of(jnp.clip(r - win // 2, 0, rows - win) * GRID_W, GRID_W)

    def score_stage(r, slot):
        q0 = pl.multiple_of(r * GRID_W, GRID_W)
        r0 = jnp.clip(r - win // 2, 0, rows - win)
        kw = k_ref[pl.ds(window_start(r), win * GRID_W), :]
        kc = k_ref[seq:seq + n_ctx, :]
        (p_lat, p_ctx), den = probabilities(q_ref[pl.ds(q0, GRID_W), :], [(kw, bias_ref[r - r0]), (kc, None)])
        plat_s[slot] = p_lat
        pctx_s[slot] = p_ctx
        den_s[slot] = jnp.broadcast_to(den, den_s.shape[1:])

    def value_stage(r, slot):
        q0 = pl.multiple_of(r * GRID_W, GRID_W)
        vw = v_ref[pl.ds(window_start(r), win * GRID_W), :]
        vc = v_ref[seq:seq + n_ctx, :]
        o = (jnp.dot(plat_s[slot], vw, preferred_element_type=F32)
             + jnp.dot(pctx_s[slot], vc, preferred_element_type=F32)) / den_s[slot]
        o_ref[pl.ds(q0, GRID_W), :] = unstack(o).astype(o_ref.dtype)

    n_groups = rows // group
    for g in range(group):
        score_stage(g, g)

    def pipelined(it, carry):
        cur = (it % 2) * group
        nxt = group - cur
        for g in range(group):
            value_stage(it * group + g, cur + g)
        for g in range(group):
            score_stage((it + 1) * group + g, nxt + g)
        return carry

    lax.fori_loop(0, n_groups - 1, pipelined, 0)
    last = ((n_groups - 1) % 2) * group
    for g in range(group):
        value_stage((n_groups - 1) * group + g, last + g)

    kc = k_ref[seq:seq + n_ctx, :]
    vc = v_ref[seq:seq + n_ctx, :]
    (p_ctx,), den = probabilities(q_ref[seq:seq + n_ctx, :], [(kc, None)])
    o = jnp.dot(p_ctx, vc, preferred_element_type=F32) / den
    o_ref[seq:seq + n_ctx, :] = unstack(o).astype(o_ref.dtype)


def _attention(q, k, v, bias, *, batch, rpb_rows, seq, n_ctx):
    t, att_w = q.shape
    rows = seq // GRID_W
    win = min(NA_ROWS_MAX, rows)
    n_pairs = att_w // LANES
    group = 8
    assert rows % group == 0
    blk = pl.BlockSpec((rpb_rows, LANES), lambda b, p: (b, p))
    return pl.pallas_call(
        functools.partial(_att_kernel, seq=seq, n_ctx=n_ctx, rows=rows, win=win, group=group),
        grid=(batch, n_pairs),
        in_specs=[blk, blk, blk,
                  pl.BlockSpec((None, win, 2 * GRID_W, win * GRID_W), lambda b, p: (p, 0, 0, 0))],
        out_specs=blk,
        out_shape=jax.ShapeDtypeStruct((t, att_w), BF16),
        scratch_shapes=[
            pltpu.VMEM((2 * group, 2 * GRID_W, win * GRID_W), BF16),
            pltpu.VMEM((2 * group, 2 * GRID_W, n_ctx), BF16),
            pltpu.VMEM((2 * group, 2 * GRID_W, LANES), F32),
        ],
        compiler_params=_params("arbitrary", "arbitrary"),
        name="attention",
    )(q, k, v, bias)


def _attention_bias(rpb, rows):
    n_heads = rpb.shape[0]
    win = min(NA_ROWS_MAX, rows)
    col = jnp.arange(GRID_W)
    col_start = jnp.clip(col - NA_COLS // 2, 0, GRID_W - NA_COLS)
    kcol = jnp.arange(GRID_W)
    valid = (kcol[None, :] >= col_start[:, None]) & (kcol[None, :] < col_start[:, None] + NA_COLS)
    col_rel = kcol[None, :] - col[:, None] + NA_COLS - 1
    var = jnp.arange(win)
    rr = jnp.arange(win)
    row_rel = rr[None, :] - var[:, None] + NA_ROWS_MAX - 1
    by_row = rpb.astype(F32)[:, row_rel]
    pick = (col_rel[:, :, None] == jnp.arange(2 * NA_COLS - 1)[None, None, :]).astype(F32)
    by_row = by_row.reshape(n_heads // 2, 2, win, win, 2 * NA_COLS - 1)
    tab = jnp.einsum("pavrj,ckj->pvacrk", by_row, pick, precision=lax.Precision.HIGHEST)
    tab = jnp.where(valid[None, None, None, :, None, :], tab, MASK_VALUE)
    return tab.reshape(n_heads // 2, win, 2 * GRID_W, win * GRID_W)


SSM_CHUNK = 16
SLOT = SSM_GROUP
SLOTS = LANES // SLOT


def _slot_of(g, ti):
    return ((g % SLOTS) + (ti % SLOTS)) % SLOTS


def _ssm_pack_kernel(u_ref, t_ref, bre_ref, bim_ref, yi_ref, sre_ref, sim_ref):
    cb, ch, nb, width = u_ref.shape
    rows = cb * nb
    n_groups = width // SSM_GROUP
    lane = lax.broadcasted_iota(jnp.int32, (1, LANES), 1) // SLOT
    slot_mask = [lane == s for s in range(SLOTS)]
    rolled = [[pltpu.roll(u_ref[:, ti, :, j * LANES:(j + 1) * LANES].reshape(rows, LANES), (ti % SLOTS) * SLOT, axis=1)
               for j in range(width // LANES)] for ti in range(ch)]
    for g in range(n_groups):
        j = g // SLOTS
        tiles = []
        for q in range(ch // SLOTS):
            acc = jnp.zeros((rows, LANES), F32)
            for t8 in range(SLOTS):
                ti = q * SLOTS + t8
                acc = jnp.where(slot_mask[_slot_of(g, ti)], rolled[ti][j], acc)
            tiles.append(acc)
        ug = jnp.concatenate(tiles, axis=1).astype(BF16)
        yi_ref[g] = jnp.dot(ug, t_ref[g], preferred_element_type=F32)
        sre_ref[:, g * LANES:(g + 1) * LANES] = jnp.dot(ug, bre_ref[g], preferred_element_type=F32)
        sim_ref[:, g * LANES:(g + 1) * LANES] = jnp.dot(ug, bim_ref[g], preferred_element_type=F32)


def _ssm_carry_kernel(sre_ref, sim_ref, are_ref, aim_ref, hfre_ref, hfim_ref, hbre_ref, hbim_ref, *, nb, n_lat_ch,
                      n_ctx_ch):
    n_ch = n_lat_ch + n_ctx_ch
    lanes = sre_ref.shape[1]
    fwd = (lax.broadcasted_iota(jnp.int32, (1, lanes), 1) % LANES) < (LANES // 2)
    a_re = jnp.broadcast_to(are_ref[...], (nb, lanes))
    a_im = jnp.broadcast_to(aim_ref[...], (nb, lanes))

    def step(k, carry):
        h_re, h_im = carry
        cf = jnp.where(k < n_ctx_ch, n_lat_ch + k, k - n_ctx_ch)
        cb = n_ch - 1 - k
        rf = pl.ds(pl.multiple_of(cf * nb, nb), nb)
        rb = pl.ds(pl.multiple_of(cb * nb, nb), nb)
        hfre_ref[rf, :] = h_re
        hfim_ref[rf, :] = h_im
        hbre_ref[rb, :] = h_re
        hbim_ref[rb, :] = h_im
        s_re = jnp.where(fwd, sre_ref[rf, :], sre_ref[rb, :])
        s_im = jnp.where(fwd, sim_ref[rf, :], sim_ref[rb, :])
        return a_re * h_re - a_im * h_im + s_re, a_re * h_im + a_im * h_re + s_im

    zero = jnp.zeros((nb, lanes), F32)
    lax.fori_loop(0, n_ch, step, (zero, zero))


def _ssm_unpack_kernel(yi_ref, hfre_ref, hfim_ref, hbre_ref, hbim_ref, co_ref, y_ref):
    cb, ch, nb, width = y_ref.shape
    rows = cb * nb
    n_groups = width // SSM_GROUP
    lane = lax.broadcasted_iota(jnp.int32, (1, LANES), 1) // SLOT
    slot_mask = [lane == s for s in range(SLOTS)]
    packed = []
    for g in range(n_groups):
        sl = slice(g * LANES, (g + 1) * LANES)
        h = jnp.concatenate([hfre_ref[:, sl], hfim_ref[:, sl], hbre_ref[:, sl], hbim_ref[:, sl]], axis=1).astype(BF16)
        packed.append(yi_ref[g] + jnp.dot(h, co_ref[g], preferred_element_type=F32))
    for ti in range(ch):
        q = ti // SLOTS
        for j in range(width // LANES):
            acc = jnp.zeros((rows, LANES), F32)
            for g8 in range(SLOTS):
                g = j * SLOTS + g8
                acc = jnp.where(slot_mask[_slot_of(g, ti)], packed[g][:, q * LANES:(q + 1) * LANES], acc)
            tile = pltpu.roll(acc, (LANES - (ti % SLOTS) * SLOT) % LANES, axis=1)
            y_ref[:, ti, :, j * LANES:(j + 1) * LANES] = tile.reshape(cb, nb, LANES)


def _ssm_mixer(u_tm, mats, *, seq, n_ctx):
    total, nb, width = u_tm.shape
    t_mat, b_re, b_im, c_out, a_re, a_im = mats
    n_groups = width // SSM_GROUP
    n_ch = total // SSM_CHUNK
    cblk = next(c for c in (17, 16, 8, 4, 2, 1) if n_ch % c == 0)
    rows = cblk * nb
    u4 = u_tm.reshape(n_ch, SSM_CHUNK, nb, width)
    state_w = n_groups * LANES
    resident = lambda a: pl.BlockSpec(a.shape, lambda i: (0,) * a.ndim, pipeline_mode=pl.Buffered(1))
    seq_block = pl.BlockSpec((cblk, SSM_CHUNK, nb, width), lambda i: (i, 0, 0, 0))
    packed_block = pl.BlockSpec((n_groups, rows, 2 * LANES), lambda i: (0, i, 0))
    state_block = pl.BlockSpec((rows, state_w), lambda i: (i, 0))
    state_shape = jax.ShapeDtypeStruct((n_ch * nb, state_w), F32)
    y_intra, s_re, s_im = pl.pallas_call(
        _ssm_pack_kernel,
        grid=(n_ch // cblk,),
        in_specs=[seq_block, resident(t_mat), resident(b_re), resident(b_im)],
        out_specs=[packed_block, state_block, state_block],
        out_shape=[jax.ShapeDtypeStruct((n_groups, n_ch * nb, 2 * LANES), F32), state_shape, state_shape],
        compiler_params=_params("arbitrary"),
        name="ssm_pack",
    )(u4, t_mat, b_re, b_im)

    lane_blk = 2 * LANES
    col_block = pl.BlockSpec((n_ch * nb, lane_blk), lambda i: (0, i))
    vec_block = pl.BlockSpec((1, lane_blk), lambda i: (0, i))
    h_states = pl.pallas_call(
        functools.partial(_ssm_carry_kernel, nb=nb, n_lat_ch=seq // SSM_CHUNK, n_ctx_ch=n_ctx // SSM_CHUNK),
        grid=(state_w // lane_blk,),
        in_specs=[col_block, col_block, vec_block, vec_block],
        out_specs=[col_block] * 4,
        out_shape=[state_shape] * 4,
        compiler_params=_params("arbitrary"),
        name="ssm_carry",
    )(s_re, s_im, a_re, a_im)

    y4 = pl.pallas_call(
        _ssm_unpack_kernel,
        grid=(n_ch // cblk,),
        in_specs=[packed_block] + [state_block] * 4 + [resident(c_out)],
        out_specs=seq_block,
        out_shape=jax.ShapeDtypeStruct(u4.shape, F32),
        compiler_params=_params("arbitrary"),
        name="ssm_unpack",
    )(y_intra, *h_states, c_out)
    return y4.reshape(total, nb, width)


def _ssm_matrices(a_re, a_im, log_step, b_re, b_im, c_re, c_im, d_skip):
    hp = lax.Precision.HIGHEST
    _, g, p, h = b_re.shape
    ch = SSM_CHUNK
    a_re = jnp.minimum(a_re.astype(F32), A_RE_MAX)
    a_im = a_im.astype(F32)
    dt = jnp.exp(log_step.astype(F32))[..., None]
    k = jnp.arange(ch + 1, dtype=F32)[:, None, None, None]
    mag = jnp.exp(a_re * dt * k)
    lr = mag * jnp.cos(a_im * dt * k)
    li = mag * jnp.sin(a_im * dt * k)
    lam_re, lam_im = lr[1], li[1]
    den = a_re * a_re + a_im * a_im
    z_re = ((lam_re - 1) * a_re + lam_im * a_im) / den
    z_im = (lam_im * a_re - (lam_re - 1) * a_im) / den
    b_re = b_re.astype(F32)
    b_im = b_im.astype(F32)
    bb_re = z_re[..., None] * b_re - z_im[..., None] * b_im
    bb_im = z_re[..., None] * b_im + z_im[..., None] * b_re
    c_re = c_re.astype(F32)
    c_im = c_im.astype(F32)
    w_re = c_re[None] * lr[:, :, :, None, :] - c_im[None] * li[:, :, :, None, :]
    w_im = c_re[None] * li[:, :, :, None, :] + c_im[None] * lr[:, :, :, None, :]
    kern = (jnp.einsum("dxgop,xgpi->dxgoi", w_re, bb_re, precision=hp)
            - jnp.einsum("dxgop,xgpi->dxgoi", w_im, bb_im, precision=hp))
    lag = np.arange(ch)[None, :] - np.arange(ch)[:, None]
    sel_f = (lag[:, :, None] == np.arange(ch + 1)).astype(np.float32)
    sel_b = (-lag[:, :, None] == np.arange(ch + 1)).astype(np.float32)
    toe = (jnp.einsum("std,dgoi->stgoi", sel_f, kern[:, 0], precision=hp)
           + jnp.einsum("std,dgoi->stgoi", sel_b, kern[:, 1], precision=hp))
    skip = d_skip.astype(F32).reshape(g, h)
    toe = toe + (jnp.eye(ch, dtype=F32)[:, :, None, None, None] * jnp.eye(h, dtype=F32)[None, None, None]
                 * skip[None, None, :, :, None])
    t_nat = toe.transpose(2, 0, 4, 1, 3).reshape(g, ch * h, ch * h)

    cm = lambda ar, ai, br, bi: (ar * br - ai * bi, ar * bi + ai * br)
    mf_re, mf_im = cm(jnp.flip(lr[:ch, 0], 0)[..., None], jnp.flip(li[:ch, 0], 0)[..., None], bb_re[0][None],
                      bb_im[0][None])
    mb_re, mb_im = cm(lr[:ch, 1][..., None], li[:ch, 1][..., None], bb_re[1][None], bb_im[1][None])
    pack_state = lambda f, b: jnp.concatenate([f, b], axis=2).transpose(1, 0, 3, 2).reshape(g, ch * h, 2 * p)
    bs_re = pack_state(mf_re, mb_re)
    bs_im = pack_state(mf_im, mb_im)

    rd = lambda w: w.transpose(1, 3, 0, 2).reshape(g, p, ch * h)
    fwd_pow = lambda w: rd(w[1:ch + 1, 0])
    bwd_pow = lambda w: rd(jnp.flip(w[1:ch + 1, 1], 0))
    zeros = jnp.zeros((g, p, ch * h), F32)
    co = jnp.concatenate([fwd_pow(w_re), zeros, -fwd_pow(w_im), zeros,
                          zeros, bwd_pow(w_re), zeros, -bwd_pow(w_im)], axis=1)

    def pack_lanes(m, axis):
        x = jnp.moveaxis(m, axis, 1)
        rest = x.shape[2:]
        x = x.reshape((g // SLOTS, SLOTS, ch // SLOTS, SLOTS, h) + rest)
        x = jnp.stack([jnp.roll(x[:, lo], lo, axis=2) for lo in range(SLOTS)], axis=1)
        return jnp.moveaxis(x.reshape((g, ch * h) + rest), 1, axis)

    rows_p = lambda m: pack_lanes(m, 1)
    cols_p = lambda m: pack_lanes(m, 2)
    t_mat = cols_p(rows_p(t_nat)).astype(BF16)
    a16_re = jnp.concatenate([lr[ch, 0], lr[ch, 1]], axis=1).reshape(1, g * 2 * p)
    a16_im = jnp.concatenate([li[ch, 0], li[ch, 1]], axis=1).reshape(1, g * 2 * p)
    return (t_mat, rows_p(bs_re).astype(BF16), rows_p(bs_im).astype(BF16), cols_p(co).astype(BF16), a16_re, a16_im)


def _mix_kernel(y_ref, att_ref, x_ref, mod_ref, gw_ref, gb_ref, wo_ref, o_ref, *, tpb, seq, ctx_row):
    i = pl.program_id(0)
    tm, d = x_ref.shape
    ssm_w = y_ref.shape[1]
    g = jax.nn.gelu(y_ref[...])
    z = jnp.dot(g.astype(BF16), gw_ref[...], preferred_element_type=F32) + gb_ref[...]
    s = g * jax.nn.sigmoid(z)
    o = (jnp.dot(s.astype(BF16), wo_ref[0:ssm_w, :], preferred_element_type=F32)
         + jnp.dot(att_ref[...], wo_ref[ssm_w:, :], preferred_element_type=F32))
    gate = _mod_rows(mod_ref, 2, i, tm, tpb, seq, ctx_row, d)
    o_ref[...] = x_ref[...] + gate * o


def _mix_out(y_ssm, y_att, x_all, mod_l, glu_w, glu_b, w_out, *, tm, tpb, seq, ctx_row):
    t, d = x_all.shape
    ssm_w = y_ssm.shape[1]
    att_w = y_att.shape[1]
    row_block = lambda width: pl.BlockSpec((tm, width), lambda i: (i, 0))
    whole = lambda a: pl.BlockSpec(a.shape, lambda i: (0, 0))
    gb = glu_b.reshape(1, ssm_w)
    return pl.pallas_call(
        functools.partial(_mix_kernel, tpb=tpb, seq=seq, ctx_row=ctx_row),
        grid=(t // tm,),
        in_specs=[row_block(ssm_w), row_block(att_w), row_block(d), whole(mod_l), whole(glu_w), whole(gb),
                  whole(w_out)],
        out_specs=row_block(d),
        out_shape=jax.ShapeDtypeStruct((t, d), F32),
        compiler_params=_params("arbitrary"),
        name="mix_out",
    )(y_ssm, y_att, x_all, mod_l, glu_w, gb, w_out)


def _ffn_kernel(x_ref, g_ref, mod_ref, w1_ref, w3_ref, w2_ref, o_ref, t_s, acc_s, *, tpb, seq, ctx_row):
    i = pl.program_id(0)
    tm, d = x_ref.shape
    nf = w1_ref.shape[0]
    t_s[...] = _norm_mod(x_ref[...], g_ref[...], mod_ref, 3, i, tpb, seq, ctx_row).astype(BF16)
    acc_s[...] = jnp.zeros_like(acc_s)

    def chunk(c, carry):
        t = t_s[...]
        h1 = jnp.dot(t, w1_ref[c], preferred_element_type=F32)
        h3 = jnp.dot(t, w3_ref[c], preferred_element_type=F32)
        a = (jax.nn.silu(h1) * h3).astype(BF16)
        acc_s[...] += jnp.dot(a, w2_ref[c], preferred_element_type=F32)
        return carry

    lax.fori_loop(0, nf, chunk, 0, unroll=2)
    gate = _mod_rows(mod_ref, 5, i, tm, tpb, seq, ctx_row, d)
    o_ref[...] = x_ref[...] + gate * acc_s[...]


def _dense_ffn(x_all, g, mod_l, w1, w3, w2, *, tm, tpb, seq, ctx_row):
    t, d = x_all.shape
    dff = w1.shape[1]
    tf = _chunk(dff, 256)
    nf = dff // tf
    w1c = w1.reshape(d, nf, tf).transpose(1, 0, 2)
    w3c = w3.reshape(d, nf, tf).transpose(1, 0, 2)
    w2c = w2.reshape(nf, tf, d)
    resident = lambda a: pl.BlockSpec(a.shape, lambda i: (0,) * a.ndim, pipeline_mode=pl.Buffered(1))
    return pl.pallas_call(
        functools.partial(_ffn_kernel, tpb=tpb, seq=seq, ctx_row=ctx_row),
        grid=(t // tm,),
        in_specs=[
            pl.BlockSpec((tm, d), lambda i: (i, 0)),
            pl.BlockSpec((1, d), lambda i: (0, 0)),
            pl.BlockSpec(mod_l.shape, lambda i: (0, 0)),
            resident(w1c), resident(w3c), resident(w2c),
        ],
        out_specs=pl.BlockSpec((tm, d), lambda i: (i, 0)),
        out_shape=jax.ShapeDtypeStruct((t, d), F32),
        scratch_shapes=[pltpu.VMEM((tm, d), BF16), pltpu.VMEM((tm, d), F32)],
        compiler_params=_params("arbitrary"),
        name="dense_ffn",
    )(x_all, g.reshape(1, d), mod_l, w1c, w3c, w2c)


def _route_kernel(x_ref, g_ref, mod_ref, r_ref, t_ref, route_ref, *, tpb, seq, ctx_row):
    i = pl.program_id(0)
    t = _norm_mod(x_ref[...], g_ref[...], mod_ref, 3, i, tpb, seq, ctx_row)
    t_ref[...] = t
    th = t.astype(BF16)
    tl = (t - th.astype(F32)).astype(BF16)
    r = r_ref[...]
    rh = r.astype(BF16)
    rl = (r - rh.astype(F32)).astype(BF16)
    dot = lambda a, b: jnp.dot(a, b, preferred_element_type=F32)
    logits = dot(th, rh) + (dot(th, rl) + dot(tl, rh)) + dot(tl, rl)
    tm = logits.shape[0]
    lane = lax.broadcasted_iota(jnp.int32, (tm, LANES), 1)
    lane_f = lane.astype(F32)
    neg_inf = jnp.float32(-jnp.inf)
    lg = jnp.where(lane < N_EXPERTS, logits, neg_inf)
    m1 = jnp.max(lg, axis=-1, keepdims=True)
    i1 = jnp.min(jnp.where(lg == m1, lane_f, float(LANES)), axis=-1, keepdims=True)
    lg2 = jnp.where(lane_f == i1, neg_inf, lg)
    m2 = jnp.max(lg2, axis=-1, keepdims=True)
    i2 = jnp.min(jnp.where(lg2 == m2, lane_f, float(LANES)), axis=-1, keepdims=True)
    e = jnp.exp(m2 - m1)
    g1 = 1.0 / (1.0 + e)
    g2 = e / (1.0 + e)
    route_ref[...] = jnp.where(lane == 0, i1, jnp.where(lane == 1, i2, jnp.where(lane == 2, g1, jnp.where(
        lane == 3, g2, 0.0))))


def _route(x_all, g, mod_l, router, *, tm, tpb, seq, ctx_row):
    t, d = x_all.shape
    r_pad = jnp.zeros((d, LANES), F32).at[:, :router.shape[1]].set(router.astype(F32))
    return pl.pallas_call(
        functools.partial(_route_kernel, tpb=tpb, seq=seq, ctx_row=ctx_row),
        grid=(t // tm,),
        in_specs=[
            pl.BlockSpec((tm, d), lambda i: (i, 0)),
            pl.BlockSpec((1, d), lambda i: (0, 0)),
            pl.BlockSpec(mod_l.shape, lambda i: (0, 0)),
            pl.BlockSpec((d, LANES), lambda i: (0, 0)),
        ],
        out_specs=[pl.BlockSpec((tm, d), lambda i: (i, 0)), pl.BlockSpec((tm, LANES), lambda i: (i, 0))],
        out_shape=[jax.ShapeDtypeStruct((t, d), F32), jax.ShapeDtypeStruct((t, LANES), F32)],
        compiler_params=_params("arbitrary"),
        name="moe_route",
    )(x_all, g.reshape(1, d), mod_l, r_pad)


def _row_copy(src_hbm, dst_vmem, sem, src_row, dst_row):
    return pltpu.make_async_copy(src_hbm.at[pl.ds(src_row, 1), :], dst_vmem.at[pl.ds(dst_row, 1), :], sem)


ISSUE_UNROLL = 8


ZERO_BLOCK_ROWS = 256


def _dispatch_kernel(fill_lo_ref, fill_hi_ref, pos_ref, t_ref, xs_hbm, zero_s, sems):
    i = pl.program_id(0)
    tm = t_ref.shape[0]
    per_trip = ISSUE_UNROLL // 2

    def issue(g, carry):
        for k in range(per_trip):
            r = g * per_trip + k
            for choice in range(2):
                pltpu.make_async_copy(t_ref.at[pl.ds(r, 1), :], xs_hbm.at[pl.ds(pos_ref[0, 0, 2 * r + choice], 1), :],
                                      sems.at[choice]).start()
        return carry

    lax.fori_loop(0, tm // per_trip, issue, 0)

    @pl.when(i == 0)
    def _():
        zero_s[...] = jnp.zeros_like(zero_s)
        zero_row = lambda row: pltpu.make_async_copy(zero_s.at[pl.ds(0, 1), :], xs_hbm.at[pl.ds(row, 1), :],
                                                     sems.at[2])
        zero_block = lambda blk: pltpu.make_async_copy(
            zero_s, xs_hbm.at[pl.ds(pl.multiple_of(blk * ZERO_BLOCK_ROWS, ZERO_BLOCK_ROWS), ZERO_BLOCK_ROWS), :],
            sems.at[2])

        def start_then_wait(copy, lo, hi):
            def start(k, carry):
                copy(k).start()
                return carry

            def wait(k, carry):
                copy(k).wait()
                return carry

            lax.fori_loop(lo, hi, start, 0)
            lax.fori_loop(lo, hi, wait, 0)

        for e in range(N_EXPERTS):
            start_then_wait(zero_row, fill_lo_ref[e], fill_hi_ref[e])
        start_then_wait(zero_block, fill_lo_ref[N_EXPERTS] // ZERO_BLOCK_ROWS,
                        fill_hi_ref[N_EXPERTS] // ZERO_BLOCK_ROWS)

    for choice in range(2):
        pltpu.make_async_copy(t_ref, xs_hbm.at[pl.ds(0, tm), :], sems.at[choice]).wait()


def _dispatch_rows(tokens, pos, fill_lo, fill_hi, *, tm, n_rows):
    t, d = tokens.shape
    steps = t // tm
    assert tm % (ISSUE_UNROLL // 2) == 0 and n_rows % ZERO_BLOCK_ROWS == 0
    grid_spec = pltpu.PrefetchScalarGridSpec(
        num_scalar_prefetch=2,
        grid=(steps,),
        in_specs=[
            pl.BlockSpec((1, 1, 2 * tm), lambda i, lo, hi: (i, 0, 0), memory_space=pltpu.SMEM),
            pl.BlockSpec((tm, d), lambda i, lo, hi: (i, 0)),
        ],
        out_specs=pl.BlockSpec(memory_space=pl.ANY),
        scratch_shapes=[pltpu.VMEM((ZERO_BLOCK_ROWS, d), tokens.dtype), pltpu.SemaphoreType.DMA((3,))],
    )
    return pl.pallas_call(
        _dispatch_kernel,
        grid_spec=grid_spec,
        out_shape=jax.ShapeDtypeStruct((n_rows, d), tokens.dtype),
        compiler_params=_params("arbitrary"),
        name="moe_dispatch",
    )(fill_lo, fill_hi, pos.reshape(steps, 1, 2 * tm), tokens)


def _gmm_kernel(te_ref, tv_ref, ts_ref, xs_ref, w1_ref, w3_ref, w2_ref, o_ref, xb_s, acc_s, *, nf, n_sub):
    j = pl.program_id(0)
    f = pl.program_id(1)
    valid = tv_ref[j] > 0

    @pl.when(valid)
    def _():
        @pl.when(f == 0)
        def _():
            xb_s[...] = xs_ref[...].astype(BF16)
            acc_s[...] = jnp.zeros_like(acc_s)

        x = xb_s[...]
        tf = w1_ref.shape[1]
        sub = tf // n_sub
        part = None
        for c0 in range(0, tf, sub):
            h1 = jnp.dot(x, w1_ref[:, c0:c0 + sub].astype(BF16), preferred_element_type=F32)
            h3 = jnp.dot(x, w3_ref[:, c0:c0 + sub].astype(BF16), preferred_element_type=F32)
            a = (jax.nn.silu(h1) * h3).astype(BF16)
            y = jnp.dot(a, w2_ref[c0:c0 + sub, :].astype(BF16), preferred_element_type=F32)
            part = y if part is None else part + y
        acc_s[...] += part

        @pl.when(f == nf - 1)
        def _():
            o_ref[...] = acc_s[...]

    @pl.when(jnp.logical_and(jnp.logical_not(valid), f == nf - 1))
    def _():
        o_ref[...] = jnp.zeros_like(o_ref)


def _grouped_swiglu(xs, tile_expert, tile_valid, tile_src, w1, w3, w2, *, tme, layer):
    n, d = xs.shape
    dff = w1.shape[3]
    tf = _chunk(dff, 512)
    nf = dff // tf
    grid_spec = pltpu.PrefetchScalarGridSpec(
        num_scalar_prefetch=3,
        grid=(n // tme, nf),
        in_specs=[
            pl.BlockSpec((tme, d), lambda j, f, te, tv, ts: (ts[j], 0)),
            pl.BlockSpec((None, None, d, tf), lambda j, f, te, tv, ts: (layer, te[j], 0, f)),
            pl.BlockSpec((None, None, d, tf), lambda j, f, te, tv, ts: (layer, te[j], 0, f)),
            pl.BlockSpec((None, None, tf, d), lambda j, f, te, tv, ts: (layer, te[j], f, 0)),
        ],
        out_specs=pl.BlockSpec((tme, d), lambda j, f, te, tv, ts: (j, 0)),
        scratch_shapes=[pltpu.VMEM((tme, d), BF16), pltpu.VMEM((tme, d), F32)],
    )
    return pl.pallas_call(
        functools.partial(_gmm_kernel, nf=nf, n_sub=2),
        grid_spec=grid_spec,
        out_shape=jax.ShapeDtypeStruct((n, d), F32),
        compiler_params=_params("arbitrary", "arbitrary"),
        name="moe_experts",
    )(tile_expert, tile_valid, tile_src, xs, w1, w3, w2)


def _combine_kernel(pos_ref, pos_next_ref, ys_hbm, x_ref, route_ref, mod_ref, o_ref, buf, sems, *, tpb, seq,
                    ctx_row):
    i = pl.program_id(0)
    tm, d = x_ref.shape

    def issue_all(pos, slot):
        def issue(g, carry):
            for k in range(ISSUE_UNROLL // 2):
                r = g * (ISSUE_UNROLL // 2) + k
                _row_copy(ys_hbm, buf.at[slot, 0], sems.at[slot, 0], pos[0, 0, 2 * r], r).start()
                _row_copy(ys_hbm, buf.at[slot, 1], sems.at[slot, 1], pos[0, 0, 2 * r + 1], r).start()
            return carry

        lax.fori_loop(0, tm // (ISSUE_UNROLL // 2), issue, 0)

    @pl.when(i == 0)
    def _():
        issue_all(pos_ref, 0)

    @pl.when(i + 1 < pl.num_programs(0))
    def _():
        issue_all(pos_next_ref, (i + 1) % 2)

    slot = i % 2
    pltpu.make_async_copy(ys_hbm.at[pl.ds(0, tm), :], buf.at[slot, 0], sems.at[slot, 0]).wait()
    pltpu.make_async_copy(ys_hbm.at[pl.ds(0, tm), :], buf.at[slot, 1], sems.at[slot, 1]).wait()
    g1 = route_ref[:, 2:3]
    g2 = route_ref[:, 3:4]
    gate = _mod_rows(mod_ref, 5, i, tm, tpb, seq, ctx_row, d)
    o_ref[...] = x_ref[...] + gate * (g1 * buf[slot, 0] + g2 * buf[slot, 1])


def _combine(ys, pos, x_all, route, mod_l, *, tm, tpb, seq, ctx_row):
    t, d = x_all.shape
    steps = t // tm
    assert tm % (ISSUE_UNROLL // 2) == 0
    pos3 = pos.reshape(steps, 1, 2 * tm)
    return pl.pallas_call(
        functools.partial(_combine_kernel, tpb=tpb, seq=seq, ctx_row=ctx_row),
        grid=(steps,),
        in_specs=[
            pl.BlockSpec((1, 1, 2 * tm), lambda i: (i, 0, 0), memory_space=pltpu.SMEM),
            pl.BlockSpec((1, 1, 2 * tm), lambda i: (jnp.minimum(i + 1, steps - 1), 0, 0), memory_space=pltpu.SMEM),
            pl.BlockSpec(memory_space=pl.ANY),
            pl.BlockSpec((tm, d), lambda i: (i, 0)),
            pl.BlockSpec((tm, LANES), lambda i: (i, 0)),
            pl.BlockSpec(mod_l.shape, lambda i: (0, 0)),
        ],
        out_specs=pl.BlockSpec((tm, d), lambda i: (i, 0)),
        out_shape=jax.ShapeDtypeStruct((t, d), F32),
        scratch_shapes=[pltpu.VMEM((2, 2, tm, d), F32), pltpu.SemaphoreType.DMA((2, 2))],
        compiler_params=_params("arbitrary"),
        name="moe_combine",
    )(pos3, pos3, ys, x_all, route, mod_l)


def _moe_ffn(x_all, g, mod_l, router, w1, w3, w2, *, tm, tpb, seq, ctx_row, tme, layer):
    t = x_all.shape[0]
    tokens, route = _route(x_all, g, mod_l, router, tm=tm, tpb=tpb, seq=seq, ctx_row=ctx_row)
    e_flat = route[:, 0:2].astype(jnp.int32).reshape(-1)
    onehot = (e_flat[:, None] == jnp.arange(N_EXPERTS, dtype=jnp.int32)[None, :]).astype(jnp.int32)
    csum = jnp.cumsum(onehot, axis=0)
    rank = jnp.take_along_axis(csum, e_flat[:, None], axis=1)[:, 0] - 1
    counts = csum[-1]
    padded = ((counts + tme - 1) // tme) * tme
    seg_end = jnp.cumsum(padded)
    seg_start = seg_end - padded
    pos = seg_start[e_flat] + rank
    n_tiles = -(-(2 * t + N_EXPERTS * (tme - 1)) // tme)
    tile_start = jnp.arange(n_tiles, dtype=jnp.int32) * tme
    tile_expert = jnp.minimum(jnp.sum((tile_start[:, None] >= seg_end[None, :]).astype(jnp.int32), axis=1),
                              N_EXPERTS - 1)
    tile_valid = (tile_start < seg_end[-1]).astype(jnp.int32)
    tile_src = jnp.minimum(jnp.arange(n_tiles, dtype=jnp.int32), seg_end[-1] // tme - 1)

    n_rows = n_tiles * tme
    fill_lo = jnp.concatenate([seg_start + counts, seg_end[-1:]]).astype(jnp.int32)
    fill_hi = jnp.concatenate([seg_end, jnp.full((1,), n_rows, jnp.int32)]).astype(jnp.int32)
    xs = _dispatch_rows(tokens, pos, fill_lo, fill_hi, tm=tm, n_rows=n_rows)
    ys = _grouped_swiglu(xs, tile_expert, tile_valid, tile_src, w1, w3, w2, tme=tme, layer=layer)
    return _combine(ys, pos, x_all, route, mod_l, tm=tm, tpb=tpb, seq=seq, ctx_row=ctx_row)


def _final_kernel(x_ref, g_ref, o_ref):
    x = x_ref[...]
    o_ref[...] = x * lax.rsqrt(jnp.mean(x * x, axis=-1, keepdims=True) + NORM_EPS) * g_ref[...]


def _final_norm(x3, g, *, seq):
    batch, _, d = x3.shape
    tr = _chunk(seq, 1024)
    return pl.pallas_call(
        _final_kernel,
        grid=(batch, seq // tr),
        in_specs=[pl.BlockSpec((None, tr, d), lambda b, j: (b, j, 0)), pl.BlockSpec((1, d), lambda b, j: (0, 0))],
        out_specs=pl.BlockSpec((None, tr, d), lambda b, j: (b, j, 0)),
        out_shape=jax.ShapeDtypeStruct((batch, seq, d), F32),
        compiler_params=_params("arbitrary", "arbitrary"),
        name="final_norm",
    )(x3, g.reshape(1, d))


def kernel(x, c, ctx, c_ctx, w_mod, b_mod, g_mix, g_ffn, w_in, w_out, ssm_a_re, ssm_a_im, ssm_log_step, ssm_b_re, ssm_b_im, ssm_c_re, ssm_c_im, ssm_d, glu_w, glu_b, na_rpb, ffn_w1, ffn_w3, ffn_w2, moe_router, moe_w1, moe_w3, moe_w2, g_final):
    batch, seq, d = x.shape
    n_ctx = ctx.shape[1]
    depth = w_mod.shape[0]
    ssm_w = ssm_d.shape[1]
    att_w = (w_in.shape[2] - ssm_w) // 3
    rpb_rows = seq + n_ctx
    assert batch == SUBLANES and batch < MOD_ROWS
    assert seq % GRID_W == 0 and att_w % LANES == 0 and seq % SSM_CHUNK == 0 and n_ctx % SSM_CHUNK == 0
    tm = _token_tile(rpb_rows)
    tpb = rpb_rows // tm
    tme = 1024
    common = dict(tm=tm, tpb=tpb, seq=seq, ctx_row=batch)

    cvec = jnp.zeros((MOD_ROWS, d), F32).at[:batch].set(c.astype(F32)).at[batch].set(c_ctx.astype(F32))
    mod = _mod_table(cvec, w_mod.astype(F32), b_mod.astype(F32))
    x_all = jnp.concatenate([x, ctx], axis=1).astype(F32).reshape(batch * rpb_rows, d)


    for l in range(depth):
        mod_l = mod[l]
        u, q, k, v = _in_proj(x_all, g_mix[l].astype(F32), mod_l, w_in[l].astype(BF16), ssm_w=ssm_w, att_w=att_w,
                              **common)

        u_tm = u.reshape(batch, rpb_rows, ssm_w).transpose(1, 0, 2)
        mats = _ssm_matrices(ssm_a_re[l], ssm_a_im[l], ssm_log_step[l], ssm_b_re[l], ssm_b_im[l], ssm_c_re[l],
                             ssm_c_im[l], ssm_d[l])
        y_tm = _ssm_mixer(u_tm, mats, seq=seq, n_ctx=n_ctx)
        y_ssm = y_tm.transpose(1, 0, 2).reshape(batch * rpb_rows, ssm_w)

        bias = _attention_bias(na_rpb[l], seq // GRID_W)
        y_att = _attention(q, k, v, bias, batch=batch, rpb_rows=rpb_rows, seq=seq, n_ctx=n_ctx)

        x_all = _mix_out(y_ssm, y_att, x_all, mod_l, glu_w[l].astype(BF16), glu_b[l].astype(F32),
                         w_out[l].astype(BF16), **common)

        if l % 2 == 0:
            x_all = _dense_ffn(x_all, g_ffn[l].astype(F32), mod_l, ffn_w1[l // 2].astype(BF16),
                               ffn_w3[l // 2].astype(BF16), ffn_w2[l // 2].astype(BF16), **common)
        else:
            x_all = _moe_ffn(x_all, g_ffn[l].astype(F32), mod_l, moe_router[l // 2], moe_w1, moe_w3, moe_w2,
                             tme=tme, layer=l // 2, **common)

    return _final_norm(x_all.reshape(batch, rpb_rows, d), g_final.astype(F32), seq=seq).astype(x.dtype)
```

```python
import functools
import math

import jax
import jax.numpy as jnp
import numpy as np
from jax import lax
from jax.experimental import pallas as pl
from jax.experimental.pallas import tpu as pltpu

F32 = jnp.float32
BF16 = jnp.bfloat16

GRID_W = 64
SSM_GROUP = 16
SSM_STATE = 64
HEAD_DIM = 64
NA_ROWS_MAX = 8
NA_COLS = 16
N_EXPERTS = 8
NORM_EPS = 1e-6
A_RE_MAX = -1e-4
MASK_VALUE = -1e30

LANES = 128
SUBLANES = 8
V7X_VMEM_LIMIT_BYTES = 56 * 1024 * 1024

MOD_ROWS = 16


def _params(*semantics):
    return pltpu.CompilerParams(dimension_semantics=semantics, vmem_limit_bytes=V7X_VMEM_LIMIT_BYTES)


def _token_tile(rows_per_batch):
    for parts in range(1, rows_per_batch + 1):
        if rows_per_batch % parts == 0:
            tm = rows_per_batch // parts
            if tm <= 1152 and tm % 16 == 0:
                return tm
    raise ValueError("no token tile for %d rows" % rows_per_batch)


def _chunk(total, target):
    best = None
    for c in range(LANES, min(total, target) + 1, LANES):
        if total % c == 0:
            best = c
    if best is None:
        raise ValueError("no lane-aligned chunk for %d" % total)
    return best


def _mod_kernel(c_ref, w_ref, b_ref, o_ref):
    a = jax.nn.silu(c_ref[...])
    o_ref[...] = jnp.dot(a, w_ref[...], preferred_element_type=F32) + b_ref[...]


def _mod_table(cvec, w_mod, b_mod):
    depth, d, n = w_mod.shape
    tn = _chunk(n, 1536)
    return pl.pallas_call(
        _mod_kernel,
        grid=(depth, n // tn),
        in_specs=[
            pl.BlockSpec((MOD_ROWS, d), lambda l, j: (0, 0)),
            pl.BlockSpec((None, d, tn), lambda l, j: (l, 0, j)),
            pl.BlockSpec((None, 1, tn), lambda l, j: (l, 0, j)),
        ],
        out_specs=pl.BlockSpec((None, MOD_ROWS, tn), lambda l, j: (l, 0, j)),
        out_shape=jax.ShapeDtypeStruct((depth, MOD_ROWS, n), F32),
        compiler_params=_params("arbitrary", "arbitrary"),
        name="mod_table",
    )(cvec, w_mod, b_mod.reshape(depth, 1, n))


def _mod_rows(mod_ref, slab, tile_idx, tm, tpb, seq, ctx_row, d):
    b = tile_idx // tpb
    v_b = mod_ref[pl.ds(b, 1), slab * d:(slab + 1) * d]
    v_c = mod_ref[ctx_row:ctx_row + 1, slab * d:(slab + 1) * d]
    row = (tile_idx % tpb) * tm + lax.broadcasted_iota(jnp.int32, (tm, 1), 0)
    return jnp.where(row >= seq, v_c, v_b)


def _norm_mod(x, g, mod_ref, slab, tile_idx, tpb, seq, ctx_row):
    tm, d = x.shape
    shift = _mod_rows(mod_ref, slab, tile_idx, tm, tpb, seq, ctx_row, d)
    scale = _mod_rows(mod_ref, slab + 1, tile_idx, tm, tpb, seq, ctx_row, d)
    rs = lax.rsqrt(jnp.mean(x * x, axis=-1, keepdims=True) + NORM_EPS)
    return (x * rs) * g * (1.0 + scale) + shift


def _in_kernel(x_ref, g_ref, mod_ref, w_ref, u_ref, q_ref, k_ref, v_ref, *, tpb, seq, ctx_row, ssm_w, att_w):
    i = pl.program_id(0)
    h = _norm_mod(x_ref[...], g_ref[...], mod_ref, 0, i, tpb, seq, ctx_row).astype(BF16)
    c1 = ssm_w + att_w
    c2 = c1 + att_w
    u_ref[...] = jnp.dot(h, w_ref[:, 0:ssm_w], preferred_element_type=F32)
    q_ref[...] = (jnp.dot(h, w_ref[:, ssm_w:c1], preferred_element_type=F32) * (HEAD_DIM ** -0.5)).astype(BF16)
    k_ref[...] = jnp.dot(h, w_ref[:, c1:c2], preferred_element_type=F32).astype(BF16)
    v_ref[...] = jnp.dot(h, w_ref[:, c2:c2 + att_w], preferred_element_type=F32).astype(BF16)


def _in_proj(x_all, g, mod_l, w_in, *, tm, tpb, seq, ctx_row, ssm_w, att_w):
    t, d = x_all.shape
    n = w_in.shape[1]
    row_block = lambda width: pl.BlockSpec((tm, width), lambda i: (i, 0))
    return pl.pallas_call(
        functools.partial(_in_kernel, tpb=tpb, seq=seq, ctx_row=ctx_row, ssm_w=ssm_w, att_w=att_w),
        grid=(t // tm,),
        in_specs=[
            row_block(d),
            pl.BlockSpec((1, d), lambda i: (0, 0)),
            pl.BlockSpec(mod_l.shape, lambda i: (0, 0)),
            pl.BlockSpec((d, n), lambda i: (0, 0)),
        ],
        out_specs=[row_block(ssm_w), row_block(att_w), row_block(att_w), row_block(att_w)],
        out_shape=[
            jax.ShapeDtypeStruct((t, ssm_w), F32),
            jax.ShapeDtypeStruct((t, att_w), BF16),
            jax.ShapeDtypeStruct((t, att_w), BF16),
            jax.ShapeDtypeStruct((t, att_w), BF16),
        ],
        compiler_params=_params("arbitrary"),
        name="in_proj",
    )(x_all, g.reshape(1, d), mod_l, w_in)


def _att_kernel(q_ref, k_ref, v_ref, bias_ref, o_ref, plat_s, pctx_s, den_s, *, seq, n_ctx, rows, win, group):
    lane = lax.broadcasted_iota(jnp.int32, (1, LANES), 1)
    first_head = lane < HEAD_DIM
    mask0 = first_head.astype(BF16)
    mask1 = 1 - mask0
    trans_b = (((1,), (1,)), ((), ()))
    lane_tiles = lambda arrs: [a[:, c:c + LANES] for a in arrs for c in range(0, a.shape[1], LANES)]

    def stack(q):
        return jnp.concatenate([q * mask0, q * mask1], axis=0)

    def unstack(o):
        m_rows = o.shape[0] // 2
        return jnp.where(first_head, o[:m_rows], o[m_rows:])

    def probabilities(q, parts):
        q2 = stack(q)
        scores = []
        for keys, bias in parts:
            s = lax.dot_general(q2, keys, trans_b, preferred_element_type=F32)
            scores.append(s if bias is None else s + bias)
        m = jnp.max(functools.reduce(jnp.maximum, lane_tiles(scores)), axis=-1, keepdims=True)
        probs = [jnp.exp(s - m) for s in scores]
        den = jnp.sum(functools.reduce(lambda a, b: a + b, lane_tiles(probs)), axis=-1, keepdims=True)
        return [p.astype(BF16) for p in probs], den

    def window_start(r):
        return pl.multiple_of(jnp.clip(r - win // 2, 0, rows - win) * GRID_W, GRID_W)

    def score_stage(r, slot):
        q0 = pl.multiple_of(r * GRID_W, GRID_W)
        r0 = jnp.clip(r - win // 2, 0, rows - win)
        kw = k_ref[pl.ds(window_start(r), win * GRID_W), :]
        kc = k_ref[seq:seq + n_ctx, :]
        (p_lat, p_ctx), den = probabilities(q_ref[pl.ds(q0, GRID_W), :], [(kw, bias_ref[r - r0]), (kc, None)])
        plat_s[slot] = p_lat
        pctx_s[slot] = p_ctx
        den_s[slot] = jnp.broadcast_to(den, den_s.shape[1:])

    def value_stage(r, slot):
        q0 = pl.multiple_of(r * GRID_W, GRID_W)
        vw = v_ref[pl.ds(window_start(r), win * GRID_W), :]
        vc = v_ref[seq:seq + n_ctx, :]
        o = (jnp.dot(plat_s[slot], vw, preferred_element_type=F32)
             + jnp.dot(pctx_s[slot], vc, preferred_element_type=F32)) / den_s[slot]
        o_ref[pl.ds(q0, GRID_W), :] = unstack(o).astype(o_ref.dtype)

    n_groups = rows // group
    for g in range(group):
        score_stage(g, g)

    def pipelined(it, carry):
        cur = (it % 2) * group
        nxt = group - cur
        for g in range(group):
            value_stage(it * group + g, cur + g)
        for g in range(group):
            score_stage((it + 1) * group + g, nxt + g)
        return carry

    lax.fori_loop(0, n_groups - 1, pipelined, 0)
    last = ((n_groups - 1) % 2) * group
    for g in range(group):
        value_stage((n_groups - 1) * group + g, last + g)

    kc = k_ref[seq:seq + n_ctx, :]
    vc = v_ref[seq:seq + n_ctx, :]
    (p_ctx,), den = probabilities(q_ref[seq:seq + n_ctx, :], [(kc, None)])
    o = jnp.dot(p_ctx, vc, preferred_element_type=F32) / den
    o_ref[seq:seq + n_ctx, :] = unstack(o).astype(o_ref.dtype)


def _attention(q, k, v, bias, *, batch, rpb_rows, seq, n_ctx):
    t, att_w = q.shape
    rows = seq // GRID_W
    win = min(NA_ROWS_MAX, rows)
    n_pairs = att_w // LANES
    group = 8
    assert rows % group == 0
    blk = pl.BlockSpec((rpb_rows, LANES), lambda b, p: (b, p))
    return pl.pallas_call(
        functools.partial(_att_kernel, seq=seq, n_ctx=n_ctx, rows=rows, win=win, group=group),
        grid=(batch, n_pairs),
        in_specs=[blk, blk, blk,
                  pl.BlockSpec((None, win, 2 * GRID_W, win * GRID_W), lambda b, p: (p, 0, 0, 0))],
        out_specs=blk,
        out_shape=jax.ShapeDtypeStruct((t, att_w), BF16),
        scratch_shapes=[
            pltpu.VMEM((2 * group, 2 * GRID_W, win * GRID_W), BF16),
            pltpu.VMEM((2 * group, 2 * GRID_W, n_ctx), BF16),
            pltpu.VMEM((2 * group, 2 * GRID_W, LANES), F32),
        ],
        compiler_params=_params("arbitrary", "arbitrary"),
        name="attention",
    )(q, k, v, bias)


def _attention_bias(rpb, rows):
    n_heads = rpb.shape[0]
    win = min(NA_ROWS_MAX, rows)
    col = jnp.arange(GRID_W)
    col_start = jnp.clip(col - NA_COLS // 2, 0, GRID_W - NA_COLS)
    kcol = jnp.arange(GRID_W)
    valid = (kcol[None, :] >= col_start[:, None]) & (kcol[None, :] < col_start[:, None] + NA_COLS)
    col_rel = kcol[None, :] - col[:, None] + NA_COLS - 1
    var = jnp.arange(win)
    rr = jnp.arange(win)
    row_rel = rr[None, :] - var[:, None] + NA_ROWS_MAX - 1
    by_row = rpb.astype(F32)[:, row_rel]
    pick = (col_rel[:, :, None] == jnp.arange(2 * NA_COLS - 1)[None, None, :]).astype(F32)
    by_row = by_row.reshape(n_heads // 2, 2, win, win, 2 * NA_COLS - 1)
    tab = jnp.einsum("pavrj,ckj->pvacrk", by_row, pick, precision=lax.Precision.HIGHEST)
    tab = jnp.where(valid[None, None, None, :, None, :], tab, MASK_VALUE)
    return tab.reshape(n_heads // 2, win, 2 * GRID_W, win * GRID_W)


SSM_CHUNK = 16
SLOT = SSM_GROUP
SLOTS = LANES // SLOT


def _slot_of(g, ti):
    return ((g % SLOTS) + (ti % SLOTS)) % SLOTS


def _ssm_pack_kernel(u_ref, t_ref, bre_ref, bim_ref, yi_ref, sre_ref, sim_ref):
    cb, ch, nb, width = u_ref.shape
    rows = cb * nb
    n_groups = width // SSM_GROUP
    lane = lax.broadcasted_iota(jnp.int32, (1, LANES), 1) // SLOT
    slot_mask = [lane == s for s in range(SLOTS)]
    rolled = [[pltpu.roll(u_ref[:, ti, :, j * LANES:(j + 1) * LANES].reshape(rows, LANES), (ti % SLOTS) * SLOT, axis=1)
               for j in range(width // LANES)] for ti in range(ch)]
    for g in range(n_groups):
        j = g // SLOTS
        tiles = []
        for q in range(ch // SLOTS):
            acc = jnp.zeros((rows, LANES), F32)
            for t8 in range(SLOTS):
                ti = q * SLOTS + t8
                acc = jnp.where(slot_mask[_slot_of(g, ti)], rolled[ti][j], acc)
            tiles.append(acc)
        ug = jnp.concatenate(tiles, axis=1).astype(BF16)
        yi_ref[g] = jnp.dot(ug, t_ref[g], preferred_element_type=F32)
        sre_ref[:, g * LANES:(g + 1) * LANES] = jnp.dot(ug, bre_ref[g], preferred_element_type=F32)
        sim_ref[:, g * LANES:(g + 1) * LANES] = jnp.dot(ug, bim_ref[g], preferred_element_type=F32)


def _ssm_carry_kernel(sre_ref, sim_ref, are_ref, aim_ref, hfre_ref, hfim_ref, hbre_ref, hbim_ref, *, nb, n_lat_ch,
                      n_ctx_ch):
    n_ch = n_lat_ch + n_ctx_ch
    lanes = sre_ref.shape[1]
    fwd = (lax.broadcasted_iota(jnp.int32, (1, lanes), 1) % LANES) < (LANES // 2)
    a_re = jnp.broadcast_to(are_ref[...], (nb, lanes))
    a_im = jnp.broadcast_to(aim_ref[...], (nb, lanes))

    def step(k, carry):
        h_re, h_im = carry
        cf = jnp.where(k < n_ctx_ch, n_lat_ch + k, k - n_ctx_ch)
        cb = n_ch - 1 - k
        rf = pl.ds(pl.multiple_of(cf * nb, nb), nb)
        rb = pl.ds(pl.multiple_of(cb * nb, nb), nb)
        hfre_ref[rf, :] = h_re
        hfim_ref[rf, :] = h_im
        hbre_ref[rb, :] = h_re
        hbim_ref[rb, :] = h_im
        s_re = jnp.where(fwd, sre_ref[rf, :], sre_ref[rb, :])
        s_im = jnp.where(fwd, sim_ref[rf, :], sim_ref[rb, :])
        return a_re * h_re - a_im * h_im + s_re, a_re * h_im + a_im * h_re + s_im

    zero = jnp.zeros((nb, lanes), F32)
    lax.fori_loop(0, n_ch, step, (zero, zero))


def _ssm_unpack_kernel(yi_ref, hfre_ref, hfim_ref, hbre_ref, hbim_ref, co_ref, y_ref):
    cb, ch, nb, width = y_ref.shape
    rows = cb * nb
    n_groups = width // SSM_GROUP
    lane = lax.broadcasted_iota(jnp.int32, (1, LANES), 1) // SLOT
    slot_mask = [lane == s for s in range(SLOTS)]
    packed = []
    for g in range(n_groups):
        sl = slice(g * LANES, (g + 1) * LANES)
        h = jnp.concatenate([hfre_ref[:, sl], hfim_ref[:, sl], hbre_ref[:, sl], hbim_ref[:, sl]], axis=1).astype(BF16)
        packed.append(yi_ref[g] + jnp.dot(h, co_ref[g], preferred_element_type=F32))
    for ti in range(ch):
        q = ti // SLOTS
        for j in range(width // LANES):
            acc = jnp.zeros((rows, LANES), F32)
            for g8 in range(SLOTS):
                g = j * SLOTS + g8
                acc = jnp.where(slot_mask[_slot_of(g, ti)], packed[g][:, q * LANES:(q + 1) * LANES], acc)
            tile = pltpu.roll(acc, (LANES - (ti % SLOTS) * SLOT) % LANES, axis=1)
            y_ref[:, ti, :, j * LANES:(j + 1) * LANES] = tile.reshape(cb, nb, LANES)


def _ssm_mixer(u_tm, mats, *, seq, n_ctx):
    total, nb, width = u_tm.shape
    t_mat, b_re, b_im, c_out, a_re, a_im = mats
    n_groups = width // SSM_GROUP
    n_ch = total // SSM_CHUNK
    cblk = next(c for c in (17, 16, 8, 4, 2, 1) if n_ch % c == 0)
    rows = cblk * nb
    u4 = u_tm.reshape(n_ch, SSM_CHUNK, nb, width)
    state_w = n_groups * LANES
    resident = lambda a: pl.BlockSpec(a.shape, lambda i: (0,) * a.ndim, pipeline_mode=pl.Buffered(1))
    seq_block = pl.BlockSpec((cblk, SSM_CHUNK, nb, width), lambda i: (i, 0, 0, 0))
    packed_block = pl.BlockSpec((n_groups, rows, 2 * LANES), lambda i: (0, i, 0))
    state_block = pl.BlockSpec((rows, state_w), lambda i: (i, 0))
    state_shape = jax.ShapeDtypeStruct((n_ch * nb, state_w), F32)
    y_intra, s_re, s_im = pl.pallas_call(
        _ssm_pack_kernel,
        grid=(n_ch // cblk,),
        in_specs=[seq_block, resident(t_mat), resident(b_re), resident(b_im)],
        out_specs=[packed_block, state_block, state_block],
        out_shape=[jax.ShapeDtypeStruct((n_groups, n_ch * nb, 2 * LANES), F32), state_shape, state_shape],
        compiler_params=_params("arbitrary"),
        name="ssm_pack",
    )(u4, t_mat, b_re, b_im)

    lane_blk = 2 * LANES
    col_block = pl.BlockSpec((n_ch * nb, lane_blk), lambda i: (0, i))
    vec_block = pl.BlockSpec((1, lane_blk), lambda i: (0, i))
    h_states = pl.pallas_call(
        functools.partial(_ssm_carry_kernel, nb=nb, n_lat_ch=seq // SSM_CHUNK, n_ctx_ch=n_ctx // SSM_CHUNK),
        grid=(state_w // lane_blk,),
        in_specs=[col_block, col_block, vec_block, vec_block],
        out_specs=[col_block] * 4,
        out_shape=[state_shape] * 4,
        compiler_params=_params("arbitrary"),
        name="ssm_carry",
    )(s_re, s_im, a_re, a_im)

    y4 = pl.pallas_call(
        _ssm_unpack_kernel,
        grid=(n_ch // cblk,),
        in_specs=[packed_block] + [state_block] * 4 + [resident(c_out)],
        out_specs=seq_block,
        out_shape=jax.ShapeDtypeStruct(u4.shape, F32),
        compiler_params=_params("arbitrary"),
        name="ssm_unpack",
    )(y_intra, *h_states, c_out)
    return y4.reshape(total, nb, width)


def _ssm_matrices(a_re, a_im, log_step, b_re, b_im, c_re, c_im, d_skip):
    hp = lax.Precision.HIGHEST
    _, g, p, h = b_re.shape
    ch = SSM_CHUNK
    a_re = jnp.minimum(a_re.astype(F32), A_RE_MAX)
    a_im = a_im.astype(F32)
    dt = jnp.exp(log_step.astype(F32))[..., None]
    k = jnp.arange(ch + 1, dtype=F32)[:, None, None, None]
    mag = jnp.exp(a_re * dt * k)
    lr = mag * jnp.cos(a_im * dt * k)
    li = mag * jnp.sin(a_im * dt * k)
    lam_re, lam_im = lr[1], li[1]
    den = a_re * a_re + a_im * a_im
    z_re = ((lam_re - 1) * a_re + lam_im * a_im) / den
    z_im = (lam_im * a_re - (lam_re - 1) * a_im) / den
    b_re = b_re.astype(F32)
    b_im = b_im.astype(F32)
    bb_re = z_re[..., None] * b_re - z_im[..., None] * b_im
    bb_im = z_re[..., None] * b_im + z_im[..., None] * b_re
    c_re = c_re.astype(F32)
    c_im = c_im.astype(F32)
    w_re = c_re[None] * lr[:, :, :, None, :] - c_im[None] * li[:, :, :, None, :]
    w_im = c_re[None] * li[:, :, :, None, :] + c_im[None] * lr[:, :, :, None, :]
    kern = (jnp.einsum("dxgop,xgpi->dxgoi", w_re, bb_re, precision=hp)
            - jnp.einsum("dxgop,xgpi->dxgoi", w_im, bb_im, precision=hp))
    n_tile = g // SLOTS
    u_idx = np.arange(ch)
    nat = np.stack([(u_idx // SLOTS) * SLOTS + ((u_idx % SLOTS) - lo) % SLOTS for lo in range(SLOTS)])
    unpack = (nat[:, :, None] == np.arange(ch)[None, None, :]).astype(np.float32)
    lag = nat[:, None, :] - nat[:, :, None]
    d_idx = np.arange(ch + 1)
    sel_f = (lag[..., None] == d_idx).astype(np.float32)
    sel_b = (-lag[..., None] == d_idx).astype(np.float32)
    by_tile = lambda a: a.reshape(a.shape[:1] + (n_tile, SLOTS) + a.shape[2:])
    toe = (jnp.einsum("lutd,dnloi->nluito", sel_f, by_tile(kern[:, 0]), precision=hp)
           + jnp.einsum("lutd,dnloi->nluito", sel_b, by_tile(kern[:, 1]), precision=hp))
    skip_lane = jnp.tile(d_skip.astype(F32).reshape(g, h), (1, ch))
    t_mat = toe.reshape(g, ch * h, ch * h) + jnp.eye(ch * h, dtype=F32)[None] * skip_lane[:, None, :]

    cm = lambda ar, ai, br, bi: (ar * br - ai * bi, ar * bi + ai * br)
    mf_re, mf_im = cm(jnp.flip(lr[:ch, 0], 0)[..., None], jnp.flip(li[:ch, 0], 0)[..., None], bb_re[0][None],
                      bb_im[0][None])
    mb_re, mb_im = cm(lr[:ch, 1][..., None], li[:ch, 1][..., None], bb_re[1][None], bb_im[1][None])
    pack_state = lambda f, b: jnp.einsum("lus,snlpi->nluip", unpack, by_tile(jnp.concatenate([f, b], axis=2)),
                                         precision=hp).reshape(g, ch * h, 2 * p)
    bs_re = pack_state(mf_re, mb_re)
    bs_im = pack_state(mf_im, mb_im)

    rd = lambda w: jnp.einsum("lut,tnlop->nlpuo", unpack, by_tile(w), precision=hp).reshape(g, p, ch * h)
    fwd_pow = lambda w: rd(w[1:ch + 1, 0])
    bwd_pow = lambda w: rd(jnp.flip(w[1:ch + 1, 1], 0))
    zeros = jnp.zeros((g, p, ch * h), F32)
    co = jnp.concatenate([fwd_pow(w_re), zeros, -fwd_pow(w_im), zeros,
                          zeros, bwd_pow(w_re), zeros, -bwd_pow(w_im)], axis=1)

    a16_re = jnp.concatenate([lr[ch, 0], lr[ch, 1]], axis=1).reshape(1, g * 2 * p)
    a16_im = jnp.concatenate([li[ch, 0], li[ch, 1]], axis=1).reshape(1, g * 2 * p)
    return t_mat.astype(BF16), bs_re.astype(BF16), bs_im.astype(BF16), co.astype(BF16), a16_re, a16_im


def _mix_kernel(y_ref, att_ref, x_ref, mod_ref, gw_ref, gb_ref, wo_ref, o_ref, *, tpb, seq, ctx_row):
    i = pl.program_id(0)
    tm, d = x_ref.shape
    ssm_w = y_ref.shape[1]
    g = jax.nn.gelu(y_ref[...])
    z = jnp.dot(g.astype(BF16), gw_ref[...], preferred_element_type=F32) + gb_ref[...]
    s = g * jax.nn.sigmoid(z)
    o = (jnp.dot(s.astype(BF16), wo_ref[0:ssm_w, :], preferred_element_type=F32)
         + jnp.dot(att_ref[...], wo_ref[ssm_w:, :], preferred_element_type=F32))
    gate = _mod_rows(mod_ref, 2, i, tm, tpb, seq, ctx_row, d)
    o_ref[...] = x_ref[...] + gate * o


def _mix_out(y_ssm, y_att, x_all, mod_l, glu_w, glu_b, w_out, *, tm, tpb, seq, ctx_row):
    t, d = x_all.shape
    ssm_w = y_ssm.shape[1]
    att_w = y_att.shape[1]
    row_block = lambda width: pl.BlockSpec((tm, width), lambda i: (i, 0))
    whole = lambda a: pl.BlockSpec(a.shape, lambda i: (0, 0))
    gb = glu_b.reshape(1, ssm_w)
    return pl.pallas_call(
        functools.partial(_mix_kernel, tpb=tpb, seq=seq, ctx_row=ctx_row),
        grid=(t // tm,),
        in_specs=[row_block(ssm_w), row_block(att_w), row_block(d), whole(mod_l), whole(glu_w), whole(gb),
                  whole(w_out)],
        out_specs=row_block(d),
        out_shape=jax.ShapeDtypeStruct((t, d), F32),
        compiler_params=_params("arbitrary"),
        name="mix_out",
    )(y_ssm, y_att, x_all, mod_l, glu_w, gb, w_out)


def _ffn_kernel(x_ref, g_ref, mod_ref, w1_ref, w3_ref, w2_ref, o_ref, t_s, acc_s, *, tpb, seq, ctx_row):
    i = pl.program_id(0)
    tm, d = x_ref.shape
    nf = w1_ref.shape[0]
    t_s[...] = _norm_mod(x_ref[...], g_ref[...], mod_ref, 3, i, tpb, seq, ctx_row).astype(BF16)
    acc_s[...] = jnp.zeros_like(acc_s)

    def chunk(c, carry):
        t = t_s[...]
        h1 = jnp.dot(t, w1_ref[c], preferred_element_type=F32)
        h3 = jnp.dot(t, w3_ref[c], preferred_element_type=F32)
        a = (jax.nn.silu(h1) * h3).astype(BF16)
        acc_s[...] += jnp.dot(a, w2_ref[c], preferred_element_type=F32)
        return carry

    lax.fori_loop(0, nf, chunk, 0, unroll=2)
    gate = _mod_rows(mod_ref, 5, i, tm, tpb, seq, ctx_row, d)
    o_ref[...] = x_ref[...] + gate * acc_s[...]


def _dense_ffn(x_all, g, mod_l, w1, w3, w2, *, tm, tpb, seq, ctx_row):
    t, d = x_all.shape
    dff = w1.shape[1]
    tf = _chunk(dff, 256)
    nf = dff // tf
    w1c = w1.reshape(d, nf, tf).transpose(1, 0, 2)
    w3c = w3.reshape(d, nf, tf).transpose(1, 0, 2)
    w2c = w2.reshape(nf, tf, d)
    resident = lambda a: pl.BlockSpec(a.shape, lambda i: (0,) * a.ndim, pipeline_mode=pl.Buffered(1))
    return pl.pallas_call(
        functools.partial(_ffn_kernel, tpb=tpb, seq=seq, ctx_row=ctx_row),
        grid=(t // tm,),
        in_specs=[
            pl.BlockSpec((tm, d), lambda i: (i, 0)),
            pl.BlockSpec((1, d), lambda i: (0, 0)),
            pl.BlockSpec(mod_l.shape, lambda i: (0, 0)),
            resident(w1c), resident(w3c), resident(w2c),
        ],
        out_specs=pl.BlockSpec((tm, d), lambda i: (i, 0)),
        out_shape=jax.ShapeDtypeStruct((t, d), F32),
        scratch_shapes=[pltpu.VMEM((tm, d), BF16), pltpu.VMEM((tm, d), F32)],
        compiler_params=_params("arbitrary"),
        name="dense_ffn",
    )(x_all, g.reshape(1, d), mod_l, w1c, w3c, w2c)


def _route_kernel(x_ref, g_ref, mod_ref, r_ref, t_ref, route_ref, *, tpb, seq, ctx_row):
    i = pl.program_id(0)
    t = _norm_mod(x_ref[...], g_ref[...], mod_ref, 3, i, tpb, seq, ctx_row)
    t_ref[...] = t
    th = t.astype(BF16)
    tl = (t - th.astype(F32)).astype(BF16)
    r = r_ref[...]
    rh = r.astype(BF16)
    rl = (r - rh.astype(F32)).astype(BF16)
    dot = lambda a, b: jnp.dot(a, b, preferred_element_type=F32)
    logits = dot(th, rh) + (dot(th, rl) + dot(tl, rh)) + dot(tl, rl)
    tm = logits.shape[0]
    lane = lax.broadcasted_iota(jnp.int32, (tm, LANES), 1)
    lane_f = lane.astype(F32)
    neg_inf = jnp.float32(-jnp.inf)
    lg = jnp.where(lane < N_EXPERTS, logits, neg_inf)
    m1 = jnp.max(lg, axis=-1, keepdims=True)
    i1 = jnp.min(jnp.where(lg == m1, lane_f, float(LANES)), axis=-1, keepdims=True)
    lg2 = jnp.where(lane_f == i1, neg_inf, lg)
    m2 = jnp.max(lg2, axis=-1, keepdims=True)
    i2 = jnp.min(jnp.where(lg2 == m2, lane_f, float(LANES)), axis=-1, keepdims=True)
    e = jnp.exp(m2 - m1)
    g1 = 1.0 / (1.0 + e)
    g2 = e / (1.0 + e)
    route_ref[...] = jnp.where(lane == 0, i1, jnp.where(lane == 1, i2, jnp.where(lane == 2, g1, jnp.where(
        lane == 3, g2, 0.0))))


def _route(x_all, g, mod_l, router, *, tm, tpb, seq, ctx_row):
    t, d = x_all.shape
    r_pad = jnp.zeros((d, LANES), F32).at[:, :router.shape[1]].set(router.astype(F32))
    return pl.pallas_call(
        functools.partial(_route_kernel, tpb=tpb, seq=seq, ctx_row=ctx_row),
        grid=(t // tm,),
        in_specs=[
            pl.BlockSpec((tm, d), lambda i: (i, 0)),
            pl.BlockSpec((1, d), lambda i: (0, 0)),
            pl.BlockSpec(mod_l.shape, lambda i: (0, 0)),
            pl.BlockSpec((d, LANES), lambda i: (0, 0)),
        ],
        out_specs=[pl.BlockSpec((tm, d), lambda i: (i, 0)), pl.BlockSpec((tm, LANES), lambda i: (i, 0))],
        out_shape=[jax.ShapeDtypeStruct((t, d), F32), jax.ShapeDtypeStruct((t, LANES), F32)],
        compiler_params=_params("arbitrary"),
        name="moe_route",
    )(x_all, g.reshape(1, d), mod_l, r_pad)


def _row_copy(src_hbm, dst_vmem, sem, src_row, dst_row):
    return pltpu.make_async_copy(src_hbm.at[pl.ds(src_row, 1), :], dst_vmem.at[pl.ds(dst_row, 1), :], sem)


ISSUE_UNROLL = 8


ZERO_BLOCK_ROWS = 256


def _dispatch_kernel(fill_lo_ref, fill_hi_ref, pos_ref, t_ref, xs_hbm, zero_s, sems):
    i = pl.program_id(0)
    tm = t_ref.shape[0]
    per_trip = ISSUE_UNROLL // 2

    def issue(g, carry):
        for k in range(per_trip):
            r = g * per_trip + k
            for choice in range(2):
                pltpu.make_async_copy(t_ref.at[pl.ds(r, 1), :], xs_hbm.at[pl.ds(pos_ref[0, 0, 2 * r + choice], 1), :],
                                      sems.at[choice]).start()
        return carry

    lax.fori_loop(0, tm // per_trip, issue, 0)

    @pl.when(i == 0)
    def _():
        zero_s[...] = jnp.zeros_like(zero_s)
        zero_row = lambda row: pltpu.make_async_copy(zero_s.at[pl.ds(0, 1), :], xs_hbm.at[pl.ds(row, 1), :],
                                                     sems.at[2])
        zero_block = lambda blk: pltpu.make_async_copy(
            zero_s, xs_hbm.at[pl.ds(pl.multiple_of(blk * ZERO_BLOCK_ROWS, ZERO_BLOCK_ROWS), ZERO_BLOCK_ROWS), :],
            sems.at[2])

        def start_then_wait(copy, lo, hi):
            def start(k, carry):
                copy(k).start()
                return carry

            def wait(k, carry):
                copy(k).wait()
                return carry

            lax.fori_loop(lo, hi, start, 0)
            lax.fori_loop(lo, hi, wait, 0)

        for e in range(N_EXPERTS):
            start_then_wait(zero_row, fill_lo_ref[e], fill_hi_ref[e])
        start_then_wait(zero_block, fill_lo_ref[N_EXPERTS] // ZERO_BLOCK_ROWS,
                        fill_hi_ref[N_EXPERTS] // ZERO_BLOCK_ROWS)

    for choice in range(2):
        pltpu.make_async_copy(t_ref, xs_hbm.at[pl.ds(0, tm), :], sems.at[choice]).wait()


def _dispatch_rows(tokens, pos, fill_lo, fill_hi, *, tm, n_rows):
    t, d = tokens.shape
    steps = t // tm
    assert tm % (ISSUE_UNROLL // 2) == 0 and n_rows % ZERO_BLOCK_ROWS == 0
    grid_spec = pltpu.PrefetchScalarGridSpec(
        num_scalar_prefetch=2,
        grid=(steps,),
        in_specs=[
            pl.BlockSpec((1, 1, 2 * tm), lambda i, lo, hi: (i, 0, 0), memory_space=pltpu.SMEM),
            pl.BlockSpec((tm, d), lambda i, lo, hi: (i, 0)),
        ],
        out_specs=pl.BlockSpec(memory_space=pl.ANY),
        scratch_shapes=[pltpu.VMEM((ZERO_BLOCK_ROWS, d), tokens.dtype), pltpu.SemaphoreType.DMA((3,))],
    )
    return pl.pallas_call(
        _dispatch_kernel,
        grid_spec=grid_spec,
        out_shape=jax.ShapeDtypeStruct((n_rows, d), tokens.dtype),
        compiler_params=_params("arbitrary"),
        name="moe_dispatch",
    )(fill_lo, fill_hi, pos.reshape(steps, 1, 2 * tm), tokens)


def _gmm_kernel(te_ref, tv_ref, ts_ref, xs_ref, w1_ref, w3_ref, w2_ref, o_ref, xb_s, acc_s, *, nf, n_sub):
    j = pl.program_id(0)
    f = pl.program_id(1)
    valid = tv_ref[j] > 0

    @pl.when(valid)
    def _():
        @pl.when(f == 0)
        def _():
            xb_s[...] = xs_ref[...].astype(BF16)
            acc_s[...] = jnp.zeros_like(acc_s)

        x = xb_s[...]
        tf = w1_ref.shape[1]
        sub = tf // n_sub
        part = None
        for c0 in range(0, tf, sub):
            h1 = jnp.dot(x, w1_ref[:, c0:c0 + sub].astype(BF16), preferred_element_type=F32)
            h3 = jnp.dot(x, w3_ref[:, c0:c0 + sub].astype(BF16), preferred_element_type=F32)
            a = (jax.nn.silu(h1) * h3).astype(BF16)
            y = jnp.dot(a, w2_ref[c0:c0 + sub, :].astype(BF16), preferred_element_type=F32)
            part = y if part is None else part + y
        acc_s[...] += part

        @pl.when(f == nf - 1)
        def _():
            o_ref[...] = acc_s[...]

    @pl.when(jnp.logical_and(jnp.logical_not(valid), f == nf - 1))
    def _():
        o_ref[...] = jnp.zeros_like(o_ref)


def _grouped_swiglu(xs, tile_expert, tile_valid, tile_src, w1, w3, w2, *, tme, layer):
    n, d = xs.shape
    dff = w1.shape[3]
    tf = _chunk(dff, 512)
    nf = dff // tf
    grid_spec = pltpu.PrefetchScalarGridSpec(
        num_scalar_prefetch=3,
        grid=(n // tme, nf),
        in_specs=[
            pl.BlockSpec((tme, d), lambda j, f, te, tv, ts: (ts[j], 0)),
            pl.BlockSpec((None, None, d, tf), lambda j, f, te, tv, ts: (layer, te[j], 0, f)),
            pl.BlockSpec((None, None, d, tf), lambda j, f, te, tv, ts: (layer, te[j], 0, f)),
            pl.BlockSpec((None, None, tf, d), lambda j, f, te, tv, ts: (layer, te[j], f, 0)),
        ],
        out_specs=pl.BlockSpec((tme, d), lambda j, f, te, tv, ts: (j, 0)),
        scratch_shapes=[pltpu.VMEM((tme, d), BF16), pltpu.VMEM((tme, d), F32)],
    )
    return pl.pallas_call(
        functools.partial(_gmm_kernel, nf=nf, n_sub=2),
        grid_spec=grid_spec,
        out_shape=jax.ShapeDtypeStruct((n, d), F32),
        compiler_params=_params("arbitrary", "arbitrary"),
        name="moe_experts",
    )(tile_expert, tile_valid, tile_src, xs, w1, w3, w2)


def _combine_kernel(pos_ref, pos_next_ref, ys_hbm, x_ref, route_ref, mod_ref, o_ref, buf, sems, *, tpb, seq,
                    ctx_row):
    i = pl.program_id(0)
    tm, d = x_ref.shape

    def issue_all(pos, slot):
        def issue(g, carry):
            for k in range(ISSUE_UNROLL // 2):
                r = g * (ISSUE_UNROLL // 2) + k
                _row_copy(ys_hbm, buf.at[slot, 0], sems.at[slot, 0], pos[0, 0, 2 * r], r).start()
                _row_copy(ys_hbm, buf.at[slot, 1], sems.at[slot, 1], pos[0, 0, 2 * r + 1], r).start()
            return carry

        lax.fori_loop(0, tm // (ISSUE_UNROLL // 2), issue, 0)

    @pl.when(i == 0)
    def _():
        issue_all(pos_ref, 0)

    @pl.when(i + 1 < pl.num_programs(0))
    def _():
        issue_all(pos_next_ref, (i + 1) % 2)

    slot = i % 2
    pltpu.make_async_copy(ys_hbm.at[pl.ds(0, tm), :], buf.at[slot, 0], sems.at[slot, 0]).wait()
    pltpu.make_async_copy(ys_hbm.at[pl.ds(0, tm), :], buf.at[slot, 1], sems.at[slot, 1]).wait()
    g1 = route_ref[:, 2:3]
    g2 = route_ref[:, 3:4]
    gate = _mod_rows(mod_ref, 5, i, tm, tpb, seq, ctx_row, d)
    o_ref[...] = x_ref[...] + gate * (g1 * buf[slot, 0] + g2 * buf[slot, 1])


def _combine(ys, pos, x_all, route, mod_l, *, tm, tpb, seq, ctx_row):
    t, d = x_all.shape
    steps = t // tm
    assert tm % (ISSUE_UNROLL // 2) == 0
    pos3 = pos.reshape(steps, 1, 2 * tm)
    return pl.pallas_call(
        functools.partial(_combine_kernel, tpb=tpb, seq=seq, ctx_row=ctx_row),
        grid=(steps,),
        in_specs=[
            pl.BlockSpec((1, 1, 2 * tm), lambda i: (i, 0, 0), memory_space=pltpu.SMEM),
            pl.BlockSpec((1, 1, 2 * tm), lambda i: (jnp.minimum(i + 1, steps - 1), 0, 0), memory_space=pltpu.SMEM),
            pl.BlockSpec(memory_space=pl.ANY),
            pl.BlockSpec((tm, d), lambda i: (i, 0)),
            pl.BlockSpec((tm, LANES), lambda i: (i, 0)),
            pl.BlockSpec(mod_l.shape, lambda i: (0, 0)),
        ],
        out_specs=pl.BlockSpec((tm, d), lambda i: (i, 0)),
        out_shape=jax.ShapeDtypeStruct((t, d), F32),
        scratch_shapes=[pltpu.VMEM((2, 2, tm, d), F32), pltpu.SemaphoreType.DMA((2, 2))],
        compiler_params=_params("arbitrary"),
        name="moe_combine",
    )(pos3, pos3, ys, x_all, route, mod_l)


def _moe_ffn(x_all, g, mod_l, router, w1, w3, w2, *, tm, tpb, seq, ctx_row, tme, layer):
    t = x_all.shape[0]
    tokens, route = _route(x_all, g, mod_l, router, tm=tm, tpb=tpb, seq=seq, ctx_row=ctx_row)
    e_flat = route[:, 0:2].astype(jnp.int32).reshape(-1)
    onehot = (e_flat[:, None] == jnp.arange(N_EXPERTS, dtype=jnp.int32)[None, :]).astype(jnp.int32)
    csum = jnp.cumsum(onehot, axis=0)
    rank = jnp.take_along_axis(csum, e_flat[:, None], axis=1)[:, 0] - 1
    counts = csum[-1]
    padded = ((counts + tme - 1) // tme) * tme
    seg_end = jnp.cumsum(padded)
    seg_start = seg_end - padded
    pos = seg_start[e_flat] + rank
    n_tiles = -(-(2 * t + N_EXPERTS * (tme - 1)) // tme)
    tile_start = jnp.arange(n_tiles, dtype=jnp.int32) * tme
    tile_expert = jnp.minimum(jnp.sum((tile_start[:, None] >= seg_end[None, :]).astype(jnp.int32), axis=1),
                              N_EXPERTS - 1)
    tile_valid = (tile_start < seg_end[-1]).astype(jnp.int32)
    tile_src = jnp.minimum(jnp.arange(n_tiles, dtype=jnp.int32), seg_end[-1] // tme - 1)

    n_rows = n_tiles * tme
    fill_lo = jnp.concatenate([seg_start + counts, seg_end[-1:]]).astype(jnp.int32)
    fill_hi = jnp.concatenate([seg_end, jnp.full((1,), n_rows, jnp.int32)]).astype(jnp.int32)
    xs = _dispatch_rows(tokens, pos, fill_lo, fill_hi, tm=tm, n_rows=n_rows)
    ys = _grouped_swiglu(xs, tile_expert, tile_valid, tile_src, w1, w3, w2, tme=tme, layer=layer)
    return _combine(ys, pos, x_all, route, mod_l, tm=tm, tpb=tpb, seq=seq, ctx_row=ctx_row)


def _final_kernel(x_ref, g_ref, o_ref):
    x = x_ref[...]
    o_ref[...] = x * lax.rsqrt(jnp.mean(x * x, axis=-1, keepdims=True) + NORM_EPS) * g_ref[...]


def _final_norm(x3, g, *, seq):
    batch, _, d = x3.shape
    tr = _chunk(seq, 1024)
    return pl.pallas_call(
        _final_kernel,
        grid=(batch, seq // tr),
        in_specs=[pl.BlockSpec((None, tr, d), lambda b, j: (b, j, 0)), pl.BlockSpec((1, d), lambda b, j: (0, 0))],
        out_specs=pl.BlockSpec((None, tr, d), lambda b, j: (b, j, 0)),
        out_shape=jax.ShapeDtypeStruct((batch, seq, d), F32),
        compiler_params=_params("arbitrary", "arbitrary"),
        name="final_norm",
    )(x3, g.reshape(1, d))


def kernel(x, c, ctx, c_ctx, w_mod, b_mod, g_mix, g_ffn, w_in, w_out, ssm_a_re, ssm_a_im, ssm_log_step, ssm_b_re, ssm_b_im, ssm_c_re, ssm_c_im, ssm_d, glu_w, glu_b, na_rpb, ffn_w1, ffn_w3, ffn_w2, moe_router, moe_w1, moe_w3, moe_w2, g_final):
    batch, seq, d = x.shape
    n_ctx = ctx.shape[1]
    depth = w_mod.shape[0]
    ssm_w = ssm_d.shape[1]
    att_w = (w_in.shape[2] - ssm_w) // 3
    rpb_rows = seq + n_ctx
    assert batch == SUBLANES and batch < MOD_ROWS
    assert seq % GRID_W == 0 and att_w % LANES == 0 and seq % SSM_CHUNK == 0 and n_ctx % SSM_CHUNK == 0
    tm = _token_tile(rpb_rows)
    tpb = rpb_rows // tm
    tme = 1024
    common = dict(tm=tm, tpb=tpb, seq=seq, ctx_row=batch)

    cvec = jnp.zeros((MOD_ROWS, d), F32).at[:batch].set(c.astype(F32)).at[batch].set(c_ctx.astype(F32))
    mod = _mod_table(cvec, w_mod.astype(F32), b_mod.astype(F32))
    x_all = jnp.concatenate([x, ctx], axis=1).astype(F32).reshape(batch * rpb_rows, d)


    for l in range(depth):
        mod_l = mod[l]
        u, q, k, v = _in_proj(x_all, g_mix[l].astype(F32), mod_l, w_in[l].astype(BF16), ssm_w=ssm_w, att_w=att_w,
                              **common)

        u_tm = u.reshape(batch, rpb_rows, ssm_w).transpose(1, 0, 2)
        mats = _ssm_matrices(ssm_a_re[l], ssm_a_im[l], ssm_log_step[l], ssm_b_re[l], ssm_b_im[l], ssm_c_re[l],
                             ssm_c_im[l], ssm_d[l])
        y_tm = _ssm_mixer(u_tm, mats, seq=seq, n_ctx=n_ctx)
        y_ssm = y_tm.transpose(1, 0, 2).reshape(batch * rpb_rows, ssm_w)

        bias = _attention_bias(na_rpb[l], seq // GRID_W)
        y_att = _attention(q, k, v, bias, batch=batch, rpb_rows=rpb_rows, seq=seq, n_ctx=n_ctx)

        x_all = _mix_out(y_ssm, y_att, x_all, mod_l, glu_w[l].astype(BF16), glu_b[l].astype(F32),
                         w_out[l].astype(BF16), **common)

        if l % 2 == 0:
            x_all = _dense_ffn(x_all, g_ffn[l].astype(F32), mod_l, ffn_w1[l // 2].astype(BF16),
                               ffn_w3[l // 2].astype(BF16), ffn_w2[l // 2].astype(BF16), **common)
        else:
            x_all = _moe_ffn(x_all, g_ffn[l].astype(F32), mod_l, moe_router[l // 2], moe_w1, moe_w3, moe_w2,
                             tme=tme, layer=l // 2, **common)

    return _final_norm(x_all.reshape(batch, rpb_rows, d), g_final.astype(F32), seq=seq).astype(x.dtype)
```

```python
import functools
import math

import jax
import jax.numpy as jnp
import numpy as np
from jax import lax
from jax.experimental import pallas as pl
from jax.experimental.pallas import tpu as pltpu

F32 = jnp.float32
BF16 = jnp.bfloat16

GRID_W = 64
SSM_GROUP = 16
SSM_STATE = 64
HEAD_DIM = 64
NA_ROWS_MAX = 8
NA_COLS = 16
N_EXPERTS = 8
NORM_EPS = 1e-6
A_RE_MAX = -1e-4
MASK_VALUE = -1e30

LANES = 128
SUBLANES = 8
V7X_VMEM_LIMIT_BYTES = 56 * 1024 * 1024

MOD_ROWS = 16


def _params(*semantics):
    return pltpu.CompilerParams(dimension_semantics=semantics, vmem_limit_bytes=V7X_VMEM_LIMIT_BYTES)


def _token_tile(rows_per_batch):
    for parts in range(1, rows_per_batch + 1):
        if rows_per_batch % parts == 0:
            tm = rows_per_batch // parts
            if tm <= 1152 and tm % 16 == 0:
                return tm
    raise ValueError("no token tile for %d rows" % rows_per_batch)


def _chunk(total, target):
    best = None
    for c in range(LANES, min(total, target) + 1, LANES):
        if total % c == 0:
            best = c
    if best is None:
        raise ValueError("no lane-aligned chunk for %d" % total)
    return best


def _mod_kernel(c_ref, w_ref, b_ref, o_ref):
    a = jax.nn.silu(c_ref[...])
    o_ref[...] = jnp.dot(a, w_ref[...], preferred_element_type=F32) + b_ref[...]


def _mod_table(cvec, w_mod, b_mod):
    depth, d, n = w_mod.shape
    tn = _chunk(n, 1536)
    return pl.pallas_call(
        _mod_kernel,
        grid=(depth, n // tn),
        in_specs=[
            pl.BlockSpec((MOD_ROWS, d), lambda l, j: (0, 0)),
            pl.BlockSpec((None, d, tn), lambda l, j: (l, 0, j)),
            pl.BlockSpec((None, 1, tn), lambda l, j: (l, 0, j)),
        ],
        out_specs=pl.BlockSpec((None, MOD_ROWS, tn), lambda l, j: (l, 0, j)),
        out_shape=jax.ShapeDtypeStruct((depth, MOD_ROWS, n), F32),
        compiler_params=_params("arbitrary", "arbitrary"),
        name="mod_table",
    )(cvec, w_mod, b_mod.reshape(depth, 1, n))


def _mod_rows(mod_ref, slab, tile_idx, tm, tpb, seq, ctx_row, d):
    b = tile_idx // tpb
    v_b = mod_ref[pl.ds(b, 1), slab * d:(slab + 1) * d]
    v_c = mod_ref[ctx_row:ctx_row + 1, slab * d:(slab + 1) * d]
    row = (tile_idx % tpb) * tm + lax.broadcasted_iota(jnp.int32, (tm, 1), 0)
    return jnp.where(row >= seq, v_c, v_b)


def _norm_mod(x, g, mod_ref, slab, tile_idx, tpb, seq, ctx_row):
    tm, d = x.shape
    shift = _mod_rows(mod_ref, slab, tile_idx, tm, tpb, seq, ctx_row, d)
    scale = _mod_rows(mod_ref, slab + 1, tile_idx, tm, tpb, seq, ctx_row, d)
    rs = lax.rsqrt(jnp.mean(x * x, axis=-1, keepdims=True) + NORM_EPS)
    return (x * rs) * g * (1.0 + scale) + shift


def _in_kernel(x_ref, g_ref, mod_ref, w_ref, u_ref, q_ref, k_ref, v_ref, *, tpb, seq, ctx_row, ssm_w, att_w):
    i = pl.program_id(0)
    h = _norm_mod(x_ref[...], g_ref[...], mod_ref, 0, i, tpb, seq, ctx_row).astype(BF16)
    c1 = ssm_w + att_w
    c2 = c1 + att_w
    u_ref[...] = jnp.dot(h, w_ref[:, 0:ssm_w], preferred_element_type=F32)
    q_ref[...] = (jnp.dot(h, w_ref[:, ssm_w:c1], preferred_element_type=F32) * (HEAD_DIM ** -0.5)).astype(BF16)
    k_ref[...] = jnp.dot(h, w_ref[:, c1:c2], preferred_element_type=F32).astype(BF16)
    v_ref[...] = jnp.dot(h, w_ref[:, c2:c2 + att_w], preferred_element_type=F32).astype(BF16)


def _in_proj(x_all, g, mod_l, w_in, *, tm, tpb, seq, ctx_row, ssm_w, att_w):
    t, d = x_all.shape
    n = w_in.shape[1]
    row_block = lambda width: pl.BlockSpec((tm, width), lambda i: (i, 0))
    return pl.pallas_call(
        functools.partial(_in_kernel, tpb=tpb, seq=seq, ctx_row=ctx_row, ssm_w=ssm_w, att_w=att_w),
        grid=(t // tm,),
        in_specs=[
            row_block(d),
            pl.BlockSpec((1, d), lambda i: (0, 0)),
            pl.BlockSpec(mod_l.shape, lambda i: (0, 0)),
            pl.BlockSpec((d, n), lambda i: (0, 0)),
        ],
        out_specs=[row_block(ssm_w), row_block(att_w), row_block(att_w), row_block(att_w)],
        out_shape=[
            jax.ShapeDtypeStruct((t, ssm_w), F32),
            jax.ShapeDtypeStruct((t, att_w), BF16),
            jax.ShapeDtypeStruct((t, att_w), BF16),
            jax.ShapeDtypeStruct((t, att_w), BF16),
        ],
        compiler_params=_params("arbitrary"),
        name="in_proj",
    )(x_all, g.reshape(1, d), mod_l, w_in)


def _att_kernel(q_ref, k_ref, v_ref, bias_ref, o_ref, plat_s, pctx_s, den_s, *, seq, n_ctx, rows, win, group):
    lane = lax.broadcasted_iota(jnp.int32, (1, LANES), 1)
    first_head = lane < HEAD_DIM
    mask0 = first_head.astype(BF16)
    mask1 = 1 - mask0
    trans_b = (((1,), (1,)), ((), ()))
    lane_tiles = lambda arrs: [a[:, c:c + LANES] for a in arrs for c in range(0, a.shape[1], LANES)]

    def stack(q):
        return jnp.concatenate([q * mask0, q * mask1], axis=0)

    def unstack(o):
        m_rows = o.shape[0] // 2
        return jnp.where(first_head, o[:m_rows], o[m_rows:])

    def probabilities(q, parts):
        q2 = stack(q)
        scores = []
        for keys, bias in parts:
            s = lax.dot_general(q2, keys, trans_b, preferred_element_type=F32)
            scores.append(s if bias is None else s + bias)
        m = jnp.max(functools.reduce(jnp.maximum, lane_tiles(scores)), axis=-1, keepdims=True)
        probs = [jnp.exp(s - m) for s in scores]
        den = jnp.sum(functools.reduce(lambda a, b: a + b, lane_tiles(probs)), axis=-1, keepdims=True)
        return [p.astype(BF16) for p in probs], den

    def window_start(r):
        return pl.multiple_of(jnp.clip(r - win // 2, 0, rows - win) * GRID_W, GRID_W)

    def score_stage(r, slot):
        q0 = pl.multiple_of(r * GRID_W, GRID_W)
        r0 = jnp.clip(r - win // 2, 0, rows - win)
        kw = k_ref[pl.ds(window_start(r), win * GRID_W), :]
        kc = k_ref[seq:seq + n_ctx, :]
        (p_lat, p_ctx), den = probabilities(q_ref[pl.ds(q0, GRID_W), :], [(kw, bias_ref[r - r0]), (kc, None)])
        plat_s[slot] = p_lat
        pctx_s[slot] = p_ctx
        den_s[slot] = jnp.broadcast_to(den, den_s.shape[1:])

    def value_stage(r, slot):
        q0 = pl.multiple_of(r * GRID_W, GRID_W)
        vw = v_ref[pl.ds(window_start(r), win * GRID_W), :]
        vc = v_ref[seq:seq + n_ctx, :]
        o = (jnp.dot(plat_s[slot], vw, preferred_element_type=F32)
             + jnp.dot(pctx_s[slot], vc, preferred_element_type=F32)) / den_s[slot]
        o_ref[pl.ds(q0, GRID_W), :] = unstack(o).astype(o_ref.dtype)

    n_groups = rows // group
    for g in range(group):
        score_stage(g, g)

    def pipelined(it, carry):
        cur = (it % 2) * group
        nxt = group - cur
        for g in range(group):
            value_stage(it * group + g, cur + g)
        for g in range(group):
            score_stage((it + 1) * group + g, nxt + g)
        return carry

    lax.fori_loop(0, n_groups - 1, pipelined, 0)
    last = ((n_groups - 1) % 2) * group
    for g in range(group):
        value_stage((n_groups - 1) * group + g, last + g)

    kc = k_ref[seq:seq + n_ctx, :]
    vc = v_ref[seq:seq + n_ctx, :]
    (p_ctx,), den = probabilities(q_ref[seq:seq + n_ctx, :], [(kc, None)])
    o = jnp.dot(p_ctx, vc, preferred_element_type=F32) / den
    o_ref[seq:seq + n_ctx, :] = unstack(o).astype(o_ref.dtype)


def _attention(q, k, v, bias, *, batch, rpb_rows, seq, n_ctx):
    t, att_w = q.shape
    rows = seq // GRID_W
    win = min(NA_ROWS_MAX, rows)
    n_pairs = att_w // LANES
    group = 8
    assert rows % group == 0
    blk = pl.BlockSpec((rpb_rows, LANES), lambda b, p: (b, p))
    return pl.pallas_call(
        functools.partial(_att_kernel, seq=seq, n_ctx=n_ctx, rows=rows, win=win, group=group),
        grid=(batch, n_pairs),
        in_specs=[blk, blk, blk,
                  pl.BlockSpec((None, win, 2 * GRID_W, win * GRID_W), lambda b, p: (p, 0, 0, 0))],
        out_specs=blk,
        out_shape=jax.ShapeDtypeStruct((t, att_w), BF16),
        scratch_shapes=[
            pltpu.VMEM((2 * group, 2 * GRID_W, win * GRID_W), BF16),
            pltpu.VMEM((2 * group, 2 * GRID_W, n_ctx), BF16),
            pltpu.VMEM((2 * group, 2 * GRID_W, LANES), F32),
        ],
        compiler_params=_params("arbitrary", "arbitrary"),
        name="attention",
    )(q, k, v, bias)


def _attention_bias(rpb, rows):
    n_heads = rpb.shape[0]
    win = min(NA_ROWS_MAX, rows)
    col = jnp.arange(GRID_W)
    col_start = jnp.clip(col - NA_COLS // 2, 0, GRID_W - NA_COLS)
    kcol = jnp.arange(GRID_W)
    valid = (kcol[None, :] >= col_start[:, None]) & (kcol[None, :] < col_start[:, None] + NA_COLS)
    col_rel = kcol[None, :] - col[:, None] + NA_COLS - 1
    var = jnp.arange(win)
    rr = jnp.arange(win)
    row_rel = rr[None, :] - var[:, None] + NA_ROWS_MAX - 1
    by_row = rpb.astype(F32)[:, row_rel]
    pick = (col_rel[:, :, None] == jnp.arange(2 * NA_COLS - 1)[None, None, :]).astype(F32)
    by_row = by_row.reshape(n_heads // 2, 2, win, win, 2 * NA_COLS - 1)
    tab = jnp.einsum("pavrj,ckj->pvacrk", by_row, pick, precision=lax.Precision.HIGHEST)
    tab = jnp.where(valid[None, None, None, :, None, :], tab, MASK_VALUE)
    return tab.reshape(n_heads // 2, win, 2 * GRID_W, win * GRID_W)


SSM_CHUNK = 16
SLOT = SSM_GROUP
SLOTS = LANES // SLOT


def _slot_of(g, ti):
    return ((g % SLOTS) + (ti % SLOTS)) % SLOTS


def _ssm_pack_kernel(u_ref, t_ref, bre_ref, bim_ref, yi_ref, sre_ref, sim_ref):
    cb, ch, nb, width = u_ref.shape
    rows = cb * nb
    n_groups = width // SSM_GROUP
    lane = lax.broadcasted_iota(jnp.int32, (1, LANES), 1) // SLOT
    slot_mask = [lane == s for s in range(SLOTS)]
    rolled = [[pltpu.roll(u_ref[:, ti, :, j * LANES:(j + 1) * LANES].reshape(rows, LANES), (ti % SLOTS) * SLOT, axis=1)
               for j in range(width // LANES)] for ti in range(ch)]
    for g in range(n_groups):
        j = g // SLOTS
        tiles = []
        for q in range(ch // SLOTS):
            acc = jnp.zeros((rows, LANES), F32)
            for t8 in range(SLOTS):
                ti = q * SLOTS + t8
                acc = jnp.where(slot_mask[_slot_of(g, ti)], rolled[ti][j], acc)
            tiles.append(acc)
        ug = jnp.concatenate(tiles, axis=1).astype(BF16)
        yi_ref[g] = jnp.dot(ug, t_ref[g], preferred_element_type=F32)
        sre_ref[:, g * LANES:(g + 1) * LANES] = jnp.dot(ug, bre_ref[g], preferred_element_type=F32)
        sim_ref[:, g * LANES:(g + 1) * LANES] = jnp.dot(ug, bim_ref[g], preferred_element_type=F32)


def _ssm_carry_kernel(sre_ref, sim_ref, are_ref, aim_ref, hfre_ref, hfim_ref, hbre_ref, hbim_ref, *, nb, n_lat_ch,
                      n_ctx_ch):
    n_ch = n_lat_ch + n_ctx_ch
    lanes = sre_ref.shape[1]
    fwd = (lax.broadcasted_iota(jnp.int32, (1, lanes), 1) % LANES) < (LANES // 2)
    a_re = jnp.broadcast_to(are_ref[...], (nb, lanes))
    a_im = jnp.broadcast_to(aim_ref[...], (nb, lanes))

    def step(k, carry):
        h_re, h_im = carry
        cf = jnp.where(k < n_ctx_ch, n_lat_ch + k, k - n_ctx_ch)
        cb = n_ch - 1 - k
        rf = pl.ds(pl.multiple_of(cf * nb, nb), nb)
        rb = pl.ds(pl.multiple_of(cb * nb, nb), nb)
        hfre_ref[rf, :] = h_re
        hfim_ref[rf, :] = h_im
        hbre_ref[rb, :] = h_re
        hbim_ref[rb, :] = h_im
        s_re = jnp.where(fwd, sre_ref[rf, :], sre_ref[rb, :])
        s_im = jnp.where(fwd, sim_ref[rf, :], sim_ref[rb, :])
        return a_re * h_re - a_im * h_im + s_re, a_re * h_im + a_im * h_re + s_im

    zero = jnp.zeros((nb, lanes), F32)
    lax.fori_loop(0, n_ch, step, (zero, zero))


def _ssm_unpack_kernel(yi_ref, hfre_ref, hfim_ref, hbre_ref, hbim_ref, co_ref, y_ref):
    cb, ch, nb, width = y_ref.shape
    rows = cb * nb
    n_groups = width // SSM_GROUP
    lane = lax.broadcasted_iota(jnp.int32, (1, LANES), 1) // SLOT
    slot_mask = [lane == s for s in range(SLOTS)]
    packed = []
    for g in range(n_groups):
        sl = slice(g * LANES, (g + 1) * LANES)
        h = jnp.concatenate([hfre_ref[:, sl], hfim_ref[:, sl], hbre_ref[:, sl], hbim_ref[:, sl]], axis=1).astype(BF16)
        packed.append(yi_ref[g] + jnp.dot(h, co_ref[g], preferred_element_type=F32))
    for ti in range(ch):
        q = ti // SLOTS
        for j in range(width // LANES):
            acc = jnp.zeros((rows, LANES), F32)
            for g8 in range(SLOTS):
                g = j * SLOTS + g8
                acc = jnp.where(slot_mask[_slot_of(g, ti)], packed[g][:, q * LANES:(q + 1) * LANES], acc)
            tile = pltpu.roll(acc, (LANES - (ti % SLOTS) * SLOT) % LANES, axis=1)
            y_ref[:, ti, :, j * LANES:(j + 1) * LANES] = tile.reshape(cb, nb, LANES)


def _ssm_mixer(u_tm, mats, *, seq, n_ctx, layer):
    total, nb, width = u_tm.shape
    t_mat, b_re, b_im, c_out, a_re, a_im = mats
    n_groups = width // SSM_GROUP
    n_ch = total // SSM_CHUNK
    cblk = next(c for c in (17, 16, 8, 4, 2, 1) if n_ch % c == 0)
    rows = cblk * nb
    u4 = u_tm.reshape(n_ch, SSM_CHUNK, nb, width)
    state_w = n_groups * LANES
    resident = lambda a: pl.BlockSpec((None,) + a.shape[1:], lambda i: (layer,) + (0,) * (a.ndim - 1),
                                      pipeline_mode=pl.Buffered(1))
    seq_block = pl.BlockSpec((cblk, SSM_CHUNK, nb, width), lambda i: (i, 0, 0, 0))
    packed_block = pl.BlockSpec((n_groups, rows, 2 * LANES), lambda i: (0, i, 0))
    state_block = pl.BlockSpec((rows, state_w), lambda i: (i, 0))
    state_shape = jax.ShapeDtypeStruct((n_ch * nb, state_w), F32)
    y_intra, s_re, s_im = pl.pallas_call(
        _ssm_pack_kernel,
        grid=(n_ch // cblk,),
        in_specs=[seq_block, resident(t_mat), resident(b_re), resident(b_im)],
        out_specs=[packed_block, state_block, state_block],
        out_shape=[jax.ShapeDtypeStruct((n_groups, n_ch * nb, 2 * LANES), F32), state_shape, state_shape],
        compiler_params=_params("arbitrary"),
        name="ssm_pack",
    )(u4, t_mat, b_re, b_im)

    lane_blk = 2 * LANES
    col_block = pl.BlockSpec((n_ch * nb, lane_blk), lambda i: (0, i))
    vec_block = pl.BlockSpec((None, 1, lane_blk), lambda i: (layer, 0, i))
    h_states = pl.pallas_call(
        functools.partial(_ssm_carry_kernel, nb=nb, n_lat_ch=seq // SSM_CHUNK, n_ctx_ch=n_ctx // SSM_CHUNK),
        grid=(state_w // lane_blk,),
        in_specs=[col_block, col_block, vec_block, vec_block],
        out_specs=[col_block] * 4,
        out_shape=[state_shape] * 4,
        compiler_params=_params("arbitrary"),
        name="ssm_carry",
    )(s_re, s_im, a_re, a_im)

    y4 = pl.pallas_call(
        _ssm_unpack_kernel,
        grid=(n_ch // cblk,),
        in_specs=[packed_block] + [state_block] * 4 + [resident(c_out)],
        out_specs=seq_block,
        out_shape=jax.ShapeDtypeStruct(u4.shape, F32),
        compiler_params=_params("arbitrary"),
        name="ssm_unpack",
    )(y_intra, *h_states, c_out)
    return y4.reshape(total, nb, width)


def _ssm_matrices(a_re, a_im, log_step, b_re, b_im, c_re, c_im, d_skip):
    hp = lax.Precision.HIGHEST
    _, g, p, h = b_re.shape
    ch = SSM_CHUNK
    a_re = jnp.minimum(a_re.astype(F32), A_RE_MAX)
    a_im = a_im.astype(F32)
    dt = jnp.exp(log_step.astype(F32))[..., None]
    k = jnp.arange(ch + 1, dtype=F32)[:, None, None, None]
    mag = jnp.exp(a_re * dt * k)
    lr = mag * jnp.cos(a_im * dt * k)
    li = mag * jnp.sin(a_im * dt * k)
    lam_re, lam_im = lr[1], li[1]
    den = a_re * a_re + a_im * a_im
    z_re = ((lam_re - 1) * a_re + lam_im * a_im) / den
    z_im = (lam_im * a_re - (lam_re - 1) * a_im) / den
    b_re = b_re.astype(F32)
    b_im = b_im.astype(F32)
    bb_re = z_re[..., None] * b_re - z_im[..., None] * b_im
    bb_im = z_re[..., None] * b_im + z_im[..., None] * b_re
    c_re = c_re.astype(F32)
    c_im = c_im.astype(F32)
    w_re = c_re[None] * lr[:, :, :, None, :] - c_im[None] * li[:, :, :, None, :]
    w_im = c_re[None] * li[:, :, :, None, :] + c_im[None] * lr[:, :, :, None, :]
    kern = (jnp.einsum("dxgop,xgpi->dxgoi", w_re, bb_re, precision=hp)
            - jnp.einsum("dxgop,xgpi->dxgoi", w_im, bb_im, precision=hp))
    n_tile = g // SLOTS
    u_idx = np.arange(ch)
    nat = np.stack([(u_idx // SLOTS) * SLOTS + ((u_idx % SLOTS) - lo) % SLOTS for lo in range(SLOTS)])
    unpack = (nat[:, :, None] == np.arange(ch)[None, None, :]).astype(np.float32)
    lag = nat[:, None, :] - nat[:, :, None]
    d_idx = np.arange(ch + 1)
    sel_f = (lag[..., None] == d_idx).astype(np.float32)
    sel_b = (-lag[..., None] == d_idx).astype(np.float32)
    by_tile = lambda a: a.reshape(a.shape[:1] + (n_tile, SLOTS) + a.shape[2:])
    toe = (jnp.einsum("lutd,dnloi->nluito", sel_f, by_tile(kern[:, 0]), precision=hp)
           + jnp.einsum("lutd,dnloi->nluito", sel_b, by_tile(kern[:, 1]), precision=hp))
    skip_lane = jnp.tile(d_skip.astype(F32).reshape(g, h), (1, ch))
    t_mat = toe.reshape(g, ch * h, ch * h) + jnp.eye(ch * h, dtype=F32)[None] * skip_lane[:, None, :]

    cm = lambda ar, ai, br, bi: (ar * br - ai * bi, ar * bi + ai * br)
    mf_re, mf_im = cm(jnp.flip(lr[:ch, 0], 0)[..., None], jnp.flip(li[:ch, 0], 0)[..., None], bb_re[0][None],
                      bb_im[0][None])
    mb_re, mb_im = cm(lr[:ch, 1][..., None], li[:ch, 1][..., None], bb_re[1][None], bb_im[1][None])
    pack_state = lambda f, b: jnp.einsum("lus,snlpi->nluip", unpack, by_tile(jnp.concatenate([f, b], axis=2)),
                                         precision=hp).reshape(g, ch * h, 2 * p)
    bs_re = pack_state(mf_re, mb_re)
    bs_im = pack_state(mf_im, mb_im)

    rd = lambda w: jnp.einsum("lut,tnlop->nlpuo", unpack, by_tile(w), precision=hp).reshape(g, p, ch * h)
    fwd_pow = lambda w: rd(w[1:ch + 1, 0])
    bwd_pow = lambda w: rd(jnp.flip(w[1:ch + 1, 1], 0))
    zeros = jnp.zeros((g, p, ch * h), F32)
    co = jnp.concatenate([fwd_pow(w_re), zeros, -fwd_pow(w_im), zeros,
                          zeros, bwd_pow(w_re), zeros, -bwd_pow(w_im)], axis=1)

    a16_re = jnp.concatenate([lr[ch, 0], lr[ch, 1]], axis=1).reshape(1, g * 2 * p)
    a16_im = jnp.concatenate([li[ch, 0], li[ch, 1]], axis=1).reshape(1, g * 2 * p)
    return t_mat.astype(BF16), bs_re.astype(BF16), bs_im.astype(BF16), co.astype(BF16), a16_re, a16_im


def _mix_kernel(y_ref, att_ref, x_ref, mod_ref, gw_ref, gb_ref, wo_ref, o_ref, *, tpb, seq, ctx_row):
    i = pl.program_id(0)
    tm, d = x_ref.shape
    ssm_w = y_ref.shape[1]
    g = jax.nn.gelu(y_ref[...])
    z = jnp.dot(g.astype(BF16), gw_ref[...], preferred_element_type=F32) + gb_ref[...]
    s = g * jax.nn.sigmoid(z)
    o = (jnp.dot(s.astype(BF16), wo_ref[0:ssm_w, :], preferred_element_type=F32)
         + jnp.dot(att_ref[...], wo_ref[ssm_w:, :], preferred_element_type=F32))
    gate = _mod_rows(mod_ref, 2, i, tm, tpb, seq, ctx_row, d)
    o_ref[...] = x_ref[...] + gate * o


def _mix_out(y_ssm, y_att, x_all, mod_l, glu_w, glu_b, w_out, *, tm, tpb, seq, ctx_row):
    t, d = x_all.shape
    ssm_w = y_ssm.shape[1]
    att_w = y_att.shape[1]
    row_block = lambda width: pl.BlockSpec((tm, width), lambda i: (i, 0))
    whole = lambda a: pl.BlockSpec(a.shape, lambda i: (0, 0))
    gb = glu_b.reshape(1, ssm_w)
    return pl.pallas_call(
        functools.partial(_mix_kernel, tpb=tpb, seq=seq, ctx_row=ctx_row),
        grid=(t // tm,),
        in_specs=[row_block(ssm_w), row_block(att_w), row_block(d), whole(mod_l), whole(glu_w), whole(gb),
                  whole(w_out)],
        out_specs=row_block(d),
        out_shape=jax.ShapeDtypeStruct((t, d), F32),
        compiler_params=_params("arbitrary"),
        name="mix_out",
    )(y_ssm, y_att, x_all, mod_l, glu_w, gb, w_out)


def _ffn_kernel(x_ref, g_ref, mod_ref, w1_ref, w3_ref, w2_ref, o_ref, t_s, acc_s, *, tpb, seq, ctx_row):
    i = pl.program_id(0)
    tm, d = x_ref.shape
    nf = w1_ref.shape[0]
    t_s[...] = _norm_mod(x_ref[...], g_ref[...], mod_ref, 3, i, tpb, seq, ctx_row).astype(BF16)
    acc_s[...] = jnp.zeros_like(acc_s)

    def chunk(c, carry):
        t = t_s[...]
        h1 = jnp.dot(t, w1_ref[c], preferred_element_type=F32)
        h3 = jnp.dot(t, w3_ref[c], preferred_element_type=F32)
        a = (jax.nn.silu(h1) * h3).astype(BF16)
        acc_s[...] += jnp.dot(a, w2_ref[c], preferred_element_type=F32)
        return carry

    lax.fori_loop(0, nf, chunk, 0, unroll=2)
    gate = _mod_rows(mod_ref, 5, i, tm, tpb, seq, ctx_row, d)
    o_ref[...] = x_ref[...] + gate * acc_s[...]


def _dense_ffn(x_all, g, mod_l, w1, w3, w2, *, tm, tpb, seq, ctx_row):
    t, d = x_all.shape
    dff = w1.shape[1]
    tf = _chunk(dff, 256)
    nf = dff // tf
    w1c = w1.reshape(d, nf, tf).transpose(1, 0, 2)
    w3c = w3.reshape(d, nf, tf).transpose(1, 0, 2)
    w2c = w2.reshape(nf, tf, d)
    resident = lambda a: pl.BlockSpec(a.shape, lambda i: (0,) * a.ndim, pipeline_mode=pl.Buffered(1))
    return pl.pallas_call(
        functools.partial(_ffn_kernel, tpb=tpb, seq=seq, ctx_row=ctx_row),
        grid=(t // tm,),
        in_specs=[
            pl.BlockSpec((tm, d), lambda i: (i, 0)),
            pl.BlockSpec((1, d), lambda i: (0, 0)),
            pl.BlockSpec(mod_l.shape, lambda i: (0, 0)),
            resident(w1c), resident(w3c), resident(w2c),
        ],
        out_specs=pl.BlockSpec((tm, d), lambda i: (i, 0)),
        out_shape=jax.ShapeDtypeStruct((t, d), F32),
        scratch_shapes=[pltpu.VMEM((tm, d), BF16), pltpu.VMEM((tm, d), F32)],
        compiler_params=_params("arbitrary"),
        name="dense_ffn",
    )(x_all, g.reshape(1, d), mod_l, w1c, w3c, w2c)


def _route_kernel(x_ref, g_ref, mod_ref, r_ref, t_ref, route_ref, *, tpb, seq, ctx_row):
    i = pl.program_id(0)
    t = _norm_mod(x_ref[...], g_ref[...], mod_ref, 3, i, tpb, seq, ctx_row)
    t_ref[...] = t
    th = t.astype(BF16)
    tl = (t - th.astype(F32)).astype(BF16)
    r = r_ref[...]
    rh = r.astype(BF16)
    rl = (r - rh.astype(F32)).astype(BF16)
    dot = lambda a, b: jnp.dot(a, b, preferred_element_type=F32)
    logits = dot(th, rh) + (dot(th, rl) + dot(tl, rh)) + dot(tl, rl)
    tm = logits.shape[0]
    lane = lax.broadcasted_iota(jnp.int32, (tm, LANES), 1)
    lane_f = lane.astype(F32)
    neg_inf = jnp.float32(-jnp.inf)
    lg = jnp.where(lane < N_EXPERTS, logits, neg_inf)
    m1 = jnp.max(lg, axis=-1, keepdims=True)
    i1 = jnp.min(jnp.where(lg == m1, lane_f, float(LANES)), axis=-1, keepdims=True)
    lg2 = jnp.where(lane_f == i1, neg_inf, lg)
    m2 = jnp.max(lg2, axis=-1, keepdims=True)
    i2 = jnp.min(jnp.where(lg2 == m2, lane_f, float(LANES)), axis=-1, keepdims=True)
    e = jnp.exp(m2 - m1)
    g1 = 1.0 / (1.0 + e)
    g2 = e / (1.0 + e)
    route_ref[...] = jnp.where(lane == 0, i1, jnp.where(lane == 1, i2, jnp.where(lane == 2, g1, jnp.where(
        lane == 3, g2, 0.0))))


def _route(x_all, g, mod_l, router, *, tm, tpb, seq, ctx_row):
    t, d = x_all.shape
    r_pad = jnp.zeros((d, LANES), F32).at[:, :router.shape[1]].set(router.astype(F32))
    return pl.pallas_call(
        functools.partial(_route_kernel, tpb=tpb, seq=seq, ctx_row=ctx_row),
        grid=(t // tm,),
        in_specs=[
            pl.BlockSpec((tm, d), lambda i: (i, 0)),
            pl.BlockSpec((1, d), lambda i: (0, 0)),
            pl.BlockSpec(mod_l.shape, lambda i: (0, 0)),
            pl.BlockSpec((d, LANES), lambda i: (0, 0)),
        ],
        out_specs=[pl.BlockSpec((tm, d), lambda i: (i, 0)), pl.BlockSpec((tm, LANES), lambda i: (i, 0))],
        out_shape=[jax.ShapeDtypeStruct((t, d), F32), jax.ShapeDtypeStruct((t, LANES), F32)],
        compiler_params=_params("arbitrary"),
        name="moe_route",
    )(x_all, g.reshape(1, d), mod_l, r_pad)


def _row_copy(src_hbm, dst_vmem, sem, src_row, dst_row):
    return pltpu.make_async_copy(src_hbm.at[pl.ds(src_row, 1), :], dst_vmem.at[pl.ds(dst_row, 1), :], sem)


ISSUE_UNROLL = 8


ZERO_BLOCK_ROWS = 256


def _dispatch_kernel(fill_lo_ref, fill_hi_ref, pos_ref, t_ref, xs_hbm, zero_s, sems):
    i = pl.program_id(0)
    tm = t_ref.shape[0]
    per_trip = ISSUE_UNROLL // 2

    def issue(g, carry):
        for k in range(per_trip):
            r = g * per_trip + k
            for choice in range(2):
                pltpu.make_async_copy(t_ref.at[pl.ds(r, 1), :], xs_hbm.at[pl.ds(pos_ref[0, 0, 2 * r + choice], 1), :],
                                      sems.at[choice]).start()
        return carry

    lax.fori_loop(0, tm // per_trip, issue, 0)

    @pl.when(i == 0)
    def _():
        zero_s[...] = jnp.zeros_like(zero_s)
        zero_row = lambda row: pltpu.make_async_copy(zero_s.at[pl.ds(0, 1), :], xs_hbm.at[pl.ds(row, 1), :],
                                                     sems.at[2])
        zero_block = lambda blk: pltpu.make_async_copy(
            zero_s, xs_hbm.at[pl.ds(pl.multiple_of(blk * ZERO_BLOCK_ROWS, ZERO_BLOCK_ROWS), ZERO_BLOCK_ROWS), :],
            sems.at[2])

        def start_then_wait(copy, lo, hi):
            def start(k, carry):
                copy(k).start()
                return carry

            def wait(k, carry):
                copy(k).wait()
                return carry

            lax.fori_loop(lo, hi, start, 0)
            lax.fori_loop(lo, hi, wait, 0)

        for e in range(N_EXPERTS):
            start_then_wait(zero_row, fill_lo_ref[e], fill_hi_ref[e])
        start_then_wait(zero_block, fill_lo_ref[N_EXPERTS] // ZERO_BLOCK_ROWS,
                        fill_hi_ref[N_EXPERTS] // ZERO_BLOCK_ROWS)

    for choice in range(2):
        pltpu.make_async_copy(t_ref, xs_hbm.at[pl.ds(0, tm), :], sems.at[choice]).wait()


def _dispatch_rows(tokens, pos, fill_lo, fill_hi, *, tm, n_rows):
    t, d = tokens.shape
    steps = t // tm
    assert tm % (ISSUE_UNROLL // 2) == 0 and n_rows % ZERO_BLOCK_ROWS == 0
    grid_spec = pltpu.PrefetchScalarGridSpec(
        num_scalar_prefetch=2,
        grid=(steps,),
        in_specs=[
            pl.BlockSpec((1, 1, 2 * tm), lambda i, lo, hi: (i, 0, 0), memory_space=pltpu.SMEM),
            pl.BlockSpec((tm, d), lambda i, lo, hi: (i, 0)),
        ],
        out_specs=pl.BlockSpec(memory_space=pl.ANY),
        scratch_shapes=[pltpu.VMEM((ZERO_BLOCK_ROWS, d), tokens.dtype), pltpu.SemaphoreType.DMA((3,))],
    )
    return pl.pallas_call(
        _dispatch_kernel,
        grid_spec=grid_spec,
        out_shape=jax.ShapeDtypeStruct((n_rows, d), tokens.dtype),
        compiler_params=_params("arbitrary"),
        name="moe_dispatch",
    )(fill_lo, fill_hi, pos.reshape(steps, 1, 2 * tm), tokens)


def _gmm_kernel(te_ref, tv_ref, ts_ref, xs_ref, w1_ref, w3_ref, w2_ref, o_ref, xb_s, acc_s, *, nf, n_sub):
    j = pl.program_id(0)
    f = pl.program_id(1)
    valid = tv_ref[j] > 0

    @pl.when(valid)
    def _():
        @pl.when(f == 0)
        def _():
            xb_s[...] = xs_ref[...].astype(BF16)
            acc_s[...] = jnp.zeros_like(acc_s)

        x = xb_s[...]
        tf = w1_ref.shape[1]
        sub = tf // n_sub
        part = None
        for c0 in range(0, tf, sub):
            h1 = jnp.dot(x, w1_ref[:, c0:c0 + sub].astype(BF16), preferred_element_type=F32)
            h3 = jnp.dot(x, w3_ref[:, c0:c0 + sub].astype(BF16), preferred_element_type=F32)
            a = (jax.nn.silu(h1) * h3).astype(BF16)
            y = jnp.dot(a, w2_ref[c0:c0 + sub, :].astype(BF16), preferred_element_type=F32)
            part = y if part is None else part + y
        acc_s[...] += part

        @pl.when(f == nf - 1)
        def _():
            o_ref[...] = acc_s[...]

    @pl.when(jnp.logical_and(jnp.logical_not(valid), f == nf - 1))
    def _():
        o_ref[...] = jnp.zeros_like(o_ref)


def _grouped_swiglu(xs, tile_expert, tile_valid, tile_src, w1, w3, w2, *, tme, layer):
    n, d = xs.shape
    dff = w1.shape[3]
    tf = _chunk(dff, 512)
    nf = dff // tf
    grid_spec = pltpu.PrefetchScalarGridSpec(
        num_scalar_prefetch=3,
        grid=(n // tme, nf),
        in_specs=[
            pl.BlockSpec((tme, d), lambda j, f, te, tv, ts: (ts[j], 0)),
            pl.BlockSpec((None, None, d, tf), lambda j, f, te, tv, ts: (layer, te[j], 0, f)),
            pl.BlockSpec((None, None, d, tf), lambda j, f, te, tv, ts: (layer, te[j], 0, f)),
            pl.BlockSpec((None, None, tf, d), lambda j, f, te, tv, ts: (layer, te[j], f, 0)),
        ],
        out_specs=pl.BlockSpec((tme, d), lambda j, f, te, tv, ts: (j, 0)),
        scratch_shapes=[pltpu.VMEM((tme, d), BF16), pltpu.VMEM((tme, d), F32)],
    )
    return pl.pallas_call(
        functools.partial(_gmm_kernel, nf=nf, n_sub=2),
        grid_spec=grid_spec,
        out_shape=jax.ShapeDtypeStruct((n, d), F32),
        compiler_params=_params("arbitrary", "arbitrary"),
        name="moe_experts",
    )(tile_expert, tile_valid, tile_src, xs, w1, w3, w2)


def _combine_kernel(pos_ref, pos_next_ref, ys_hbm, x_ref, route_ref, mod_ref, o_ref, buf, sems, *, tpb, seq,
                    ctx_row):
    i = pl.program_id(0)
    tm, d = x_ref.shape

    def issue_all(pos, slot):
        def issue(g, carry):
            for k in range(ISSUE_UNROLL // 2):
                r = g * (ISSUE_UNROLL // 2) + k
                _row_copy(ys_hbm, buf.at[slot, 0], sems.at[slot, 0], pos[0, 0, 2 * r], r).start()
                _row_copy(ys_hbm, buf.at[slot, 1], sems.at[slot, 1], pos[0, 0, 2 * r + 1], r).start()
            return carry

        lax.fori_loop(0, tm // (ISSUE_UNROLL // 2), issue, 0)

    @pl.when(i == 0)
    def _():
        issue_all(pos_ref, 0)

    @pl.when(i + 1 < pl.num_programs(0))
    def _():
        issue_all(pos_next_ref, (i + 1) % 2)

    slot = i % 2
    pltpu.make_async_copy(ys_hbm.at[pl.ds(0, tm), :], buf.at[slot, 0], sems.at[slot, 0]).wait()
    pltpu.make_async_copy(ys_hbm.at[pl.ds(0, tm), :], buf.at[slot, 1], sems.at[slot, 1]).wait()
    g1 = route_ref[:, 2:3]
    g2 = route_ref[:, 3:4]
    gate = _mod_rows(mod_ref, 5, i, tm, tpb, seq, ctx_row, d)
    o_ref[...] = x_ref[...] + gate * (g1 * buf[slot, 0] + g2 * buf[slot, 1])


def _combine(ys, pos, x_all, route, mod_l, *, tm, tpb, seq, ctx_row):
    t, d = x_all.shape
    steps = t // tm
    assert tm % (ISSUE_UNROLL // 2) == 0
    pos3 = pos.reshape(steps, 1, 2 * tm)
    return pl.pallas_call(
        functools.partial(_combine_kernel, tpb=tpb, seq=seq, ctx_row=ctx_row),
        grid=(steps,),
        in_specs=[
            pl.BlockSpec((1, 1, 2 * tm), lambda i: (i, 0, 0), memory_space=pltpu.SMEM),
            pl.BlockSpec((1, 1, 2 * tm), lambda i: (jnp.minimum(i + 1, steps - 1), 0, 0), memory_space=pltpu.SMEM),
            pl.BlockSpec(memory_space=pl.ANY),
            pl.BlockSpec((tm, d), lambda i: (i, 0)),
            pl.BlockSpec((tm, LANES), lambda i: (i, 0)),
            pl.BlockSpec(mod_l.shape, lambda i: (0, 0)),
        ],
        out_specs=pl.BlockSpec((tm, d), lambda i: (i, 0)),
        out_shape=jax.ShapeDtypeStruct((t, d), F32),
        scratch_shapes=[pltpu.VMEM((2, 2, tm, d), F32), pltpu.SemaphoreType.DMA((2, 2))],
        compiler_params=_params("arbitrary"),
        name="moe_combine",
    )(pos3, pos3, ys, x_all, route, mod_l)


def _moe_ffn(x_all, g, mod_l, router, w1, w3, w2, *, tm, tpb, seq, ctx_row, tme, layer):
    t = x_all.shape[0]
    tokens, route = _route(x_all, g, mod_l, router, tm=tm, tpb=tpb, seq=seq, ctx_row=ctx_row)
    e_flat = route[:, 0:2].astype(jnp.int32).reshape(-1)
    onehot = (e_flat[:, None] == jnp.arange(N_EXPERTS, dtype=jnp.int32)[None, :]).astype(jnp.int32)
    csum = jnp.cumsum(onehot, axis=0)
    rank = jnp.take_along_axis(csum, e_flat[:, None], axis=1)[:, 0] - 1
    counts = csum[-1]
    padded = ((counts + tme - 1) // tme) * tme
    seg_end = jnp.cumsum(padded)
    seg_start = seg_end - padded
    pos = seg_start[e_flat] + rank
    n_tiles = -(-(2 * t + N_EXPERTS * (tme - 1)) // tme)
    tile_start = jnp.arange(n_tiles, dtype=jnp.int32) * tme
    tile_expert = jnp.minimum(jnp.sum((tile_start[:, None] >= seg_end[None, :]).astype(jnp.int32), axis=1),
                              N_EXPERTS - 1)
    tile_valid = (tile_start < seg_end[-1]).astype(jnp.int32)
    tile_src = jnp.minimum(jnp.arange(n_tiles, dtype=jnp.int32), seg_end[-1] // tme - 1)

    n_rows = n_tiles * tme
    fill_lo = jnp.concatenate([seg_start + counts, seg_end[-1:]]).astype(jnp.int32)
    fill_hi = jnp.concatenate([seg_end, jnp.full((1,), n_rows, jnp.int32)]).astype(jnp.int32)
    xs = _dispatch_rows(tokens, pos, fill_lo, fill_hi, tm=tm, n_rows=n_rows)
    ys = _grouped_swiglu(xs, tile_expert, tile_valid, tile_src, w1, w3, w2, tme=tme, layer=layer)
    return _combine(ys, pos, x_all, route, mod_l, tm=tm, tpb=tpb, seq=seq, ctx_row=ctx_row)


def _final_kernel(x_ref, g_ref, o_ref):
    x = x_ref[...]
    o_ref[...] = x * lax.rsqrt(jnp.mean(x * x, axis=-1, keepdims=True) + NORM_EPS) * g_ref[...]


def _final_norm(x3, g, *, seq):
    batch, _, d = x3.shape
    tr = _chunk(seq, 1024)
    return pl.pallas_call(
        _final_kernel,
        grid=(batch, seq // tr),
        in_specs=[pl.BlockSpec((None, tr, d), lambda b, j: (b, j, 0)), pl.BlockSpec((1, d), lambda b, j: (0, 0))],
        out_specs=pl.BlockSpec((None, tr, d), lambda b, j: (b, j, 0)),
        out_shape=jax.ShapeDtypeStruct((batch, seq, d), F32),
        compiler_params=_params("arbitrary", "arbitrary"),
        name="final_norm",
    )(x3, g.reshape(1, d))


def kernel(x, c, ctx, c_ctx, w_mod, b_mod, g_mix, g_ffn, w_in, w_out, ssm_a_re, ssm_a_im, ssm_log_step, ssm_b_re, ssm_b_im, ssm_c_re, ssm_c_im, ssm_d, glu_w, glu_b, na_rpb, ffn_w1, ffn_w3, ffn_w2, moe_router, moe_w1, moe_w3, moe_w2, g_final):
    batch, seq, d = x.shape
    n_ctx = ctx.shape[1]
    depth = w_mod.shape[0]
    ssm_w = ssm_d.shape[1]
    att_w = (w_in.shape[2] - ssm_w) // 3
    rpb_rows = seq + n_ctx
    assert batch == SUBLANES and batch < MOD_ROWS
    assert seq % GRID_W == 0 and att_w % LANES == 0 and seq % SSM_CHUNK == 0 and n_ctx % SSM_CHUNK == 0
    tm = _token_tile(rpb_rows)
    tpb = rpb_rows // tm
    tme = 1024
    common = dict(tm=tm, tpb=tpb, seq=seq, ctx_row=batch)

    cvec = jnp.zeros((MOD_ROWS, d), F32).at[:batch].set(c.astype(F32)).at[batch].set(c_ctx.astype(F32))
    mod = _mod_table(cvec, w_mod.astype(F32), b_mod.astype(F32))
    x_all = jnp.concatenate([x, ctx], axis=1).astype(F32).reshape(batch * rpb_rows, d)


    ssm_mats = jax.vmap(_ssm_matrices)(ssm_a_re, ssm_a_im, ssm_log_step, ssm_b_re, ssm_b_im, ssm_c_re, ssm_c_im, ssm_d)

    for l in range(depth):
        mod_l = mod[l]
        u, q, k, v = _in_proj(x_all, g_mix[l].astype(F32), mod_l, w_in[l].astype(BF16), ssm_w=ssm_w, att_w=att_w,
                              **common)

        u_tm = u.reshape(batch, rpb_rows, ssm_w).transpose(1, 0, 2)
        y_tm = _ssm_mixer(u_tm, ssm_mats, seq=seq, n_ctx=n_ctx, layer=l)
        y_ssm = y_tm.transpose(1, 0, 2).reshape(batch * rpb_rows, ssm_w)

        bias = _attention_bias(na_rpb[l], seq // GRID_W)
        y_att = _attention(q, k, v, bias, batch=batch, rpb_rows=rpb_rows, seq=seq, n_ctx=n_ctx)

        x_all = _mix_out(y_ssm, y_att, x_all, mod_l, glu_w[l].astype(BF16), glu_b[l].astype(F32),
                         w_out[l].astype(BF16), **common)

        if l % 2 == 0:
            x_all = _dense_ffn(x_all, g_ffn[l].astype(F32), mod_l, ffn_w1[l // 2].astype(BF16),
                               ffn_w3[l // 2].astype(BF16), ffn_w2[l // 2].astype(BF16), **common)
        else:
            x_all = _moe_ffn(x_all, g_ffn[l].astype(F32), mod_l, moe_router[l // 2], moe_w1, moe_w3, moe_w2,
                             tme=tme, layer=l // 2, **common)

    return _final_norm(x_all.reshape(batch, rpb_rows, d), g_final.astype(F32), seq=seq).astype(x.dtype)
```

```python
import functools
import math

import jax
import jax.numpy as jnp
import numpy as np
from jax import lax
from jax.experimental import pallas as pl
from jax.experimental.pallas import tpu as pltpu

F32 = jnp.float32
BF16 = jnp.bfloat16

GRID_W = 64
SSM_GROUP = 16
SSM_STATE = 64
HEAD_DIM = 64
NA_ROWS_MAX = 8
NA_COLS = 16
N_EXPERTS = 8
NORM_EPS = 1e-6
A_RE_MAX = -1e-4
MASK_VALUE = -1e30

LANES = 128
SUBLANES = 8
V7X_VMEM_LIMIT_BYTES = 56 * 1024 * 1024

MOD_ROWS = 16


def _params(*semantics):
    return pltpu.CompilerParams(dimension_semantics=semantics, vmem_limit_bytes=V7X_VMEM_LIMIT_BYTES)


def _token_tile(rows_per_batch):
    for parts in range(1, rows_per_batch + 1):
        if rows_per_batch % parts == 0:
            tm = rows_per_batch // parts
            if tm <= 1152 and tm % 16 == 0:
                return tm
    raise ValueError("no token tile for %d rows" % rows_per_batch)


def _chunk(total, target):
    best = None
    for c in range(LANES, min(total, target) + 1, LANES):
        if total % c == 0:
            best = c
    if best is None:
        raise ValueError("no lane-aligned chunk for %d" % total)
    return best


def _mod_kernel(c_ref, w_ref, b_ref, o_ref):
    a = jax.nn.silu(c_ref[...])
    o_ref[...] = jnp.dot(a, w_ref[...], preferred_element_type=F32) + b_ref[...]


def _mod_table(cvec, w_mod, b_mod):
    depth, d, n = w_mod.shape
    tn = _chunk(n, 1536)
    return pl.pallas_call(
        _mod_kernel,
        grid=(depth, n // tn),
        in_specs=[
            pl.BlockSpec((MOD_ROWS, d), lambda l, j: (0, 0)),
            pl.BlockSpec((None, d, tn), lambda l, j: (l, 0, j)),
            pl.BlockSpec((None, 1, tn), lambda l, j: (l, 0, j)),
        ],
        out_specs=pl.BlockSpec((None, MOD_ROWS, tn), lambda l, j: (l, 0, j)),
        out_shape=jax.ShapeDtypeStruct((depth, MOD_ROWS, n), F32),
        compiler_params=_params("arbitrary", "arbitrary"),
        name="mod_table",
    )(cvec, w_mod, b_mod.reshape(depth, 1, n))


def _mod_rows(mod_ref, slab, tile_idx, tm, tpb, seq, ctx_row, d):
    b = tile_idx // tpb
    v_b = mod_ref[pl.ds(b, 1), slab * d:(slab + 1) * d]
    v_c = mod_ref[ctx_row:ctx_row + 1, slab * d:(slab + 1) * d]
    row = (tile_idx % tpb) * tm + lax.broadcasted_iota(jnp.int32, (tm, 1), 0)
    return jnp.where(row >= seq, v_c, v_b)


def _norm_mod(x, g, mod_ref, slab, tile_idx, tpb, seq, ctx_row):
    tm, d = x.shape
    shift = _mod_rows(mod_ref, slab, tile_idx, tm, tpb, seq, ctx_row, d)
    scale = _mod_rows(mod_ref, slab + 1, tile_idx, tm, tpb, seq, ctx_row, d)
    rs = lax.rsqrt(jnp.mean(x * x, axis=-1, keepdims=True) + NORM_EPS)
    return (x * rs) * g * (1.0 + scale) + shift


def _in_kernel(x_ref, g_ref, mod_ref, w_ref, u_ref, q_ref, k_ref, v_ref, *, tpb, seq, ctx_row, ssm_w, att_w):
    i = pl.program_id(0)
    h = _norm_mod(x_ref[...], g_ref[...], mod_ref, 0, i, tpb, seq, ctx_row).astype(BF16)
    c1 = ssm_w + att_w
    c2 = c1 + att_w
    u_ref[...] = jnp.dot(h, w_ref[:, 0:ssm_w], preferred_element_type=F32)
    q_ref[...] = (jnp.dot(h, w_ref[:, ssm_w:c1], preferred_element_type=F32) * (HEAD_DIM ** -0.5)).astype(BF16)
    k_ref[...] = jnp.dot(h, w_ref[:, c1:c2], preferred_element_type=F32).astype(BF16)
    v_ref[...] = jnp.dot(h, w_ref[:, c2:c2 + att_w], preferred_element_type=F32).astype(BF16)


def _in_proj(x_all, g, mod_l, w_in, *, tm, tpb, seq, ctx_row, ssm_w, att_w):
    t, d = x_all.shape
    n = w_in.shape[1]
    row_block = lambda width: pl.BlockSpec((tm, width), lambda i: (i, 0))
    return pl.pallas_call(
        functools.partial(_in_kernel, tpb=tpb, seq=seq, ctx_row=ctx_row, ssm_w=ssm_w, att_w=att_w),
        grid=(t // tm,),
        in_specs=[
            row_block(d),
            pl.BlockSpec((1, d), lambda i: (0, 0)),
            pl.BlockSpec(mod_l.shape, lambda i: (0, 0)),
            pl.BlockSpec((d, n), lambda i: (0, 0)),
        ],
        out_specs=[row_block(ssm_w), row_block(att_w), row_block(att_w), row_block(att_w)],
        out_shape=[
            jax.ShapeDtypeStruct((t, ssm_w), F32),
            jax.ShapeDtypeStruct((t, att_w), BF16),
            jax.ShapeDtypeStruct((t, att_w), BF16),
            jax.ShapeDtypeStruct((t, att_w), BF16),
        ],
        compiler_params=_params("arbitrary"),
        name="in_proj",
    )(x_all, g.reshape(1, d), mod_l, w_in)


def _att_kernel(q_ref, k_ref, v_ref, bias_ref, o_ref, plat_s, pctx_s, den_s, *, seq, n_ctx, rows, win, group):
    lane = lax.broadcasted_iota(jnp.int32, (1, LANES), 1)
    first_head = lane < HEAD_DIM
    mask0 = first_head.astype(BF16)
    mask1 = 1 - mask0
    trans_b = (((1,), (1,)), ((), ()))
    lane_tiles = lambda arrs: [a[:, c:c + LANES] for a in arrs for c in range(0, a.shape[1], LANES)]

    def stack(q):
        return jnp.concatenate([q * mask0, q * mask1], axis=0)

    def unstack(o):
        m_rows = o.shape[0] // 2
        return jnp.where(first_head, o[:m_rows], o[m_rows:])

    def probabilities(q, parts):
        q2 = stack(q)
        scores = []
        for keys, bias in parts:
            s = lax.dot_general(q2, keys, trans_b, preferred_element_type=F32)
            scores.append(s if bias is None else s + bias)
        m = jnp.max(functools.reduce(jnp.maximum, lane_tiles(scores)), axis=-1, keepdims=True)
        probs = [jnp.exp(s - m) for s in scores]
        den = jnp.sum(functools.reduce(lambda a, b: a + b, lane_tiles(probs)), axis=-1, keepdims=True)
        return [p.astype(BF16) for p in probs], den

    def window_start(r):
        return pl.multiple_of(jnp.clip(r - win // 2, 0, rows - win) * GRID_W, GRID_W)

    def score_stage(r, slot):
        q0 = pl.multiple_of(r * GRID_W, GRID_W)
        r0 = jnp.clip(r - win // 2, 0, rows - win)
        kw = k_ref[pl.ds(window_start(r), win * GRID_W), :]
        kc = k_ref[seq:seq + n_ctx, :]
        (p_lat, p_ctx), den = probabilities(q_ref[pl.ds(q0, GRID_W), :], [(kw, bias_ref[r - r0]), (kc, None)])
        plat_s[slot] = p_lat
        pctx_s[slot] = p_ctx
        den_s[slot] = jnp.broadcast_to(den, den_s.shape[1:])

    def value_stage(r, slot):
        q0 = pl.multiple_of(r * GRID_W, GRID_W)
        vw = v_ref[pl.ds(window_start(r), win * GRID_W), :]
        vc = v_ref[seq:seq + n_ctx, :]
        o = (jnp.dot(plat_s[slot], vw, preferred_element_type=F32)
             + jnp.dot(pctx_s[slot], vc, preferred_element_type=F32)) / den_s[slot]
        o_ref[pl.ds(q0, GRID_W), :] = unstack(o).astype(o_ref.dtype)

    n_groups = rows // group
    for g in range(group):
        score_stage(g, g)

    def pipelined(it, carry):
        cur = (it % 2) * group
        nxt = group - cur
        for g in range(group):
            value_stage(it * group + g, cur + g)
        for g in range(group):
            score_stage((it + 1) * group + g, nxt + g)
        return carry

    lax.fori_loop(0, n_groups - 1, pipelined, 0)
    last = ((n_groups - 1) % 2) * group
    for g in range(group):
        value_stage((n_groups - 1) * group + g, last + g)

    kc = k_ref[seq:seq + n_ctx, :]
    vc = v_ref[seq:seq + n_ctx, :]
    (p_ctx,), den = probabilities(q_ref[seq:seq + n_ctx, :], [(kc, None)])
    o = jnp.dot(p_ctx, vc, preferred_element_type=F32) / den
    o_ref[seq:seq + n_ctx, :] = unstack(o).astype(o_ref.dtype)


def _attention(q, k, v, bias, *, batch, rpb_rows, seq, n_ctx):
    t, att_w = q.shape
    rows = seq // GRID_W
    win = min(NA_ROWS_MAX, rows)
    n_pairs = att_w // LANES
    group = 8
    assert rows % group == 0
    blk = pl.BlockSpec((rpb_rows, LANES), lambda b, p: (b, p))
    return pl.pallas_call(
        functools.partial(_att_kernel, seq=seq, n_ctx=n_ctx, rows=rows, win=win, group=group),
        grid=(batch, n_pairs),
        in_specs=[blk, blk, blk,
                  pl.BlockSpec((None, win, 2 * GRID_W, win * GRID_W), lambda b, p: (p, 0, 0, 0))],
        out_specs=blk,
        out_shape=jax.ShapeDtypeStruct((t, att_w), BF16),
        scratch_shapes=[
            pltpu.VMEM((2 * group, 2 * GRID_W, win * GRID_W), BF16),
            pltpu.VMEM((2 * group, 2 * GRID_W, n_ctx), BF16),
            pltpu.VMEM((2 * group, 2 * GRID_W, LANES), F32),
        ],
        compiler_params=_params("arbitrary", "arbitrary"),
        name="attention",
    )(q, k, v, bias)


def _attention_bias(rpb, rows):
    n_heads = rpb.shape[0]
    win = min(NA_ROWS_MAX, rows)
    col = jnp.arange(GRID_W)
    col_start = jnp.clip(col - NA_COLS // 2, 0, GRID_W - NA_COLS)
    kcol = jnp.arange(GRID_W)
    valid = (kcol[None, :] >= col_start[:, None]) & (kcol[None, :] < col_start[:, None] + NA_COLS)
    col_rel = kcol[None, :] - col[:, None] + NA_COLS - 1
    var = jnp.arange(win)
    rr = jnp.arange(win)
    row_rel = rr[None, :] - var[:, None] + NA_ROWS_MAX - 1
    by_row = rpb.astype(F32)[:, row_rel]
    pick = (col_rel[:, :, None] == jnp.arange(2 * NA_COLS - 1)[None, None, :]).astype(F32)
    by_row = by_row.reshape(n_heads // 2, 2, win, win, 2 * NA_COLS - 1)
    tab = jnp.einsum("pavrj,ckj->pvacrk", by_row, pick, precision=lax.Precision.HIGHEST)
    tab = jnp.where(valid[None, None, None, :, None, :], tab, MASK_VALUE)
    return tab.reshape(n_heads // 2, win, 2 * GRID_W, win * GRID_W)


SSM_CHUNK = 16
SLOT = SSM_GROUP
SLOTS = LANES // SLOT


def _slot_of(g, ti):
    return ((g % SLOTS) + (ti % SLOTS)) % SLOTS


def _ssm_pack_kernel(u_ref, t_ref, bre_ref, bim_ref, yi_ref, sre_ref, sim_ref):
    cb, ch, nb, width = u_ref.shape
    rows = cb * nb
    n_groups = width // SSM_GROUP
    lane = lax.broadcasted_iota(jnp.int32, (1, LANES), 1) // SLOT
    slot_mask = [lane == s for s in range(SLOTS)]
    rolled = [[pltpu.roll(u_ref[:, ti, :, j * LANES:(j + 1) * LANES].reshape(rows, LANES), (ti % SLOTS) * SLOT, axis=1)
               for j in range(width // LANES)] for ti in range(ch)]
    for g in range(n_groups):
        j = g // SLOTS
        tiles = []
        for q in range(ch // SLOTS):
            acc = jnp.zeros((rows, LANES), F32)
            for t8 in range(SLOTS):
                ti = q * SLOTS + t8
                acc = jnp.where(slot_mask[_slot_of(g, ti)], rolled[ti][j], acc)
            tiles.append(acc)
        ug = jnp.concatenate(tiles, axis=1).astype(BF16)
        yi_ref[g] = jnp.dot(ug, t_ref[g], preferred_element_type=F32)
        sre_ref[:, g * LANES:(g + 1) * LANES] = jnp.dot(ug, bre_ref[g], preferred_element_type=F32)
        sim_ref[:, g * LANES:(g + 1) * LANES] = jnp.dot(ug, bim_ref[g], preferred_element_type=F32)


def _ssm_carry_kernel(sre_ref, sim_ref, are_ref, aim_ref, hfre_ref, hfim_ref, hbre_ref, hbim_ref, *, nb, n_lat_ch,
                      n_ctx_ch):
    n_ch = n_lat_ch + n_ctx_ch
    lanes = sre_ref.shape[1]
    fwd = (lax.broadcasted_iota(jnp.int32, (1, lanes), 1) % LANES) < (LANES // 2)
    a_re = jnp.broadcast_to(are_ref[...], (nb, lanes))
    a_im = jnp.broadcast_to(aim_ref[...], (nb, lanes))

    def step(k, carry):
        h_re, h_im = carry
        cf = jnp.where(k < n_ctx_ch, n_lat_ch + k, k - n_ctx_ch)
        cb = n_ch - 1 - k
        rf = pl.ds(pl.multiple_of(cf * nb, nb), nb)
        rb = pl.ds(pl.multiple_of(cb * nb, nb), nb)
        hfre_ref[rf, :] = h_re
        hfim_ref[rf, :] = h_im
        hbre_ref[rb, :] = h_re
        hbim_ref[rb, :] = h_im
        s_re = jnp.where(fwd, sre_ref[rf, :], sre_ref[rb, :])
        s_im = jnp.where(fwd, sim_ref[rf, :], sim_ref[rb, :])
        return a_re * h_re - a_im * h_im + s_re, a_re * h_im + a_im * h_re + s_im

    zero = jnp.zeros((nb, lanes), F32)
    lax.fori_loop(0, n_ch, step, (zero, zero))


def _ssm_unpack_kernel(yi_ref, hfre_ref, hfim_ref, hbre_ref, hbim_ref, co_ref, y_ref):
    cb, ch, nb, width = y_ref.shape
    rows = cb * nb
    n_groups = width // SSM_GROUP
    lane = lax.broadcasted_iota(jnp.int32, (1, LANES), 1) // SLOT
    slot_mask = [lane == s for s in range(SLOTS)]
    packed = []
    for g in range(n_groups):
        sl = slice(g * LANES, (g + 1) * LANES)
        h = jnp.concatenate([hfre_ref[:, sl], hfim_ref[:, sl], hbre_ref[:, sl], hbim_ref[:, sl]], axis=1).astype(BF16)
        packed.append(yi_ref[g] + jnp.dot(h, co_ref[g], preferred_element_type=F32))
    for ti in range(ch):
        q = ti // SLOTS
        for j in range(width // LANES):
            acc = jnp.zeros((rows, LANES), F32)
            for g8 in range(SLOTS):
                g = j * SLOTS + g8
                acc = jnp.where(slot_mask[_slot_of(g, ti)], packed[g][:, q * LANES:(q + 1) * LANES], acc)
            tile = pltpu.roll(acc, (LANES - (ti % SLOTS) * SLOT) % LANES, axis=1)
            y_ref[:, ti, :, j * LANES:(j + 1) * LANES] = tile.reshape(cb, nb, LANES)


def _ssm_mixer(u_tm, mats, *, seq, n_ctx):
    total, nb, width = u_tm.shape
    t_mat, b_re, b_im, c_out, a_re, a_im = mats
    n_groups = width // SSM_GROUP
    n_ch = total // SSM_CHUNK
    cblk = next(c for c in (17, 16, 8, 4, 2, 1) if n_ch % c == 0)
    rows = cblk * nb
    u4 = u_tm.reshape(n_ch, SSM_CHUNK, nb, width)
    state_w = n_groups * LANES
    resident = lambda a: pl.BlockSpec(a.shape, lambda i: (0,) * a.ndim, pipeline_mode=pl.Buffered(1))
    seq_block = pl.BlockSpec((cblk, SSM_CHUNK, nb, width), lambda i: (i, 0, 0, 0))
    packed_block = pl.BlockSpec((n_groups, rows, 2 * LANES), lambda i: (0, i, 0))
    state_block = pl.BlockSpec((rows, state_w), lambda i: (i, 0))
    state_shape = jax.ShapeDtypeStruct((n_ch * nb, state_w), F32)
    y_intra, s_re, s_im = pl.pallas_call(
        _ssm_pack_kernel,
        grid=(n_ch // cblk,),
        in_specs=[seq_block, resident(t_mat), resident(b_re), resident(b_im)],
        out_specs=[packed_block, state_block, state_block],
        out_shape=[jax.ShapeDtypeStruct((n_groups, n_ch * nb, 2 * LANES), F32), state_shape, state_shape],
        compiler_params=_params("arbitrary"),
        name="ssm_pack",
    )(u4, t_mat, b_re, b_im)

    lane_blk = 2 * LANES
    col_block = pl.BlockSpec((n_ch * nb, lane_blk), lambda i: (0, i))
    vec_block = pl.BlockSpec((1, lane_blk), lambda i: (0, i))
    h_states = pl.pallas_call(
        functools.partial(_ssm_carry_kernel, nb=nb, n_lat_ch=seq // SSM_CHUNK, n_ctx_ch=n_ctx // SSM_CHUNK),
        grid=(state_w // lane_blk,),
        in_specs=[col_block, col_block, vec_block, vec_block],
        out_specs=[col_block] * 4,
        out_shape=[state_shape] * 4,
        compiler_params=_params("arbitrary"),
        name="ssm_carry",
    )(s_re, s_im, a_re, a_im)

    y4 = pl.pallas_call(
        _ssm_unpack_kernel,
        grid=(n_ch // cblk,),
        in_specs=[packed_block] + [state_block] * 4 + [resident(c_out)],
        out_specs=seq_block,
        out_shape=jax.ShapeDtypeStruct(u4.shape, F32),
        compiler_params=_params("arbitrary"),
        name="ssm_unpack",
    )(y_intra, *h_states, c_out)
    return y4.reshape(total, nb, width)


def _ssm_matrices(a_re, a_im, log_step, b_re, b_im, c_re, c_im, d_skip):
    hp = lax.Precision.HIGHEST
    _, g, p, h = b_re.shape
    ch = SSM_CHUNK
    a_re = jnp.minimum(a_re.astype(F32), A_RE_MAX)
    a_im = a_im.astype(F32)
    dt = jnp.exp(log_step.astype(F32))[..., None]
    k = jnp.arange(ch + 1, dtype=F32)[:, None, None, None]
    mag = jnp.exp(a_re * dt * k)
    lr = mag * jnp.cos(a_im * dt * k)
    li = mag * jnp.sin(a_im * dt * k)
    lam_re, lam_im = lr[1], li[1]
    den = a_re * a_re + a_im * a_im
    z_re = ((lam_re - 1) * a_re + lam_im * a_im) / den
    z_im = (lam_im * a_re - (lam_re - 1) * a_im) / den
    b_re = b_re.astype(F32)
    b_im = b_im.astype(F32)
    bb_re = z_re[..., None] * b_re - z_im[..., None] * b_im
    bb_im = z_re[..., None] * b_im + z_im[..., None] * b_re
    c_re = c_re.astype(F32)
    c_im = c_im.astype(F32)
    w_re = c_re[None] * lr[:, :, :, None, :] - c_im[None] * li[:, :, :, None, :]
    w_im = c_re[None] * li[:, :, :, None, :] + c_im[None] * lr[:, :, :, None, :]
    kern = (jnp.einsum("dxgop,xgpi->dxgoi", w_re, bb_re, precision=hp)
            - jnp.einsum("dxgop,xgpi->dxgoi", w_im, bb_im, precision=hp))
    n_tile = g // SLOTS
    u_idx = np.arange(ch)
    nat = np.stack([(u_idx // SLOTS) * SLOTS + ((u_idx % SLOTS) - lo) % SLOTS for lo in range(SLOTS)])
    unpack = (nat[:, :, None] == np.arange(ch)[None, None, :]).astype(np.float32)
    lag = nat[:, None, :] - nat[:, :, None]
    d_idx = np.arange(ch + 1)
    sel_f = (lag[..., None] == d_idx).astype(np.float32)
    sel_b = (-lag[..., None] == d_idx).astype(np.float32)
    by_tile = lambda a: a.reshape(a.shape[:1] + (n_tile, SLOTS) + a.shape[2:])
    toe = (jnp.einsum("lutd,dnloi->nluito", sel_f, by_tile(kern[:, 0]), precision=hp)
           + jnp.einsum("lutd,dnloi->nluito", sel_b, by_tile(kern[:, 1]), precision=hp))
    skip_lane = jnp.tile(d_skip.astype(F32).reshape(g, h), (1, ch))
    t_mat = toe.reshape(g, ch * h, ch * h) + jnp.eye(ch * h, dtype=F32)[None] * skip_lane[:, None, :]

    cm = lambda ar, ai, br, bi: (ar * br - ai * bi, ar * bi + ai * br)
    mf_re, mf_im = cm(jnp.flip(lr[:ch, 0], 0)[..., None], jnp.flip(li[:ch, 0], 0)[..., None], bb_re[0][None],
                      bb_im[0][None])
    mb_re, mb_im = cm(lr[:ch, 1][..., None], li[:ch, 1][..., None], bb_re[1][None], bb_im[1][None])
    pack_state = lambda f, b: jnp.einsum("lus,snlpi->nluip", unpack.astype(BF16),
                                         by_tile(jnp.concatenate([f, b], axis=2)).astype(BF16),
                                         preferred_element_type=F32).reshape(g, ch * h, 2 * p)
    bs_re = pack_state(mf_re, mb_re)
    bs_im = pack_state(mf_im, mb_im)

    rd = lambda w: jnp.einsum("lut,tnlop->nlpuo", unpack.astype(BF16), by_tile(w).astype(BF16),
                              preferred_element_type=F32).reshape(g, p, ch * h)
    fwd_pow = lambda w: rd(w[1:ch + 1, 0])
    bwd_pow = lambda w: rd(jnp.flip(w[1:ch + 1, 1], 0))
    zeros = jnp.zeros((g, p, ch * h), F32)
    co = jnp.concatenate([fwd_pow(w_re), zeros, -fwd_pow(w_im), zeros,
                          zeros, bwd_pow(w_re), zeros, -bwd_pow(w_im)], axis=1)

    a16_re = jnp.concatenate([lr[ch, 0], lr[ch, 1]], axis=1).reshape(1, g * 2 * p)
    a16_im = jnp.concatenate([li[ch, 0], li[ch, 1]], axis=1).reshape(1, g * 2 * p)
    return t_mat.astype(BF16), bs_re.astype(BF16), bs_im.astype(BF16), co.astype(BF16), a16_re, a16_im


def _mix_kernel(y_ref, att_ref, x_ref, mod_ref, gw_ref, gb_ref, wo_ref, o_ref, *, tpb, seq, ctx_row):
    i = pl.program_id(0)
    tm, d = x_ref.shape
    ssm_w = y_ref.shape[1]
    g = jax.nn.gelu(y_ref[...])
    z = jnp.dot(g.astype(BF16), gw_ref[...], preferred_element_type=F32) + gb_ref[...]
    s = g * jax.nn.sigmoid(z)
    o = (jnp.dot(s.astype(BF16), wo_ref[0:ssm_w, :], preferred_element_type=F32)
         + jnp.dot(att_ref[...], wo_ref[ssm_w:, :], preferred_element_type=F32))
    gate = _mod_rows(mod_ref, 2, i, tm, tpb, seq, ctx_row, d)
    o_ref[...] = x_ref[...] + gate * o


def _mix_out(y_ssm, y_att, x_all, mod_l, glu_w, glu_b, w_out, *, tm, tpb, seq, ctx_row):
    t, d = x_all.shape
    ssm_w = y_ssm.shape[1]
    att_w = y_att.shape[1]
    row_block = lambda width: pl.BlockSpec((tm, width), lambda i: (i, 0))
    whole = lambda a: pl.BlockSpec(a.shape, lambda i: (0, 0))
    gb = glu_b.reshape(1, ssm_w)
    return pl.pallas_call(
        functools.partial(_mix_kernel, tpb=tpb, seq=seq, ctx_row=ctx_row),
        grid=(t // tm,),
        in_specs=[row_block(ssm_w), row_block(att_w), row_block(d), whole(mod_l), whole(glu_w), whole(gb),
                  whole(w_out)],
        out_specs=row_block(d),
        out_shape=jax.ShapeDtypeStruct((t, d), F32),
        compiler_params=_params("arbitrary"),
        name="mix_out",
    )(y_ssm, y_att, x_all, mod_l, glu_w, gb, w_out)


def _ffn_kernel(x_ref, g_ref, mod_ref, w1_ref, w3_ref, w2_ref, o_ref, t_s, acc_s, *, tpb, seq, ctx_row):
    i = pl.program_id(0)
    tm, d = x_ref.shape
    nf = w1_ref.shape[0]
    t_s[...] = _norm_mod(x_ref[...], g_ref[...], mod_ref, 3, i, tpb, seq, ctx_row).astype(BF16)
    acc_s[...] = jnp.zeros_like(acc_s)

    def chunk(c, carry):
        t = t_s[...]
        h1 = jnp.dot(t, w1_ref[c], preferred_element_type=F32)
        h3 = jnp.dot(t, w3_ref[c], preferred_element_type=F32)
        a = (jax.nn.silu(h1) * h3).astype(BF16)
        acc_s[...] += jnp.dot(a, w2_ref[c], preferred_element_type=F32)
        return carry

    lax.fori_loop(0, nf, chunk, 0, unroll=2)
    gate = _mod_rows(mod_ref, 5, i, tm, tpb, seq, ctx_row, d)
    o_ref[...] = x_ref[...] + gate * acc_s[...]


def _dense_ffn(x_all, g, mod_l, w1, w3, w2, *, tm, tpb, seq, ctx_row):
    t, d = x_all.shape
    dff = w1.shape[1]
    tf = _chunk(dff, 256)
    nf = dff // tf
    w1c = w1.reshape(d, nf, tf).transpose(1, 0, 2)
    w3c = w3.reshape(d, nf, tf).transpose(1, 0, 2)
    w2c = w2.reshape(nf, tf, d)
    resident = lambda a: pl.BlockSpec(a.shape, lambda i: (0,) * a.ndim, pipeline_mode=pl.Buffered(1))
    return pl.pallas_call(
        functools.partial(_ffn_kernel, tpb=tpb, seq=seq, ctx_row=ctx_row),
        grid=(t // tm,),
        in_specs=[
            pl.BlockSpec((tm, d), lambda i: (i, 0)),
            pl.BlockSpec((1, d), lambda i: (0, 0)),
            pl.BlockSpec(mod_l.shape, lambda i: (0, 0)),
            resident(w1c), resident(w3c), resident(w2c),
        ],
        out_specs=pl.BlockSpec((tm, d), lambda i: (i, 0)),
        out_shape=jax.ShapeDtypeStruct((t, d), F32),
        scratch_shapes=[pltpu.VMEM((tm, d), BF16), pltpu.VMEM((tm, d), F32)],
        compiler_params=_params("arbitrary"),
        name="dense_ffn",
    )(x_all, g.reshape(1, d), mod_l, w1c, w3c, w2c)


def _route_kernel(x_ref, g_ref, mod_ref, r_ref, t_ref, route_ref, *, tpb, seq, ctx_row):
    i = pl.program_id(0)
    t = _norm_mod(x_ref[...], g_ref[...], mod_ref, 3, i, tpb, seq, ctx_row)
    t_ref[...] = t
    th = t.astype(BF16)
    tl = (t - th.astype(F32)).astype(BF16)
    r = r_ref[...]
    rh = r.astype(BF16)
    rl = (r - rh.astype(F32)).astype(BF16)
    dot = lambda a, b: jnp.dot(a, b, preferred_element_type=F32)
    logits = dot(th, rh) + (dot(th, rl) + dot(tl, rh)) + dot(tl, rl)
    tm = logits.shape[0]
    lane = lax.broadcasted_iota(jnp.int32, (tm, LANES), 1)
    lane_f = lane.astype(F32)
    neg_inf = jnp.float32(-jnp.inf)
    lg = jnp.where(lane < N_EXPERTS, logits, neg_inf)
    m1 = jnp.max(lg, axis=-1, keepdims=True)
    i1 = jnp.min(jnp.where(lg == m1, lane_f, float(LANES)), axis=-1, keepdims=True)
    lg2 = jnp.where(lane_f == i1, neg_inf, lg)
    m2 = jnp.max(lg2, axis=-1, keepdims=True)
    i2 = jnp.min(jnp.where(lg2 == m2, lane_f, float(LANES)), axis=-1, keepdims=True)
    e = jnp.exp(m2 - m1)
    g1 = 1.0 / (1.0 + e)
    g2 = e / (1.0 + e)
    route_ref[...] = jnp.where(lane == 0, i1, jnp.where(lane == 1, i2, jnp.where(lane == 2, g1, jnp.where(
        lane == 3, g2, 0.0))))


def _route(x_all, g, mod_l, router, *, tm, tpb, seq, ctx_row):
    t, d = x_all.shape
    r_pad = jnp.zeros((d, LANES), F32).at[:, :router.shape[1]].set(router.astype(F32))
    return pl.pallas_call(
        functools.partial(_route_kernel, tpb=tpb, seq=seq, ctx_row=ctx_row),
        grid=(t // tm,),
        in_specs=[
            pl.BlockSpec((tm, d), lambda i: (i, 0)),
            pl.BlockSpec((1, d), lambda i: (0, 0)),
            pl.BlockSpec(mod_l.shape, lambda i: (0, 0)),
            pl.BlockSpec((d, LANES), lambda i: (0, 0)),
        ],
        out_specs=[pl.BlockSpec((tm, d), lambda i: (i, 0)), pl.BlockSpec((tm, LANES), lambda i: (i, 0))],
        out_shape=[jax.ShapeDtypeStruct((t, d), F32), jax.ShapeDtypeStruct((t, LANES), F32)],
        compiler_params=_params("arbitrary"),
        name="moe_route",
    )(x_all, g.reshape(1, d), mod_l, r_pad)


def _row_copy(src_hbm, dst_vmem, sem, src_row, dst_row):
    return pltpu.make_async_copy(src_hbm.at[pl.ds(src_row, 1), :], dst_vmem.at[pl.ds(dst_row, 1), :], sem)


ISSUE_UNROLL = 8


ZERO_BLOCK_ROWS = 256


def _dispatch_kernel(fill_lo_ref, fill_hi_ref, pos_ref, t_ref, xs_hbm, zero_s, sems):
    i = pl.program_id(0)
    tm = t_ref.shape[0]
    per_trip = ISSUE_UNROLL // 2

    def issue(g, carry):
        for k in range(per_trip):
            r = g * per_trip + k
            for choice in range(2):
                pltpu.make_async_copy(t_ref.at[pl.ds(r, 1), :], xs_hbm.at[pl.ds(pos_ref[0, 0, 2 * r + choice], 1), :],
                                      sems.at[choice]).start()
        return carry

    lax.fori_loop(0, tm // per_trip, issue, 0)

    @pl.when(i == 0)
    def _():
        zero_s[...] = jnp.zeros_like(zero_s)
        zero_row = lambda row: pltpu.make_async_copy(zero_s.at[pl.ds(0, 1), :], xs_hbm.at[pl.ds(row, 1), :],
                                                     sems.at[2])
        zero_block = lambda blk: pltpu.make_async_copy(
            zero_s, xs_hbm.at[pl.ds(pl.multiple_of(blk * ZERO_BLOCK_ROWS, ZERO_BLOCK_ROWS), ZERO_BLOCK_ROWS), :],
            sems.at[2])

        def start_then_wait(copy, lo, hi):
            def start(k, carry):
                copy(k).start()
                return carry

            def wait(k, carry):
                copy(k).wait()
                return carry

            lax.fori_loop(lo, hi, start, 0)
            lax.fori_loop(lo, hi, wait, 0)

        for e in range(N_EXPERTS):
            start_then_wait(zero_row, fill_lo_ref[e], fill_hi_ref[e])
        start_then_wait(zero_block, fill_lo_ref[N_EXPERTS] // ZERO_BLOCK_ROWS,
                        fill_hi_ref[N_EXPERTS] // ZERO_BLOCK_ROWS)

    for choice in range(2):
        pltpu.make_async_copy(t_ref, xs_hbm.at[pl.ds(0, tm), :], sems.at[choice]).wait()


def _dispatch_rows(tokens, pos, fill_lo, fill_hi, *, tm, n_rows):
    t, d = tokens.shape
    steps = t // tm
    assert tm % (ISSUE_UNROLL // 2) == 0 and n_rows % ZERO_BLOCK_ROWS == 0
    grid_spec = pltpu.PrefetchScalarGridSpec(
        num_scalar_prefetch=2,
        grid=(steps,),
        in_specs=[
            pl.BlockSpec((1, 1, 2 * tm), lambda i, lo, hi: (i, 0, 0), memory_space=pltpu.SMEM),
            pl.BlockSpec((tm, d), lambda i, lo, hi: (i, 0)),
        ],
        out_specs=pl.BlockSpec(memory_space=pl.ANY),
        scratch_shapes=[pltpu.VMEM((ZERO_BLOCK_ROWS, d), tokens.dtype), pltpu.SemaphoreType.DMA((3,))],
    )
    return pl.pallas_call(
        _dispatch_kernel,
        grid_spec=grid_spec,
        out_shape=jax.ShapeDtypeStruct((n_rows, d), tokens.dtype),
        compiler_params=_params("arbitrary"),
        name="moe_dispatch",
    )(fill_lo, fill_hi, pos.reshape(steps, 1, 2 * tm), tokens)


def _gmm_kernel(te_ref, tv_ref, ts_ref, xs_ref, w1_ref, w3_ref, w2_ref, o_ref, xb_s, acc_s, *, nf, n_sub):
    j = pl.program_id(0)
    f = pl.program_id(1)
    valid = tv_ref[j] > 0

    @pl.when(valid)
    def _():
        @pl.when(f == 0)
        def _():
            xb_s[...] = xs_ref[...].astype(BF16)
            acc_s[...] = jnp.zeros_like(acc_s)

        x = xb_s[...]
        tf = w1_ref.shape[1]
        sub = tf // n_sub
        part = None
        for c0 in range(0, tf, sub):
            h1 = jnp.dot(x, w1_ref[:, c0:c0 + sub].astype(BF16), preferred_element_type=F32)
            h3 = jnp.dot(x, w3_ref[:, c0:c0 + sub].astype(BF16), preferred_element_type=F32)
            a = (jax.nn.silu(h1) * h3).astype(BF16)
            y = jnp.dot(a, w2_ref[c0:c0 + sub, :].astype(BF16), preferred_element_type=F32)
            part = y if part is None else part + y
        acc_s[...] += part

        @pl.when(f == nf - 1)
        def _():
            o_ref[...] = acc_s[...]

    @pl.when(jnp.logical_and(jnp.logical_not(valid), f == nf - 1))
    def _():
        o_ref[...] = jnp.zeros_like(o_ref)


def _grouped_swiglu(xs, tile_expert, tile_valid, tile_src, w1, w3, w2, *, tme, layer):
    n, d = xs.shape
    dff = w1.shape[3]
    tf = _chunk(dff, 512)
    nf = dff // tf
    grid_spec = pltpu.PrefetchScalarGridSpec(
        num_scalar_prefetch=3,
        grid=(n // tme, nf),
        in_specs=[
            pl.BlockSpec((tme, d), lambda j, f, te, tv, ts: (ts[j], 0)),
            pl.BlockSpec((None, None, d, tf), lambda j, f, te, tv, ts: (layer, te[j], 0, f)),
            pl.BlockSpec((None, None, d, tf), lambda j, f, te, tv, ts: (layer, te[j], 0, f)),
            pl.BlockSpec((None, None, tf, d), lambda j, f, te, tv, ts: (layer, te[j], f, 0)),
        ],
        out_specs=pl.BlockSpec((tme, d), lambda j, f, te, tv, ts: (j, 0)),
        scratch_shapes=[pltpu.VMEM((tme, d), BF16), pltpu.VMEM((tme, d), F32)],
    )
    return pl.pallas_call(
        functools.partial(_gmm_kernel, nf=nf, n_sub=2),
        grid_spec=grid_spec,
        out_shape=jax.ShapeDtypeStruct((n, d), F32),
        compiler_params=_params("arbitrary", "arbitrary"),
        name="moe_experts",
    )(tile_expert, tile_valid, tile_src, xs, w1, w3, w2)


def _combine_kernel(pos_ref, pos_next_ref, ys_hbm, x_ref, route_ref, mod_ref, o_ref, buf, sems, *, tpb, seq,
                    ctx_row):
    i = pl.program_id(0)
    tm, d = x_ref.shape

    def issue_all(pos, slot):
        def issue(g, carry):
            for k in range(ISSUE_UNROLL // 2):
                r = g * (ISSUE_UNROLL // 2) + k
                _row_copy(ys_hbm, buf.at[slot, 0], sems.at[slot, 0], pos[0, 0, 2 * r], r).start()
                _row_copy(ys_hbm, buf.at[slot, 1], sems.at[slot, 1], pos[0, 0, 2 * r + 1], r).start()
            return carry

        lax.fori_loop(0, tm // (ISSUE_UNROLL // 2), issue, 0)

    @pl.when(i == 0)
    def _():
        issue_all(pos_ref, 0)

    @pl.when(i + 1 < pl.num_programs(0))
    def _():
        issue_all(pos_next_ref, (i + 1) % 2)

    slot = i % 2
    pltpu.make_async_copy(ys_hbm.at[pl.ds(0, tm), :], buf.at[slot, 0], sems.at[slot, 0]).wait()
    pltpu.make_async_copy(ys_hbm.at[pl.ds(0, tm), :], buf.at[slot, 1], sems.at[slot, 1]).wait()
    g1 = route_ref[:, 2:3]
    g2 = route_ref[:, 3:4]
    gate = _mod_rows(mod_ref, 5, i, tm, tpb, seq, ctx_row, d)
    o_ref[...] = x_ref[...] + gate * (g1 * buf[slot, 0] + g2 * buf[slot, 1])


def _combine(ys, pos, x_all, route, mod_l, *, tm, tpb, seq, ctx_row):
    t, d = x_all.shape
    steps = t // tm
    assert tm % (ISSUE_UNROLL // 2) == 0
    pos3 = pos.reshape(steps, 1, 2 * tm)
    return pl.pallas_call(
        functools.partial(_combine_kernel, tpb=tpb, seq=seq, ctx_row=ctx_row),
        grid=(steps,),
        in_specs=[
            pl.BlockSpec((1, 1, 2 * tm), lambda i: (i, 0, 0), memory_space=pltpu.SMEM),
            pl.BlockSpec((1, 1, 2 * tm), lambda i: (jnp.minimum(i + 1, steps - 1), 0, 0), memory_space=pltpu.SMEM),
            pl.BlockSpec(memory_space=pl.ANY),
            pl.BlockSpec((tm, d), lambda i: (i, 0)),
            pl.BlockSpec((tm, LANES), lambda i: (i, 0)),
            pl.BlockSpec(mod_l.shape, lambda i: (0, 0)),
        ],
        out_specs=pl.BlockSpec((tm, d), lambda i: (i, 0)),
        out_shape=jax.ShapeDtypeStruct((t, d), F32),
        scratch_shapes=[pltpu.VMEM((2, 2, tm, d), F32), pltpu.SemaphoreType.DMA((2, 2))],
        compiler_params=_params("arbitrary"),
        name="moe_combine",
    )(pos3, pos3, ys, x_all, route, mod_l)


def _moe_ffn(x_all, g, mod_l, router, w1, w3, w2, *, tm, tpb, seq, ctx_row, tme, layer):
    t = x_all.shape[0]
    tokens, route = _route(x_all, g, mod_l, router, tm=tm, tpb=tpb, seq=seq, ctx_row=ctx_row)
    e_flat = route[:, 0:2].astype(jnp.int32).reshape(-1)
    onehot = (e_flat[:, None] == jnp.arange(N_EXPERTS, dtype=jnp.int32)[None, :]).astype(jnp.int32)
    csum = jnp.cumsum(onehot, axis=0)
    rank = jnp.take_along_axis(csum, e_flat[:, None], axis=1)[:, 0] - 1
    counts = csum[-1]
    padded = ((counts + tme - 1) // tme) * tme
    seg_end = jnp.cumsum(padded)
    seg_start = seg_end - padded
    pos = seg_start[e_flat] + rank
    n_tiles = -(-(2 * t + N_EXPERTS * (tme - 1)) // tme)
    tile_start = jnp.arange(n_tiles, dtype=jnp.int32) * tme
    tile_expert = jnp.minimum(jnp.sum((tile_start[:, None] >= seg_end[None, :]).astype(jnp.int32), axis=1),
                              N_EXPERTS - 1)
    tile_valid = (tile_start < seg_end[-1]).astype(jnp.int32)
    tile_src = jnp.minimum(jnp.arange(n_tiles, dtype=jnp.int32), seg_end[-1] // tme - 1)

    n_rows = n_tiles * tme
    fill_lo = jnp.concatenate([seg_start + counts, seg_end[-1:]]).astype(jnp.int32)
    fill_hi = jnp.concatenate([seg_end, jnp.full((1,), n_rows, jnp.int32)]).astype(jnp.int32)
    xs = _dispatch_rows(tokens, pos, fill_lo, fill_hi, tm=tm, n_rows=n_rows)
    ys = _grouped_swiglu(xs, tile_expert, tile_valid, tile_src, w1, w3, w2, tme=tme, layer=layer)
    return _combine(ys, pos, x_all, route, mod_l, tm=tm, tpb=tpb, seq=seq, ctx_row=ctx_row)


def _final_kernel(x_ref, g_ref, o_ref):
    x = x_ref[...]
    o_ref[...] = x * lax.rsqrt(jnp.mean(x * x, axis=-1, keepdims=True) + NORM_EPS) * g_ref[...]


def _final_norm(x3, g, *, seq):
    batch, _, d = x3.shape
    tr = _chunk(seq, 1024)
    return pl.pallas_call(
        _final_kernel,
        grid=(batch, seq // tr),
        in_specs=[pl.BlockSpec((None, tr, d), lambda b, j: (b, j, 0)), pl.BlockSpec((1, d), lambda b, j: (0, 0))],
        out_specs=pl.BlockSpec((None, tr, d), lambda b, j: (b, j, 0)),
        out_shape=jax.ShapeDtypeStruct((batch, seq, d), F32),
        compiler_params=_params("arbitrary", "arbitrary"),
        name="final_norm",
    )(x3, g.reshape(1, d))


def kernel(x, c, ctx, c_ctx, w_mod, b_mod, g_mix, g_ffn, w_in, w_out, ssm_a_re, ssm_a_im, ssm_log_step, ssm_b_re, ssm_b_im, ssm_c_re, ssm_c_im, ssm_d, glu_w, glu_b, na_rpb, ffn_w1, ffn_w3, ffn_w2, moe_router, moe_w1, moe_w3, moe_w2, g_final):
    batch, seq, d = x.shape
    n_ctx = ctx.shape[1]
    depth = w_mod.shape[0]
    ssm_w = ssm_d.shape[1]
    att_w = (w_in.shape[2] - ssm_w) // 3
    rpb_rows = seq + n_ctx
    assert batch == SUBLANES and batch < MOD_ROWS
    assert seq % GRID_W == 0 and att_w % LANES == 0 and seq % SSM_CHUNK == 0 and n_ctx % SSM_CHUNK == 0
    tm = _token_tile(rpb_rows)
    tpb = rpb_rows // tm
    tme = 1024
    common = dict(tm=tm, tpb=tpb, seq=seq, ctx_row=batch)

    cvec = jnp.zeros((MOD_ROWS, d), F32).at[:batch].set(c.astype(F32)).at[batch].set(c_ctx.astype(F32))
    mod = _mod_table(cvec, w_mod.astype(F32), b_mod.astype(F32))
    x_all = jnp.concatenate([x, ctx], axis=1).astype(F32).reshape(batch * rpb_rows, d)


    for l in range(depth):
        mod_l = mod[l]
        u, q, k, v = _in_proj(x_all, g_mix[l].astype(F32), mod_l, w_in[l].astype(BF16), ssm_w=ssm_w, att_w=att_w,
                              **common)

        u_tm = u.reshape(batch, rpb_rows, ssm_w).transpose(1, 0, 2)
        mats = _ssm_matrices(ssm_a_re[l], ssm_a_im[l], ssm_log_step[l], ssm_b_re[l], ssm_b_im[l], ssm_c_re[l],
                             ssm_c_im[l], ssm_d[l])
        y_tm = _ssm_mixer(u_tm, mats, seq=seq, n_ctx=n_ctx)
        y_ssm = y_tm.transpose(1, 0, 2).reshape(batch * rpb_rows, ssm_w)

        bias = _attention_bias(na_rpb[l], seq // GRID_W)
        y_att = _attention(q, k, v, bias, batch=batch, rpb_rows=rpb_rows, seq=seq, n_ctx=n_ctx)

        x_all = _mix_out(y_ssm, y_att, x_all, mod_l, glu_w[l].astype(BF16), glu_b[l].astype(F32),
                         w_out[l].astype(BF16), **common)

        if l % 2 == 0:
            x_all = _dense_ffn(x_all, g_ffn[l].astype(F32), mod_l, ffn_w1[l // 2].astype(BF16),
                               ffn_w3[l // 2].astype(BF16), ffn_w2[l // 2].astype(BF16), **common)
        else:
            x_all = _moe_ffn(x_all, g_ffn[l].astype(F32), mod_l, moe_router[l // 2], moe_w1, moe_w3, moe_w2,
                             tme=tme, layer=l // 2, **common)

    return _final_norm(x_all.reshape(batch, rpb_rows, d), g_final.astype(F32), seq=seq).astype(x.dtype)
```
